```python
import math
import jax
import jax.numpy as jnp
from jax import lax
import numpy as np

D_MODEL = 2048
BATCH = 1
SEQ = 8192
DEPTH = 4

GRID_W = 64
CTX_LEN = 256
HEAD_DIM = 128
AXIS_DIM = HEAD_DIM // 2
N_Q_HEADS = 8
N_KV_HEADS = 2
ATTN_W = N_Q_HEADS * HEAD_DIM
KV_W = N_KV_HEADS * HEAD_DIM
CONV_W = D_MODEL - ATTN_W
SHORT_K = 3
KV_LO = ATTN_W
KV_HI = ATTN_W + 2 * KV_W
EVEN_IN_W = KV_HI + 3 * CONV_W
HYENA_W = D_MODEL
FILTER_BANDS = 16
FILTER_EMB = 1 + 2 * FILTER_BANDS
FILTER_HIDDEN = 64
FILTER_MOD_SHIFT = 0.05
D_FF = 4 * D_MODEL
ROPE_THETA = 10000.0
Q_BLOCK = 128
EPS = 1e-6
N_EVEN = (DEPTH + 1) // 2
N_ODD = DEPTH // 2
LAST_CTX_READ = ((DEPTH - 1) // 2) * 2

kernel_name = 'hybrid_gqa_shortconv_hyena_dit'


def rms_norm(x, g):
    xf = x.astype(jnp.float32)
    y = xf * lax.rsqrt(jnp.mean(xf * xf, axis=-1, keepdims=True) + EPS)
    return (y * g.astype(jnp.float32)).astype(x.dtype)


def modulate(x, g, shift, scale):
    return rms_norm(x, g) * (1 + scale) + shift


def adaln(cond, w, b):
    return jnp.split(jax.nn.silu(cond) @ w + b, 6, axis=-1)


def split_heads(t, nh):
    return t.reshape(t.shape[:-1] + (nh, HEAD_DIM))


def axial_rope_tables(n):
    t = jnp.arange(n, dtype=jnp.int32)
    row = (t // GRID_W).astype(jnp.float32)
    col = (t % GRID_W).astype(jnp.float32)
    inv = 1.0 / (ROPE_THETA ** (jnp.arange(0, AXIS_DIM, 2, dtype=jnp.float32) / AXIS_DIM))
    ang = jnp.concatenate([row[:, None] * inv, col[:, None] * inv], axis=-1)
    return jnp.cos(ang), jnp.sin(ang)


def apply_axial_rope(x, rope):
    cos, sin = rope
    b, n, h, d = x.shape
    nf = d // 4
    xr = x.astype(jnp.float32).reshape(b, n, h, 2, 2, nf)
    x1, x2 = xr[..., 0, :], xr[..., 1, :]
    cs = cos.reshape(1, n, 1, 2, nf)
    sn = sin.reshape(1, n, 1, 2, nf)
    out = jnp.stack([x1 * cs - x2 * sn, x2 * cs + x1 * sn], axis=-2)
    return out.reshape(b, n, h, d).astype(x.dtype)


def attn_q(p, q_norm, rope):
    q = rms_norm(split_heads(p, N_Q_HEADS), q_norm)
    return q if rope is None else apply_axial_rope(q, rope)


def attn_kv(p, k_norm, rope):
    k = rms_norm(split_heads(p[..., :KV_W], N_KV_HEADS), k_norm)
    v = split_heads(p[..., KV_W:], N_KV_HEADS)
    return (k if rope is None else apply_axial_rope(k, rope)), v


def latent_attention(q, k, v, k_ctx, v_ctx):
    b, n, h, d = q.shape
    g = h // N_KV_HEADS
    nblk = n // Q_BLOCK
    qb = q.reshape(b, nblk, Q_BLOCK, N_KV_HEADS, g, d).transpose(1, 0, 2, 3, 4, 5)
    scale = d ** -0.5

    def block(qi):
        s_lat = jnp.einsum('bqkgd,bskd->bkgqs', qi, k)
        s_ctx = jnp.einsum('bqkgd,bckd->bkgqc', qi, k_ctx)
        s = jnp.concatenate([s_lat, s_ctx], axis=-1).astype(jnp.float32) * scale
        p = jax.nn.softmax(s, axis=-1).astype(v.dtype)
        return (jnp.einsum('bkgqs,bskd->bqkgd', p[..., :n], v)
                + jnp.einsum('bkgqc,bckd->bqkgd', p[..., n:], v_ctx))

    o = lax.map(block, qb)
    return o.transpose(1, 0, 2, 3, 4, 5).reshape(b, n, h * d)


def context_attention(q, k, v):
    b, n, h, d = q.shape
    g = h // N_KV_HEADS
    qg = q.reshape(b, n, N_KV_HEADS, g, d)
    s = jnp.einsum('bqkgd,bskd->bkgqs', qg, k).astype(jnp.float32) * (d ** -0.5)
    p = jax.nn.softmax(s, axis=-1).astype(v.dtype)
    return jnp.einsum('bkgqs,bskd->bqkgd', p, v).reshape(b, n, h * d)


def depthwise_conv(x, w):
    return lax.conv_general_dilated(
        x, w[:, None, :].astype(x.dtype), window_strides=(1,), padding='SAME',
        dimension_numbers=('NWC', 'WIO', 'NWC'), feature_group_count=x.shape[-1])


def short_conv_mixer(p, conv_w):
    u, gate_b, gate_c = jnp.split(p, 3, axis=-1)
    return gate_b * depthwise_conv(gate_c * u, conv_w)


def hyena_filters(n, f_w1, f_b1, f_freq, f_w2, f_b2, f_w3, f_decay):
    t = jnp.arange(n, dtype=jnp.float32)
    t_norm = t / max(n - 1, 1)
    bands = jnp.arange(1, FILTER_BANDS + 1, dtype=jnp.float32)
    ang = (2 * math.pi / n) * t[:, None] * bands[None, :]
    feats = jnp.concatenate([t_norm[:, None], jnp.cos(ang), jnp.sin(ang)], axis=-1)
    h = jnp.sin(f_freq * (feats @ f_w1 + f_b1))
    h = jnp.sin(f_freq * (h @ f_w2 + f_b2))
    h = h @ f_w3
    window = jnp.exp(-t_norm[:, None] * jnp.abs(f_decay)[None, :]) + FILTER_MOD_SHIFT
    return (h * window).astype(jnp.float32)


def bidir_long_conv(u, filt, skip):
    b, n, ch = u.shape
    k2 = jnp.concatenate([filt[:, :ch], filt[::-1, ch:]], axis=0)
    uf = jnp.fft.rfft(u.astype(jnp.float32), n=2 * n, axis=1)
    kf = jnp.fft.rfft(k2, n=2 * n, axis=0)
    y = jnp.fft.irfft(uf * kf[None], n=2 * n, axis=1)[:, :n]
    return (y + u.astype(jnp.float32) * skip.astype(jnp.float32)).astype(u.dtype)


def hyena_mixer(h, w_in, conv_w, conv_b, filt, skip):
    p = depthwise_conv(h @ w_in, conv_w) + conv_b
    x0, x1, v = jnp.split(p, 3, axis=-1)
    return x0 * bidir_long_conv(x1 * v, filt, skip)


def ffn(h, w1, w2):
    return jnp.square(jax.nn.relu(h @ w1)) @ w2


def setup_inputs(seed: int = 0) -> dict:
    key = jax.random.key(seed)
    ks = iter(jax.random.split(key, 32))

    def nrm(shape, std):
        return jax.random.normal(next(ks), shape, jnp.float32) * std

    def gain(shape):
        return 1.0 + nrm(shape, 0.02)

    decay_base = jnp.linspace(math.log(1e-2) / 1.5, math.log(1e-2) / 0.3, HYENA_W, dtype=jnp.float32)
    decay_base = jnp.tile(decay_base, 2)
    return {
        'x': nrm((BATCH, SEQ, D_MODEL), 1.0),
        'c': nrm((BATCH, D_MODEL), 1.0),
        'ctx': nrm((BATCH, CTX_LEN, D_MODEL), 1.0),
        'c_ctx': nrm((D_MODEL,), 1.0),
        'mod_w': nrm((DEPTH, D_MODEL, 6 * D_MODEL), 0.5 * D_MODEL ** -0.5),
        'mod_b': nrm((DEPTH, 6 * D_MODEL), 0.02),
        'norm_mix_g': gain((DEPTH, D_MODEL)),
        'norm_ffn_g': gain((DEPTH, D_MODEL)),
        'ffn_w1': nrm((DEPTH, D_MODEL, D_FF), D_MODEL ** -0.5),
        'ffn_w2': nrm((DEPTH, D_FF, D_MODEL), D_FF ** -0.5),
        'ev_w_in': nrm((N_EVEN, D_MODEL, EVEN_IN_W), D_MODEL ** -0.5),
        'ev_q_norm': gain((N_EVEN, HEAD_DIM)),
        'ev_k_norm': gain((N_EVEN, HEAD_DIM)),
        'ev_conv_w': nrm((N_EVEN, SHORT_K, CONV_W), SHORT_K ** -0.5),
        'ev_w_out': nrm((N_EVEN, ATTN_W + CONV_W, D_MODEL), (ATTN_W + CONV_W) ** -0.5),
        'od_w_in': nrm((N_ODD, D_MODEL, 3 * HYENA_W), D_MODEL ** -0.5),
        'od_conv_w': nrm((N_ODD, SHORT_K, 3 * HYENA_W), SHORT_K ** -0.5),
        'od_conv_b': nrm((N_ODD, 3 * HYENA_W), 0.02),
        'od_f_w1': nrm((N_ODD, FILTER_EMB, FILTER_HIDDEN), FILTER_EMB ** -0.5),
        'od_f_b1': nrm((N_ODD, FILTER_HIDDEN), 0.02),
        'od_f_freq': 1.0 + nrm((N_ODD, FILTER_HIDDEN), 0.1),
        'od_f_w2': nrm((N_ODD, FILTER_HIDDEN, FILTER_HIDDEN), FILTER_HIDDEN ** -0.5),
        'od_f_b2': nrm((N_ODD, FILTER_HIDDEN), 0.02),
        'od_f_w3': nrm((N_ODD, FILTER_HIDDEN, 2 * HYENA_W), 0.03 * FILTER_HIDDEN ** -0.5),
        'od_f_decay': decay_base[None, :] + nrm((N_ODD, 2 * HYENA_W), 0.1),
        'od_skip': nrm((N_ODD, HYENA_W), 1.0),
        'od_w_out': nrm((N_ODD, HYENA_W, D_MODEL), HYENA_W ** -0.5),
        'final_g': gain((D_MODEL,)),
    }


def reference(x, c, ctx, c_ctx, mod_w, mod_b, norm_mix_g, norm_ffn_g, ffn_w1, ffn_w2,
              ev_w_in, ev_q_norm, ev_k_norm, ev_conv_w, ev_w_out,
              od_w_in, od_conv_w, od_conv_b, od_f_w1, od_f_b1, od_f_freq, od_f_w2,
              od_f_b2, od_f_w3, od_f_decay, od_skip, od_w_out, final_g):
    n = x.shape[1]
    n_ctx = ctx.shape[1]
    rope = axial_rope_tables(n)
    xc = ctx
    for l in range(DEPTH):
        ctx_full = l < LAST_CTX_READ
        sh1, sc1, g1, sh2, sc2, g2 = adaln(c[:, None, :], mod_w[l], mod_b[l])
        if l <= LAST_CTX_READ:
            csh1, csc1, cg1, csh2, csc2, cg2 = adaln(c_ctx, mod_w[l], mod_b[l])
        if l % 2 == 0:
            e = l // 2
            w_in = ev_w_in[e]
            pl = modulate(x, norm_mix_g[l], sh1, sc1) @ w_in
            hc = modulate(xc, norm_mix_g[l], csh1, csc1)
            pc = hc @ (w_in if ctx_full else w_in[:, KV_LO:KV_HI])
            k_c, v_c = attn_kv(pc[..., KV_LO:KV_HI] if ctx_full else pc, ev_k_norm[e], None)
            k_l, v_l = attn_kv(pl[..., KV_LO:KV_HI], ev_k_norm[e], rope)
            q_l = attn_q(pl[..., :ATTN_W], ev_q_norm[e], rope)
            mix_l = jnp.concatenate([latent_attention(q_l, k_l, v_l, k_c, v_c),
                                     short_conv_mixer(pl[..., KV_HI:], ev_conv_w[e])], axis=-1)
            x = x + g1 * (mix_l @ ev_w_out[e])
            if ctx_full:
                q_c = attn_q(pc[..., :ATTN_W], ev_q_norm[e], None)
                mix_c = jnp.concatenate([context_attention(q_c, k_c, v_c),
                                         short_conv_mixer(pc[..., KV_HI:], ev_conv_w[e])], axis=-1)
                xc = xc + cg1 * (mix_c @ ev_w_out[e])
        else:
            o = l // 2
            fp = (od_f_w1[o], od_f_b1[o], od_f_freq[o], od_f_w2[o], od_f_b2[o], od_f_w3[o], od_f_decay[o])
            y = hyena_mixer(modulate(x, norm_mix_g[l], sh1, sc1), od_w_in[o], od_conv_w[o],
                            od_conv_b[o], hyena_filters(n, *fp), od_skip[o])
            x = x + g1 * (y @ od_w_out[o])
            if ctx_full:
                yc = hyena_mixer(modulate(xc, norm_mix_g[l], csh1, csc1), od_w_in[o], od_conv_w[o],
                                 od_conv_b[o], hyena_filters(n_ctx, *fp), od_skip[o])
                xc = xc + cg1 * (yc @ od_w_out[o])
        x = x + g2 * ffn(modulate(x, norm_ffn_g[l], sh2, sc2), ffn_w1[l], ffn_w2[l])
        if ctx_full:
            xc = xc + cg2 * ffn(modulate(xc, norm_ffn_g[l], csh2, csc2), ffn_w1[l], ffn_w2[l])
    return rms_norm(x, final_g)
```

```python
import functools
import math

import numpy as np
import jax
import jax.numpy as jnp
from jax import lax
from jax.experimental import pallas as pl
from jax.experimental.pallas import tpu as pltpu

F32 = jnp.float32
BF16 = jnp.bfloat16

HEAD_DIM = 128
N_Q_HEADS = 8
N_KV_HEADS = 2
Q_PER_KV = N_Q_HEADS // N_KV_HEADS
GRID_W = 64
ROPE_THETA = 10000.0
FILTER_BANDS = 16
FILTER_MOD_SHIFT = 0.05
EPS = 1e-6

LANES = 128
SUBLANES = 8
VMEM_BUDGET = 56 * 1024 * 1024
DFT_N1 = 128


def _cparams(sem, vmem_bytes):
    return pltpu.CompilerParams(dimension_semantics=sem,
                                vmem_limit_bytes=int(min(max(vmem_bytes, 16 << 20), VMEM_BUDGET)))


def _nbytes(shape, dtype):
    return int(np.prod(shape)) * jnp.dtype(dtype).itemsize


def _dot(a, b):
    return jnp.dot(a, b, preferred_element_type=F32)


def _dot_hi(a, b):
    return jnp.dot(a, b, preferred_element_type=F32, precision=lax.Precision.HIGHEST)


def _rms_rows(x):
    return x * lax.rsqrt(jnp.mean(x * x, axis=-1, keepdims=True) + EPS)


def _mod_kernel(c_ref, w_ref, b_ref, o_ref):
    c = c_ref[...]
    s = c * (1.0 / (1.0 + jnp.exp(-c)))
    o_ref[0] = _dot(s.astype(BF16), w_ref[0].astype(BF16)) + b_ref[0]


def _mod_vectors(cond, mod_w, mod_b):
    depth, d, n6 = mod_w.shape
    tn = 1536
    rows = cond.shape[0]
    vm = 2 * _nbytes((d, tn), F32) + _nbytes((d, tn), BF16) + (4 << 20)
    return pl.pallas_call(
        _mod_kernel,
        grid=(depth, n6 // tn),
        in_specs=[pl.BlockSpec((rows, d), lambda l, j: (0, 0)),
                  pl.BlockSpec((1, d, tn), lambda l, j: (l, 0, j)),
                  pl.BlockSpec((1, 1, tn), lambda l, j: (l, 0, j))],
        out_specs=pl.BlockSpec((1, rows, tn), lambda l, j: (l, 0, j)),
        out_shape=jax.ShapeDtypeStruct((depth, rows, n6), F32),
        compiler_params=_cparams(("parallel", "parallel"), vm),
        name="adaln_vectors",
    )(cond, mod_w, mod_b.reshape(depth, 1, n6))


def _modmm_kernel(x_ref, g_ref, sh_ref, sc_ref, w_ref, o_ref, h_ref, *, row):
    @pl.when(pl.program_id(1) == 0)
    def _():
        h = _rms_rows(x_ref[...]) * g_ref[...]
        h = h * (1.0 + sc_ref[row:row + 1, :]) + sh_ref[row:row + 1, :]
        h_ref[...] = h.astype(BF16)

    o_ref[...] = _dot(h_ref[...], w_ref[...]).astype(o_ref.dtype)


def _modmm(x, gain, modv, row, col0, w, tm, tn):
    m, d = x.shape
    n = w.shape[1]
    tm = min(tm, m)
    vm = (2 * _nbytes((tm, d), F32) + _nbytes((tm, d), BF16) + 2 * _nbytes((d, tn), BF16)
          + 2 * _nbytes((tm, tn), F32) + 2 * _nbytes((tm, d), F32) + (4 << 20))
    return pl.pallas_call(
        functools.partial(_modmm_kernel, row=row),
        grid=(m // tm, n // tn),
        in_specs=[pl.BlockSpec((tm, d), lambda i, j: (i, 0)),
                  pl.BlockSpec((1, d), lambda i, j: (0, 0)),
                  pl.BlockSpec((SUBLANES, d), lambda i, j: (0, col0)),
                  pl.BlockSpec((SUBLANES, d), lambda i, j: (0, col0 + 1)),
                  pl.BlockSpec((d, tn), lambda i, j: (0, j))],
        out_specs=pl.BlockSpec((tm, tn), lambda i, j: (i, j)),
        out_shape=jax.ShapeDtypeStruct((m, n), F32),
        scratch_shapes=[pltpu.VMEM((tm, d), BF16)],
        compiler_params=_cparams(("parallel", "arbitrary"), vm),
        name="modulated_projection",
    )(x, gain.reshape(1, d), modv, modv, w)


def _outproj_kernel(*refs, n_in, row):
    a_refs = refs[:n_in]
    w_refs = refs[n_in:2 * n_in]
    x_ref, g_ref, o_ref = refs[2 * n_in:]
    acc = _dot(a_refs[0][...].astype(BF16), w_refs[0][...])
    for a_ref, w_ref in zip(a_refs[1:], w_refs[1:]):
        acc = acc + _dot(a_ref[...].astype(BF16), w_ref[...])
    o_ref[...] = x_ref[...] + g_ref[row:row + 1, :] * acc


def _outproj(acts, w, x, modv, row, gate_col, tm):
    m, d = x.shape
    tm = min(tm, m)
    n_in = len(acts)
    in_specs, args = [], []
    for a in acts:
        in_specs.append(pl.BlockSpec((tm, a.shape[1]), lambda i: (i, 0)))
        args.append(a)
    row0 = 0
    for a in acts:
        k = a.shape[1]
        in_specs.append(pl.BlockSpec((k, d), functools.partial(lambda i, b: (b, 0), b=row0 // k)))
        args.append(w)
        row0 += k
    in_specs += [pl.BlockSpec((tm, d), lambda i: (i, 0)),
                 pl.BlockSpec((SUBLANES, d), lambda i: (0, gate_col))]
    args += [x, modv]
    vm = (2 * sum(_nbytes((tm, a.shape[1]), a.dtype) for a in acts) + 2 * _nbytes(w.shape, BF16)
          + 6 * _nbytes((tm, d), F32) + (4 << 20))
    return pl.pallas_call(
        functools.partial(_outproj_kernel, n_in=n_in, row=row),
        grid=(m // tm,),
        in_specs=in_specs,
        out_specs=pl.BlockSpec((tm, d), lambda i: (i, 0)),
        out_shape=jax.ShapeDtypeStruct((m, d), F32),
        compiler_params=_cparams(("parallel",), vm),
        name="gated_out_projection",
    )(*args)


def _ffn_kernel(x_ref, g_ref, sh_ref, sc_ref, gate_ref, w1_ref, w2_ref, o_ref, h_ref, acc_ref, *, row):
    j = pl.program_id(1)

    @pl.when(j == 0)
    def _():
        h = _rms_rows(x_ref[...]) * g_ref[...]
        h = h * (1.0 + sc_ref[row:row + 1, :]) + sh_ref[row:row + 1, :]
        h_ref[...] = h.astype(BF16)
        acc_ref[...] = jnp.zeros(acc_ref.shape, F32)

    a = jnp.maximum(_dot(h_ref[...], w1_ref[...]), 0.0)
    acc_ref[...] += _dot((a * a).astype(BF16), w2_ref[...])

    @pl.when(j == pl.num_programs(1) - 1)
    def _():
        o_ref[...] = x_ref[...] + gate_ref[row:row + 1, :] * acc_ref[...]


def _ffn(x, gain, modv, row, w1, w2, tm, tf):
    m, d = x.shape
    f = w1.shape[1]
    tm = min(tm, m)
    vm = (4 * _nbytes((tm, d), F32) + _nbytes((tm, d), BF16) + _nbytes((tm, d), F32)
          + 4 * _nbytes((d, tf), BF16) + 3 * _nbytes((tm, tf), F32) + (4 << 20))
    return pl.pallas_call(
        functools.partial(_ffn_kernel, row=row),
        grid=(m // tm, f // tf),
        in_specs=[pl.BlockSpec((tm, d), lambda i, j: (i, 0)),
                  pl.BlockSpec((1, d), lambda i, j: (0, 0)),
                  pl.BlockSpec((SUBLANES, d), lambda i, j: (0, 3)),
                  pl.BlockSpec((SUBLANES, d), lambda i, j: (0, 4)),
                  pl.BlockSpec((SUBLANES, d), lambda i, j: (0, 5)),
                  pl.BlockSpec((d, tf), lambda i, j: (0, j)),
                  pl.BlockSpec((tf, d), lambda i, j: (j, 0))],
        out_specs=pl.BlockSpec((tm, d), lambda i, j: (i, 0)),
        out_shape=jax.ShapeDtypeStruct((m, d), F32),
        scratch_shapes=[pltpu.VMEM((tm, d), BF16), pltpu.VMEM((tm, d), F32)],
        compiler_params=_cparams(("parallel", "arbitrary"), vm),
        name="gated_ffn",
    )(x, gain.reshape(1, d), modv, modv, modv, w1, w2)


def _rope(x, cos, sin):
    lane = lax.broadcasted_iota(jnp.int32, x.shape, 1)
    first_half = (lane % (HEAD_DIM // 2)) < (HEAD_DIM // 4)
    partner = jnp.where(first_half,
                        pltpu.roll(x, HEAD_DIM - HEAD_DIM // 4, 1),
                        pltpu.roll(x, HEAD_DIM // 4, 1))
    return x * cos + partner * sin


def _qkv_kernel(*refs, use_rope, q_scale):
    if use_rope:
        pq_ref, pkv_ref, qg_ref, kg_ref, cos_ref, sin_ref, q_ref, k_ref, v_ref = refs
        cos, sin = cos_ref[...], sin_ref[...]
    else:
        pq_ref, pkv_ref, qg_ref, kg_ref, q_ref, k_ref, v_ref = refs
    kv_w = N_KV_HEADS * HEAD_DIM
    for h in range(N_Q_HEADS):
        sl = slice(h * HEAD_DIM, (h + 1) * HEAD_DIM)
        xn = _rms_rows(pq_ref[:, sl]) * qg_ref[...]
        if use_rope:
            xn = _rope(xn, cos, sin)
        q_ref[:, sl] = (xn * q_scale).astype(BF16)
    for h in range(N_KV_HEADS):
        sl = slice(h * HEAD_DIM, (h + 1) * HEAD_DIM)
        xn = _rms_rows(pkv_ref[:, sl]) * kg_ref[...]
        if use_rope:
            xn = _rope(xn, cos, sin)
        k_ref[:, sl] = xn.astype(BF16)
    v_ref[...] = pkv_ref[:, kv_w:2 * kv_w].astype(BF16)


def _qkv_prep(p, q_col_block, kv_col_block, q_gain, k_gain, rope, tm):
    m = p.shape[0]
    tm = min(tm, m)
    attn_w = N_Q_HEADS * HEAD_DIM
    kv_w = N_KV_HEADS * HEAD_DIM
    in_specs = [pl.BlockSpec((tm, attn_w), lambda i: (i, q_col_block)),
                pl.BlockSpec((tm, 2 * kv_w), lambda i: (i, kv_col_block)),
                pl.BlockSpec((1, HEAD_DIM), lambda i: (0, 0)),
                pl.BlockSpec((1, HEAD_DIM), lambda i: (0, 0))]
    args = [p, p, q_gain.reshape(1, HEAD_DIM), k_gain.reshape(1, HEAD_DIM)]
    if rope is not None:
        in_specs += [pl.BlockSpec((tm, HEAD_DIM), lambda i: (i, 0))] * 2
        args += list(rope)
    vm = 4 * _nbytes((tm, attn_w + 2 * kv_w), F32) + (8 << 20)
    return pl.pallas_call(
        functools.partial(_qkv_kernel, use_rope=rope is not None, q_scale=HEAD_DIM ** -0.5),
        grid=(m // tm,),
        in_specs=in_specs,
        out_specs=[pl.BlockSpec((tm, attn_w), lambda i: (i, 0)),
                   pl.BlockSpec((tm, kv_w), lambda i: (i, 0)),
                   pl.BlockSpec((tm, kv_w), lambda i: (i, 0))],
        out_shape=[jax.ShapeDtypeStruct((m, attn_w), BF16),
                   jax.ShapeDtypeStruct((m, kv_w), BF16),
                   jax.ShapeDtypeStruct((m, kv_w), BF16)],
        compiler_params=_cparams(("parallel",), vm),
        name="qkv_norm_rope",
    )(*args)


def _attn_kernel(*refs, has_extra, tq):
    if has_extra:
        q_ref, k_ref, v_ref, ke_ref, ve_ref, o_ref, qs_ref, m_ref, l_ref, acc_ref = refs
    else:
        q_ref, k_ref, v_ref, o_ref, qs_ref, m_ref, l_ref, acc_ref = refs
    kv = pl.program_id(2)

    def scores(k):
        return lax.dot_general(qs_ref[...], k, (((1,), (1,)), ((), ())), preferred_element_type=F32)

    @pl.when(kv == 0)
    def _():
        for h in range(Q_PER_KV):
            qs_ref[h * tq:(h + 1) * tq, :] = q_ref[:, h * HEAD_DIM:(h + 1) * HEAD_DIM]
        if has_extra:
            s = scores(ke_ref[...])
            m0 = jnp.max(s, axis=-1, keepdims=True)
            p = jnp.exp(s - m0)
            m_ref[...] = m0
            l_ref[...] = jnp.sum(p, axis=-1, keepdims=True)
            acc_ref[...] = _dot(p.astype(BF16), ve_ref[...])
        else:
            m_ref[...] = jnp.full(m_ref.shape, -jnp.inf, F32)
            l_ref[...] = jnp.zeros(l_ref.shape, F32)
            acc_ref[...] = jnp.zeros(acc_ref.shape, F32)

    s = scores(k_ref[...])
    m_prev = m_ref[...]
    m_new = jnp.maximum(m_prev, jnp.max(s, axis=-1, keepdims=True))
    alpha = jnp.exp(m_prev - m_new)
    p = jnp.exp(s - m_new)
    l_ref[...] = alpha * l_ref[...] + jnp.sum(p, axis=-1, keepdims=True)
    acc_ref[...] = alpha * acc_ref[...] + _dot(p.astype(BF16), v_ref[...])
    m_ref[...] = m_new

    @pl.when(kv == pl.num_programs(2) - 1)
    def _():
        o = acc_ref[...] / l_ref[...]
        for h in range(Q_PER_KV):
            o_ref[:, h * HEAD_DIM:(h + 1) * HEAD_DIM] = o[h * tq:(h + 1) * tq, :].astype(o_ref.dtype)


def _attention(q, k, v, k_extra, v_extra, tq, tk):
    m = q.shape[0]
    s_len = k.shape[0]
    tq = min(tq, m)
    tk = min(tk, s_len)
    gw = Q_PER_KV * HEAD_DIM
    rows = Q_PER_KV * tq
    has_extra = k_extra is not None
    in_specs = [pl.BlockSpec((tq, gw), lambda g, i, j: (i, g)),
                pl.BlockSpec((tk, HEAD_DIM), lambda g, i, j: (j, g)),
                pl.BlockSpec((tk, HEAD_DIM), lambda g, i, j: (j, g))]
    args = [q, k, v]
    if has_extra:
        e = k_extra.shape[0]
        in_specs += [pl.BlockSpec((e, HEAD_DIM), lambda g, i, j: (0, g))] * 2
        args += [k_extra, v_extra]
    vm = 6 * _nbytes((rows, tk), F32) + 8 * _nbytes((rows, LANES), F32) + (8 << 20)
    return pl.pallas_call(
        functools.partial(_attn_kernel, has_extra=has_extra, tq=tq),
        grid=(N_KV_HEADS, m // tq, s_len // tk),
        in_specs=in_specs,
        out_specs=pl.BlockSpec((tq, gw), lambda g, i, j: (i, g)),
        out_shape=jax.ShapeDtypeStruct((m, N_Q_HEADS * HEAD_DIM), BF16),
        scratch_shapes=[pltpu.VMEM((rows, HEAD_DIM), BF16),
                        pltpu.VMEM((rows, 1), F32),
                        pltpu.VMEM((rows, 1), F32),
                        pltpu.VMEM((rows, HEAD_DIM), F32)],
        compiler_params=_cparams(("parallel", "parallel", "arbitrary"), vm),
        name="gqa_flash_attention",
    )(*args)


def _conv3(z, w_ref):
    n = z.shape[0]
    row = lax.broadcasted_iota(jnp.int32, z.shape, 0)
    prev = jnp.where(row == 0, 0.0, pltpu.roll(z, 1, 0))
    nxt = jnp.where(row == n - 1, 0.0, pltpu.roll(z, n - 1, 0))
    return prev * w_ref[0:1, :] + z * w_ref[1:2, :] + nxt * w_ref[2:3, :]


def _convmix_kernel(u_ref, b_ref, c_ref, w_ref, o_ref):
    o_ref[...] = (b_ref[...] * _conv3(c_ref[...] * u_ref[...], w_ref)).astype(o_ref.dtype)


def _short_conv_mixer(p, col0, conv_w):
    m = p.shape[0]
    c = conv_w.shape[1]
    ct = LANES
    nb = c // ct
    b0 = col0 // ct
    vm = 8 * _nbytes((m, ct), F32) + 8 * _nbytes((m, ct), F32) + (4 << 20)
    return pl.pallas_call(
        _convmix_kernel,
        grid=(nb,),
        in_specs=[pl.BlockSpec((m, ct), lambda j: (0, b0 + j)),
                  pl.BlockSpec((m, ct), lambda j: (0, b0 + nb + j)),
                  pl.BlockSpec((m, ct), lambda j: (0, b0 + 2 * nb + j)),
                  pl.BlockSpec((3, ct), lambda j: (0, j))],
        out_specs=pl.BlockSpec((m, ct), lambda j: (0, j)),
        out_shape=jax.ShapeDtypeStruct((m, c), BF16),
        compiler_params=_cparams(("parallel",), vm),
        name="short_conv_mixer",
    )(p, p, p, conv_w)


def _hyena_prep_kernel(p0_ref, p1_ref, p2_ref, w0_ref, w1_ref, w2_ref, b0_ref, b1_ref, b2_ref,
                       skip_ref, u_ref, x0_ref, usk_ref):
    x0 = _conv3(p0_ref[...], w0_ref) + b0_ref[...]
    x1 = _conv3(p1_ref[...], w1_ref) + b1_ref[...]
    v = _conv3(p2_ref[...], w2_ref) + b2_ref[...]
    u = x1 * v
    u_ref[...] = u
    x0_ref[...] = x0
    usk_ref[...] = x0 * (u * skip_ref[...])


def _hyena_prep(p, conv_w, conv_b, skip):
    m = p.shape[0]
    c = skip.shape[0]
    ct = LANES
    nb = c // ct
    vm = 12 * _nbytes((m, ct), F32) + 10 * _nbytes((m, ct), F32) + (4 << 20)
    blk = lambda off: pl.BlockSpec((m, ct), functools.partial(lambda j, o: (0, o + j), o=off))
    wblk = lambda off: pl.BlockSpec((3, ct), functools.partial(lambda j, o: (0, o + j), o=off))
    bblk = lambda off: pl.BlockSpec((1, ct), functools.partial(lambda j, o: (0, o + j), o=off))
    out = jax.ShapeDtypeStruct((m, c), F32)
    return pl.pallas_call(
        _hyena_prep_kernel,
        grid=(nb,),
        in_specs=[blk(0), blk(nb), blk(2 * nb), wblk(0), wblk(nb), wblk(2 * nb),
                  bblk(0), bblk(nb), bblk(2 * nb), pl.BlockSpec((1, ct), lambda j: (0, j))],
        out_specs=[pl.BlockSpec((m, ct), lambda j: (0, j))] * 3,
        out_shape=[out, out, out],
        compiler_params=_cparams(("parallel",), vm),
        name="hyena_short_conv_gate",
    )(p, p, p, conv_w, conv_w, conv_w, conv_b.reshape(1, -1), conv_b.reshape(1, -1),
      conv_b.reshape(1, -1), skip.reshape(1, c))


def _filter_feats(n, order):
    t = np.asarray(order, np.float64)
    t_norm = t / max(n - 1, 1)
    bands = np.arange(1, FILTER_BANDS + 1, dtype=np.float64)
    ang = (2 * math.pi / n) * t[:, None] * bands[None, :]
    feats = np.concatenate([t_norm[:, None], np.cos(ang), np.sin(ang)], axis=-1)
    out = np.zeros((len(t), LANES), np.float32)
    out[:, :feats.shape[1]] = feats
    return out


def _filter_kernel(f_ref, w1_ref, b1_ref, fr_ref, w2_ref, b2_ref, w3_ref, dec_ref, o_ref, h_ref):
    @pl.when(pl.program_id(2) == 0)
    def _():
        fr = fr_ref[...]
        h = jnp.sin(fr * (_dot_hi(f_ref[0], w1_ref[...]) + b1_ref[...]))
        h_ref[...] = jnp.sin(fr * (_dot_hi(h, w2_ref[...]) + b2_ref[...]))

    t_norm = f_ref[0][:, 0:1]
    window = jnp.exp(-t_norm * jnp.abs(dec_ref[...])) + FILTER_MOD_SHIFT
    o_ref[...] = (_dot_hi(h_ref[...], w3_ref[...]) * window).astype(o_ref.dtype)


def _hyena_filter(n, ch, f_w1, f_b1, f_freq, f_w2, f_b2, f_w3, f_decay, tr, tc):
    hid = f_w1.shape[1]
    feats = np.stack([_filter_feats(n, np.arange(n)), _filter_feats(n, np.arange(n)[::-1])])
    w1p = jnp.zeros((LANES, hid), F32).at[:f_w1.shape[0]].set(f_w1)
    tr = min(tr, n)
    nrb = n // tr
    ncb = ch // tc
    vm = 4 * _nbytes((tr, tc), F32) + 2 * _nbytes((tr, LANES), F32) + (8 << 20)
    return pl.pallas_call(
        _filter_kernel,
        grid=(2, nrb, ncb),
        in_specs=[pl.BlockSpec((1, tr, LANES), lambda s, i, j: (s, i, 0)),
                  pl.BlockSpec((LANES, hid), lambda s, i, j: (0, 0)),
                  pl.BlockSpec((1, hid), lambda s, i, j: (0, 0)),
                  pl.BlockSpec((1, hid), lambda s, i, j: (0, 0)),
                  pl.BlockSpec((hid, hid), lambda s, i, j: (0, 0)),
                  pl.BlockSpec((1, hid), lambda s, i, j: (0, 0)),
                  pl.BlockSpec((hid, tc), lambda s, i, j: (0, s * ncb + j)),
                  pl.BlockSpec((1, tc), lambda s, i, j: (0, s * ncb + j))],
        out_specs=pl.BlockSpec((tr, tc), lambda s, i, j: (s * nrb + i, j)),
        out_shape=jax.ShapeDtypeStruct((2 * n, ch), F32),
        scratch_shapes=[pltpu.VMEM((tr, hid), F32)],
        compiler_params=_cparams(("parallel", "parallel", "arbitrary"), vm),
        name="hyena_filter",
    )(jnp.asarray(feats), w1p, f_b1.reshape(1, hid), f_freq.reshape(1, hid), f_w2,
      f_b2.reshape(1, hid), f_w3, f_decay.reshape(1, -1))


def _dft_geometry(n):
    big_n = 2 * n
    n2 = big_n // DFT_N1
    nh = -(-(n2 // 2 + 1) // SUBLANES) * SUBLANES
    return big_n, n2, nh


@functools.lru_cache(maxsize=None)
def _stage1_matrix(n, rows_n2):
    big_n, _, nh = _dft_geometry(n)
    groups = DFT_N1 // SUBLANES
    out = np.zeros((groups, 2, nh, SUBLANES, rows_n2, SUBLANES), np.float32)
    k2 = np.arange(nh, dtype=np.float64)[:, None]
    n2v = np.arange(rows_n2, dtype=np.float64)[None, :]
    for g in range(groups):
        for j in range(SUBLANES):
            t = SUBLANES * g + j + DFT_N1 * n2v
            ang = 2 * math.pi * np.mod(k2 * t, big_n) / big_n
            out[g, 0, :, j, :, j] = np.cos(ang)
            out[g, 1, :, j, :, j] = -np.sin(ang)
    return out.reshape(groups, 2 * nh * SUBLANES, rows_n2 * SUBLANES).astype(BF16)


@functools.lru_cache(maxsize=None)
def _stage2_matrices():
    idx = np.arange(DFT_N1, dtype=np.float64)
    ang = 2 * math.pi * np.mod(np.outer(idx, idx), DFT_N1) / DFT_N1
    c, s = np.cos(ang), np.sin(ang)
    fwd = np.block([[c, s], [-s, c]])
    inv = np.block([[c, -s], [s, c]])
    return fwd.astype(BF16), inv.astype(BF16)


@functools.lru_cache(maxsize=None)
def _stage4_matrix(n):
    big_n, n2c, nh = _dft_geometry(n)
    groups = DFT_N1 // SUBLANES
    rows_n2 = n2c // 2
    out = np.zeros((groups, rows_n2, SUBLANES, 2, nh, SUBLANES), np.float32)
    k2 = np.arange(nh, dtype=np.float64)[None, :]
    wgt = np.where(k2 <= n2c // 2, 2.0, 0.0)
    wgt[0, 0] = 1.0
    wgt[0, n2c // 2] = 1.0
    n2v = np.arange(rows_n2, dtype=np.float64)[:, None]
    for g in range(groups):
        for j in range(SUBLANES):
            t = SUBLANES * g + j + DFT_N1 * n2v
            ang = 2 * math.pi * np.mod(k2 * t, big_n) / big_n
            out[g, :, j, 0, :, j] = wgt * np.cos(ang) / big_n
            out[g, :, j, 1, :, j] = -wgt * np.sin(ang) / big_n
    return out.reshape(groups, rows_n2 * SUBLANES, 2 * nh * SUBLANES).astype(BF16)


def _dft_stage1_kernel(x_ref, l_ref, o_ref, *, gsteps):
    nb, _, ct = x_ref.shape
    rows = o_ref.shape[0]
    for gi in range(gsteps):
        sl = slice(SUBLANES * gi, SUBLANES * (gi + 1))
        xv = x_ref[:, sl, :].reshape(nb * SUBLANES, ct).astype(BF16)
        out = _dot(l_ref[gi], xv)
        o_ref[:, sl, :] = out.reshape(rows, SUBLANES, ct)


def _dft_stage1(x3, n, gsteps, ct):
    rows_n2, _, c = x3.shape
    _, _, nh = _dft_geometry(n)
    mat = _stage1_matrix(n, rows_n2)
    groups = mat.shape[0]
    ct = min(ct, c)
    gw = SUBLANES * gsteps
    vm = (2 * _nbytes((rows_n2, gw, ct), F32) + 2 * _nbytes((gsteps,) + mat.shape[1:], BF16)
          + 2 * _nbytes((2 * nh, gw, ct), F32) + 3 * _nbytes((2 * nh * SUBLANES, ct), F32) + (4 << 20))
    return pl.pallas_call(
        functools.partial(_dft_stage1_kernel, gsteps=gsteps),
        grid=(groups // gsteps, c // ct),
        in_specs=[pl.BlockSpec((rows_n2, gw, ct), lambda g, j: (0, g, j)),
                  pl.BlockSpec((gsteps,) + mat.shape[1:], lambda g, j: (g, 0, 0))],
        out_specs=pl.BlockSpec((2 * nh, gw, ct), lambda g, j: (0, g, j)),
        out_shape=jax.ShapeDtypeStruct((2 * nh, DFT_N1, c), F32),
        compiler_params=_cparams(("parallel", "parallel"), vm),
        name="dft_stage_n2",
    )(x3, mat)


def _stack_parts(a_ref, kb):
    cols = [jnp.concatenate([a_ref[0, k], a_ref[1, k]], axis=0) for k in range(kb)]
    return jnp.concatenate(cols, axis=1).astype(BF16)


def _filter_spectrum_kernel(a_ref, l2_ref, o_ref, *, kb):
    ct = a_ref.shape[-1]
    x = _dot(l2_ref[...], _stack_parts(a_ref, kb))
    for k in range(kb):
        o_ref[0, k] = x[:DFT_N1, k * ct:(k + 1) * ct]
        o_ref[1, k] = x[DFT_N1:, k * ct:(k + 1) * ct]


def _spectral_product_kernel(a_ref, kf_ref, l2_ref, l3_ref, o_ref, *, kb):
    ct = a_ref.shape[-1]
    x = _dot(l2_ref[...], _stack_parts(a_ref, kb))
    ys = []
    for k in range(kb):
        xr = x[:DFT_N1, k * ct:(k + 1) * ct]
        xi = x[DFT_N1:, k * ct:(k + 1) * ct]
        kr, ki = kf_ref[0, k], kf_ref[1, k]
        ys.append(jnp.concatenate([xr * kr - xi * ki, xr * ki + xi * kr], axis=0))
    v = _dot(l3_ref[...], jnp.concatenate(ys, axis=1).astype(BF16))
    for k in range(kb):
        o_ref[0, k] = v[:DFT_N1, k * ct:(k + 1) * ct]
        o_ref[1, k] = v[DFT_N1:, k * ct:(k + 1) * ct]


def _dft_stage2(a4, kf4, kb, ct):
    _, nh, _, c = a4.shape
    fwd, inv = _stage2_matrices()
    ct = min(ct, c)
    blk = pl.BlockSpec((2, kb, DFT_N1, ct), lambda g, j: (0, g, 0, j))
    mblk = pl.BlockSpec((2 * DFT_N1, 2 * DFT_N1), lambda g, j: (0, 0))
    nblk = 2 if kf4 is None else 3
    vm = 2 * nblk * _nbytes((2, kb, DFT_N1, ct), F32) + 8 * _nbytes((2 * DFT_N1, kb * ct), F32) + (4 << 20)
    if kf4 is None:
        kern, in_specs, args = _filter_spectrum_kernel, [blk, mblk], (a4, fwd)
        name = "dft_stage_n1_filter"
    else:
        kern, in_specs, args = _spectral_product_kernel, [blk, blk, mblk, mblk], (a4, kf4, fwd, inv)
        name = "dft_stage_n1_product"
    return pl.pallas_call(
        functools.partial(kern, kb=kb),
        grid=(nh // kb, c // ct),
        in_specs=in_specs,
        out_specs=blk,
        out_shape=jax.ShapeDtypeStruct(a4.shape, F32),
        compiler_params=_cparams(("parallel", "parallel"), vm),
        name=name,
    )(*args)


def _dft_stage4_kernel(v_ref, l_ref, x0_ref, usk_ref, o_ref, *, gsteps):
    rows, _, ct = v_ref.shape
    nb = o_ref.shape[0]
    for gi in range(gsteps):
        sl = slice(SUBLANES * gi, SUBLANES * (gi + 1))
        vv = v_ref[:, sl, :].reshape(rows * SUBLANES, ct).astype(BF16)
        y = _dot(l_ref[gi], vv).reshape(nb, SUBLANES, ct)
        o_ref[:, sl, :] = x0_ref[:, sl, :] * y + usk_ref[:, sl, :]


def _dft_stage4(v3, x03, usk3, n, gsteps, ct):
    rows, _, c = v3.shape
    mat = _stage4_matrix(n)
    groups = mat.shape[0]
    nb = x03.shape[0]
    ct = min(ct, c)
    gw = SUBLANES * gsteps
    vm = (2 * _nbytes((rows, gw, ct), F32) + 2 * _nbytes((gsteps,) + mat.shape[1:], BF16)
          + 6 * _nbytes((nb, gw, ct), F32) + 3 * _nbytes((rows * SUBLANES, ct), F32) + (4 << 20))
    oblk = pl.BlockSpec((nb, gw, ct), lambda g, j: (0, g, j))
    return pl.pallas_call(
        functools.partial(_dft_stage4_kernel, gsteps=gsteps),
        grid=(groups // gsteps, c // ct),
        in_specs=[pl.BlockSpec((rows, gw, ct), lambda g, j: (0, g, j)),
                  pl.BlockSpec((gsteps,) + mat.shape[1:], lambda g, j: (g, 0, 0)),
                  oblk, oblk],
        out_specs=oblk,
        out_shape=jax.ShapeDtypeStruct((nb, DFT_N1, c), F32),
        compiler_params=_cparams(("parallel", "parallel"), vm),
        name="dft_stage_k2_inverse",
    )(v3, mat, x03, usk3)


def _long_conv_gate(u, x0, usk, filt2, n):
    c = u.shape[1]
    _, _, nh = _dft_geometry(n)
    as3 = lambda a: a.reshape(a.shape[0] // DFT_N1, DFT_N1, c)
    kf = _dft_stage1(as3(filt2), n, gsteps=2, ct=512)
    kf = _dft_stage2(kf.reshape(2, nh, DFT_N1, c), None, kb=SUBLANES, ct=256)
    a = _dft_stage1(as3(u), n, gsteps=2, ct=512)
    v = _dft_stage2(a.reshape(2, nh, DFT_N1, c), kf, kb=SUBLANES, ct=256)
    y = _dft_stage4(v.reshape(2 * nh, DFT_N1, c), as3(x0), as3(usk), n, gsteps=2, ct=512)
    return y.reshape(n, c)


@functools.lru_cache(maxsize=None)
def _dense_dft_matrices(n):
    big_n = 2 * n
    nf = n + 1
    nfp = -(-nf // SUBLANES) * SUBLANES
    k = np.arange(nf, dtype=np.float64)[:, None]
    t = np.arange(big_n, dtype=np.float64)[None, :]
    ang = 2 * math.pi * np.mod(k * t, big_n) / big_n
    fwd = np.zeros((2, nfp, big_n), np.float32)
    fwd[0, :nf] = np.cos(ang)
    fwd[1, :nf] = -np.sin(ang)
    wgt = np.full((nf, 1), 2.0)
    wgt[0] = 1.0
    wgt[n] = 1.0
    inv = np.zeros((2, nfp, n), np.float32)
    inv[0, :nf] = (wgt * np.cos(ang) / big_n)[:, :n]
    inv[1, :nf] = (-wgt * np.sin(ang) / big_n)[:, :n]
    fwd = fwd.reshape(2 * nfp, big_n)
    inv = inv.reshape(2 * nfp, n).T
    return (np.ascontiguousarray(fwd[:, :n]).astype(BF16), fwd.astype(BF16),
            np.ascontiguousarray(inv).astype(BF16))


def _short_conv_gate_kernel(u_ref, f_ref, x0_ref, usk_ref, lu_ref, lf_ref, li_ref, o_ref):
    half = lu_ref.shape[0] // 2
    xs = _dot(lu_ref[...], u_ref[...].astype(BF16))
    ks = _dot(lf_ref[...], f_ref[...].astype(BF16))
    xr, xi, kr, ki = xs[:half], xs[half:], ks[:half], ks[half:]
    y = jnp.concatenate([xr * kr - xi * ki, xr * ki + xi * kr], axis=0).astype(BF16)
    o_ref[...] = x0_ref[...] * _dot(li_ref[...], y) + usk_ref[...]


def _long_conv_gate_short(u, x0, usk, filt2, n):
    c = u.shape[1]
    lu, lf, li = _dense_dft_matrices(n)
    ct = 256
    blk = pl.BlockSpec((n, ct), lambda j: (0, j))
    full = lambda a: pl.BlockSpec(a.shape, lambda j: (0, 0))
    vm = 32 << 20
    return pl.pallas_call(
        _short_conv_gate_kernel,
        grid=(c // ct,),
        in_specs=[blk, pl.BlockSpec((2 * n, ct), lambda j: (0, j)), blk, blk, full(lu), full(lf), full(li)],
        out_specs=blk,
        out_shape=jax.ShapeDtypeStruct((n, c), F32),
        compiler_params=_cparams(("parallel",), vm),
        name="context_long_conv",
    )(u, filt2, x0, usk, lu, lf, li)


def _final_norm_kernel(x_ref, g_ref, o_ref):
    o_ref[...] = _rms_rows(x_ref[...]) * g_ref[...]


def _final_norm(x, g, tm):
    m, d = x.shape
    return pl.pallas_call(
        _final_norm_kernel,
        grid=(m // tm,),
        in_specs=[pl.BlockSpec((tm, d), lambda i: (i, 0)), pl.BlockSpec((1, d), lambda i: (0, 0))],
        out_specs=pl.BlockSpec((tm, d), lambda i: (i, 0)),
        out_shape=jax.ShapeDtypeStruct((m, d), F32),
        compiler_params=_cparams(("parallel",), 6 * _nbytes((tm, d), F32) + (4 << 20)),
        name="final_rmsnorm",
    )(x, g.reshape(1, d))


def _rope_tables(n):
    t = jnp.arange(n, dtype=jnp.int32)
    row = (t // GRID_W).astype(F32)
    col = (t % GRID_W).astype(F32)
    axis_dim = HEAD_DIM // 2
    inv = 1.0 / (ROPE_THETA ** (jnp.arange(0, axis_dim, 2, dtype=F32) / axis_dim))
    ar, ac = row[:, None] * inv, col[:, None] * inv
    cos = jnp.concatenate([jnp.cos(ar), jnp.cos(ar), jnp.cos(ac), jnp.cos(ac)], axis=-1)
    sin = jnp.concatenate([-jnp.sin(ar), jnp.sin(ar), -jnp.sin(ac), jnp.sin(ac)], axis=-1)
    return cos, sin


TM_PROJ = 512
TF_FFN = 1024
TQ_ATTN = 256
TK_ATTN = 512


def kernel(x, c, ctx, c_ctx, mod_w, mod_b, norm_mix_g, norm_ffn_g, ffn_w1, ffn_w2, ev_w_in, ev_q_norm, ev_k_norm, ev_conv_w, ev_w_out, od_w_in, od_conv_w, od_conv_b, od_f_w1, od_f_b1, od_f_freq, od_f_w2, od_f_b2, od_f_w3, od_f_decay, od_skip, od_w_out, final_g):
    batch, n, d = x.shape
    assert batch == 1
    n_ctx = ctx.shape[1]
    depth = mod_w.shape[0]
    last_ctx_read = ((depth - 1) // 2) * 2
    attn_w = N_Q_HEADS * HEAD_DIM
    kv_w = N_KV_HEADS * HEAD_DIM

    xs = x[0]
    xc = ctx[0]
    cond = jnp.zeros((SUBLANES, d), F32).at[0].set(c[0]).at[1].set(c_ctx)
    modv_all = _mod_vectors(cond, mod_w, mod_b)
    rope = _rope_tables(n)

    for l in range(depth):
        modv = modv_all[l]
        ctx_full = l < last_ctx_read
        w1 = ffn_w1[l].astype(BF16)
        w2 = ffn_w2[l].astype(BF16)
        if l % 2 == 0:
            e = l // 2
            w_in = ev_w_in[e].astype(BF16)
            w_out = ev_w_out[e].astype(BF16)
            p_l = _modmm(xs, norm_mix_g[l], modv, 0, 0, w_in, TM_PROJ, 1536)
            q_l, k_l, v_l = _qkv_prep(p_l, 0, attn_w // (2 * kv_w), ev_q_norm[e], ev_k_norm[e], rope, TM_PROJ)
            p_c = _modmm(xc, norm_mix_g[l], modv, 1, 0, w_in, TM_PROJ, 1536)
            q_c, k_c, v_c = _qkv_prep(p_c, 0, attn_w // (2 * kv_w), ev_q_norm[e], ev_k_norm[e], None, TM_PROJ)
            att_l = _attention(q_l, k_l, v_l, k_c, v_c, TQ_ATTN, TK_ATTN)
            conv_l = _short_conv_mixer(p_l, attn_w + 2 * kv_w, ev_conv_w[e])
            if ctx_full:
                att_c = _attention(q_c, k_c, v_c, None, None, TQ_ATTN, TK_ATTN)
                conv_c = _short_conv_mixer(p_c, attn_w + 2 * kv_w, ev_conv_w[e])
                xc = _outproj([att_c, conv_c], w_out, xc, modv, 1, 2, TM_PROJ)
            xs = _outproj([att_l, conv_l], w_out, xs, modv, 0, 2, TM_PROJ)
        else:
            o = l // 2
            w_in = od_w_in[o].astype(BF16)
            w_out = od_w_out[o].astype(BF16)
            fargs = (od_f_w1[o], od_f_b1[o], od_f_freq[o], od_f_w2[o], od_f_b2[o], od_f_w3[o], od_f_decay[o])
            p_l = _modmm(xs, norm_mix_g[l], modv, 0, 0, w_in, TM_PROJ, 1536)
            u, x0, usk = _hyena_prep(p_l, od_conv_w[o], od_conv_b[o], od_skip[o])
            filt2 = _hyena_filter(n, d, *fargs, tr=1024, tc=1024)
            y = _long_conv_gate(u, x0, usk, filt2, n)
            xs = _outproj([y], w_out, xs, modv, 0, 2, TM_PROJ)
            if ctx_full:
                p_c = _modmm(xc, norm_mix_g[l], modv, 1, 0, w_in, TM_PROJ, 1536)
                u, x0, usk = _hyena_prep(p_c, od_conv_w[o], od_conv_b[o], od_skip[o])
                filt2 = _hyena_filter(n_ctx, d, *fargs, tr=1024, tc=1024)
                y = _long_conv_gate_short(u, x0, usk, filt2, n_ctx)
                xc = _outproj([y], w_out, xc, modv, 1, 2, TM_PROJ)
        xs = _ffn(xs, norm_ffn_g[l], modv, 0, w1, w2, TM_PROJ, TF_FFN)
        if ctx_full:
            xc = _ffn(xc, norm_ffn_g[l], modv, 1, w1, w2, TM_PROJ, TF_FFN)

    return _final_norm(xs, final_g, TM_PROJ)[None]
```

```python
import functools
import math

import numpy as np
import jax
import jax.numpy as jnp
from jax import lax
from jax.experimental import pallas as pl
from jax.experimental.pallas import tpu as pltpu

F32 = jnp.float32
BF16 = jnp.bfloat16

HEAD_DIM = 128
N_Q_HEADS = 8
N_KV_HEADS = 2
Q_PER_KV = N_Q_HEADS // N_KV_HEADS
GRID_W = 64
ROPE_THETA = 10000.0
FILTER_BANDS = 16
FILTER_MOD_SHIFT = 0.05
EPS = 1e-6

LANES = 128
SUBLANES = 8
VMEM_BUDGET = 56 * 1024 * 1024
DFT_N1 = 128
ONES_ROWS = 16


def _cparams(sem, vmem_bytes):
    return pltpu.CompilerParams(dimension_semantics=sem,
                                vmem_limit_bytes=int(min(max(vmem_bytes, 16 << 20), VMEM_BUDGET)))


def _nbytes(shape, dtype):
    return int(np.prod(shape)) * jnp.dtype(dtype).itemsize


def _dot(a, b):
    return jnp.dot(a, b, preferred_element_type=F32)


def _dot_hi(a, b):
    return jnp.dot(a, b, preferred_element_type=F32, precision=lax.Precision.HIGHEST)


def _rms_rows(x):
    return x * lax.rsqrt(jnp.mean(x * x, axis=-1, keepdims=True) + EPS)


def _mod_kernel(c_ref, w_ref, b_ref, o_ref):
    c = c_ref[...]
    s = c * (1.0 / (1.0 + jnp.exp(-c)))
    o_ref[0] = _dot(s.astype(BF16), w_ref[0].astype(BF16)) + b_ref[0]


def _mod_vectors(cond, mod_w, mod_b):
    depth, d, n6 = mod_w.shape
    tn = 1536
    rows = cond.shape[0]
    vm = 2 * _nbytes((d, tn), F32) + _nbytes((d, tn), BF16) + (4 << 20)
    return pl.pallas_call(
        _mod_kernel,
        grid=(depth, n6 // tn),
        in_specs=[pl.BlockSpec((rows, d), lambda l, j: (0, 0)),
                  pl.BlockSpec((1, d, tn), lambda l, j: (l, 0, j)),
                  pl.BlockSpec((1, 1, tn), lambda l, j: (l, 0, j))],
        out_specs=pl.BlockSpec((1, rows, tn), lambda l, j: (l, 0, j)),
        out_shape=jax.ShapeDtypeStruct((depth, rows, n6), F32),
        compiler_params=_cparams(("parallel", "parallel"), vm),
        name="adaln_vectors",
    )(cond, mod_w, mod_b.reshape(depth, 1, n6))


def _modmm_kernel(x_ref, g_ref, sh_ref, sc_ref, w_ref, o_ref, h_ref, *, row):
    @pl.when(pl.program_id(1) == 0)
    def _():
        h = _rms_rows(x_ref[...]) * g_ref[...]
        h = h * (1.0 + sc_ref[row:row + 1, :]) + sh_ref[row:row + 1, :]
        h_ref[...] = h.astype(BF16)

    o_ref[...] = _dot(h_ref[...], w_ref[...]).astype(o_ref.dtype)


def _modmm(x, gain, modv, row, col0, w, tm, tn):
    m, d = x.shape
    n = w.shape[1]
    tm = min(tm, m)
    vm = (2 * _nbytes((tm, d), F32) + _nbytes((tm, d), BF16) + 2 * _nbytes((d, tn), BF16)
          + 2 * _nbytes((tm, tn), F32) + 2 * _nbytes((tm, d), F32) + (4 << 20))
    return pl.pallas_call(
        functools.partial(_modmm_kernel, row=row),
        grid=(m // tm, n // tn),
        in_specs=[pl.BlockSpec((tm, d), lambda i, j: (i, 0)),
                  pl.BlockSpec((1, d), lambda i, j: (0, 0)),
                  pl.BlockSpec((SUBLANES, d), lambda i, j: (0, col0)),
                  pl.BlockSpec((SUBLANES, d), lambda i, j: (0, col0 + 1)),
                  pl.BlockSpec((d, tn), lambda i, j: (0, j))],
        out_specs=pl.BlockSpec((tm, tn), lambda i, j: (i, j)),
        out_shape=jax.ShapeDtypeStruct((m, n), F32),
        scratch_shapes=[pltpu.VMEM((tm, d), BF16)],
        compiler_params=_cparams(("parallel", "arbitrary"), vm),
        name="modulated_projection",
    )(x, gain.reshape(1, d), modv, modv, w)


def _outproj_kernel(*refs, n_in, row):
    a_refs = refs[:n_in]
    w_refs = refs[n_in:2 * n_in]
    x_ref, g_ref, o_ref = refs[2 * n_in:]
    acc = _dot(a_refs[0][...].astype(BF16), w_refs[0][...])
    for a_ref, w_ref in zip(a_refs[1:], w_refs[1:]):
        acc = acc + _dot(a_ref[...].astype(BF16), w_ref[...])
    o_ref[...] = x_ref[...] + g_ref[row:row + 1, :] * acc


def _outproj(acts, w, x, modv, row, gate_col, tm):
    m, d = x.shape
    tm = min(tm, m)
    n_in = len(acts)
    in_specs, args = [], []
    for a in acts:
        in_specs.append(pl.BlockSpec((tm, a.shape[1]), lambda i: (i, 0)))
        args.append(a)
    row0 = 0
    for a in acts:
        k = a.shape[1]
        in_specs.append(pl.BlockSpec((k, d), functools.partial(lambda i, b: (b, 0), b=row0 // k)))
        args.append(w)
        row0 += k
    in_specs += [pl.BlockSpec((tm, d), lambda i: (i, 0)),
                 pl.BlockSpec((SUBLANES, d), lambda i: (0, gate_col))]
    args += [x, modv]
    vm = (2 * sum(_nbytes((tm, a.shape[1]), a.dtype) for a in acts) + 2 * _nbytes(w.shape, BF16)
          + 6 * _nbytes((tm, d), F32) + (4 << 20))
    return pl.pallas_call(
        functools.partial(_outproj_kernel, n_in=n_in, row=row),
        grid=(m // tm,),
        in_specs=in_specs,
        out_specs=pl.BlockSpec((tm, d), lambda i: (i, 0)),
        out_shape=jax.ShapeDtypeStruct((m, d), F32),
        compiler_params=_cparams(("parallel",), vm),
        name="gated_out_projection",
    )(*args)


def _ffn_kernel(x_ref, g_ref, sh_ref, sc_ref, gate_ref, w1_ref, w2_ref, o_ref, h_ref, acc_ref, *, row):
    j = pl.program_id(1)

    @pl.when(j == 0)
    def _():
        h = _rms_rows(x_ref[...]) * g_ref[...]
        h = h * (1.0 + sc_ref[row:row + 1, :]) + sh_ref[row:row + 1, :]
        h_ref[...] = h.astype(BF16)
        acc_ref[...] = jnp.zeros(acc_ref.shape, F32)

    a = jnp.maximum(_dot(h_ref[...], w1_ref[...]), 0.0)
    acc_ref[...] += _dot((a * a).astype(BF16), w2_ref[...])

    @pl.when(j == pl.num_programs(1) - 1)
    def _():
        o_ref[...] = x_ref[...] + gate_ref[row:row + 1, :] * acc_ref[...]


def _ffn(x, gain, modv, row, w1, w2, tm, tf):
    m, d = x.shape
    f = w1.shape[1]
    tm = min(tm, m)
    vm = (4 * _nbytes((tm, d), F32) + _nbytes((tm, d), BF16) + _nbytes((tm, d), F32)
          + 4 * _nbytes((d, tf), BF16) + 3 * _nbytes((tm, tf), F32) + (4 << 20))
    return pl.pallas_call(
        functools.partial(_ffn_kernel, row=row),
        grid=(m // tm, f // tf),
        in_specs=[pl.BlockSpec((tm, d), lambda i, j: (i, 0)),
                  pl.BlockSpec((1, d), lambda i, j: (0, 0)),
                  pl.BlockSpec((SUBLANES, d), lambda i, j: (0, 3)),
                  pl.BlockSpec((SUBLANES, d), lambda i, j: (0, 4)),
                  pl.BlockSpec((SUBLANES, d), lambda i, j: (0, 5)),
                  pl.BlockSpec((d, tf), lambda i, j: (0, j)),
                  pl.BlockSpec((tf, d), lambda i, j: (j, 0))],
        out_specs=pl.BlockSpec((tm, d), lambda i, j: (i, 0)),
        out_shape=jax.ShapeDtypeStruct((m, d), F32),
        scratch_shapes=[pltpu.VMEM((tm, d), BF16), pltpu.VMEM((tm, d), F32)],
        compiler_params=_cparams(("parallel", "arbitrary"), vm),
        name="gated_ffn",
    )(x, gain.reshape(1, d), modv, modv, modv, w1, w2)


def _rope(x, cos, sin):
    lane = lax.broadcasted_iota(jnp.int32, x.shape, 1)
    first_half = (lane % (HEAD_DIM // 2)) < (HEAD_DIM // 4)
    partner = jnp.where(first_half,
                        pltpu.roll(x, HEAD_DIM - HEAD_DIM // 4, 1),
                        pltpu.roll(x, HEAD_DIM // 4, 1))
    return x * cos + partner * sin


def _qkv_kernel(*refs, use_rope, q_scale):
    if use_rope:
        pq_ref, pkv_ref, qg_ref, kg_ref, cos_ref, sin_ref, q_ref, k_ref, v_ref = refs
        cos, sin = cos_ref[...], sin_ref[...]
    else:
        pq_ref, pkv_ref, qg_ref, kg_ref, q_ref, k_ref, v_ref = refs
    kv_w = N_KV_HEADS * HEAD_DIM
    for h in range(N_Q_HEADS):
        sl = slice(h * HEAD_DIM, (h + 1) * HEAD_DIM)
        xn = _rms_rows(pq_ref[:, sl]) * qg_ref[...]
        if use_rope:
            xn = _rope(xn, cos, sin)
        q_ref[:, sl] = (xn * q_scale).astype(BF16)
    for h in range(N_KV_HEADS):
        sl = slice(h * HEAD_DIM, (h + 1) * HEAD_DIM)
        xn = _rms_rows(pkv_ref[:, sl]) * kg_ref[...]
        if use_rope:
            xn = _rope(xn, cos, sin)
        k_ref[:, sl] = xn.astype(BF16)
    v_ref[...] = pkv_ref[:, kv_w:2 * kv_w].T.astype(BF16)


def _qkv_prep(p, q_col_block, kv_col_block, q_gain, k_gain, rope, tm):
    m = p.shape[0]
    tm = min(tm, m)
    attn_w = N_Q_HEADS * HEAD_DIM
    kv_w = N_KV_HEADS * HEAD_DIM
    in_specs = [pl.BlockSpec((tm, attn_w), lambda i: (i, q_col_block)),
                pl.BlockSpec((tm, 2 * kv_w), lambda i: (i, kv_col_block)),
                pl.BlockSpec((1, HEAD_DIM), lambda i: (0, 0)),
                pl.BlockSpec((1, HEAD_DIM), lambda i: (0, 0))]
    args = [p, p, q_gain.reshape(1, HEAD_DIM), k_gain.reshape(1, HEAD_DIM)]
    if rope is not None:
        in_specs += [pl.BlockSpec((tm, HEAD_DIM), lambda i: (i, 0))] * 2
        args += list(rope)
    vm = 4 * _nbytes((tm, attn_w + 2 * kv_w), F32) + (8 << 20)
    return pl.pallas_call(
        functools.partial(_qkv_kernel, use_rope=rope is not None, q_scale=HEAD_DIM ** -0.5),
        grid=(m // tm,),
        in_specs=in_specs,
        out_specs=[pl.BlockSpec((tm, attn_w), lambda i: (i, 0)),
                   pl.BlockSpec((tm, kv_w), lambda i: (i, 0)),
                   pl.BlockSpec((kv_w, tm), lambda i: (0, i))],
        out_shape=[jax.ShapeDtypeStruct((m, attn_w), BF16),
                   jax.ShapeDtypeStruct((m, kv_w), BF16),
                   jax.ShapeDtypeStruct((kv_w, m), BF16)],
        compiler_params=_cparams(("parallel",), vm),
        name="qkv_norm_rope",
    )(*args)


def _attn_kernel(*refs, has_extra, tq):
    if has_extra:
        q_ref, k_ref, vt_ref, ke_ref, vte_ref, o_ref, qs_ref, m_ref, acc_ref = refs
    else:
        q_ref, k_ref, vt_ref, o_ref, qs_ref, m_ref, acc_ref = refs
    kv = pl.program_id(2)

    def update(k, vt):
        s = lax.dot_general(k, qs_ref[...], (((1,), (1,)), ((), ())), preferred_element_type=F32)
        m_prev = m_ref[...]
        m_new = jnp.maximum(m_prev, jnp.max(s, axis=0, keepdims=True))
        alpha = jnp.exp(m_prev - m_new)
        p = jnp.exp(s - m_new).astype(BF16)
        vt1 = jnp.concatenate([vt, jnp.ones((ONES_ROWS, vt.shape[1]), BF16)], axis=0)
        acc_ref[...] = alpha * acc_ref[...] + _dot(vt1, p)
        m_ref[...] = m_new

    @pl.when(kv == 0)
    def _():
        for h in range(Q_PER_KV):
            qs_ref[h * tq:(h + 1) * tq, :] = q_ref[:, h * HEAD_DIM:(h + 1) * HEAD_DIM]
        m_ref[...] = jnp.full(m_ref.shape, -jnp.inf, F32)
        acc_ref[...] = jnp.zeros(acc_ref.shape, F32)
        if has_extra:
            update(ke_ref[...], vte_ref[...])

    update(k_ref[...], vt_ref[...])

    @pl.when(kv == pl.num_programs(2) - 1)
    def _():
        o_t = acc_ref[0:HEAD_DIM, :] / acc_ref[HEAD_DIM:HEAD_DIM + 1, :]
        for h in range(Q_PER_KV):
            o_ref[:, h * HEAD_DIM:(h + 1) * HEAD_DIM] = o_t[:, h * tq:(h + 1) * tq].T.astype(o_ref.dtype)


def _attention(q, k, vt, k_extra, vt_extra, tq, tk):
    m = q.shape[0]
    s_len = k.shape[0]
    tq = min(tq, m)
    tk = min(tk, s_len)
    gw = Q_PER_KV * HEAD_DIM
    cols = Q_PER_KV * tq
    has_extra = k_extra is not None
    in_specs = [pl.BlockSpec((tq, gw), lambda g, i, j: (i, g)),
                pl.BlockSpec((tk, HEAD_DIM), lambda g, i, j: (j, g)),
                pl.BlockSpec((HEAD_DIM, tk), lambda g, i, j: (g, j))]
    args = [q, k, vt]
    if has_extra:
        e = k_extra.shape[0]
        in_specs += [pl.BlockSpec((e, HEAD_DIM), lambda g, i, j: (0, g)),
                     pl.BlockSpec((HEAD_DIM, e), lambda g, i, j: (g, 0))]
        args += [k_extra, vt_extra]
    acc_rows = HEAD_DIM + ONES_ROWS
    vm = 4 * _nbytes((tk, cols), F32) + 6 * _nbytes((acc_rows, cols), F32) + (8 << 20)
    return pl.pallas_call(
        functools.partial(_attn_kernel, has_extra=has_extra, tq=tq),
        grid=(N_KV_HEADS, m // tq, s_len // tk),
        in_specs=in_specs,
        out_specs=pl.BlockSpec((tq, gw), lambda g, i, j: (i, g)),
        out_shape=jax.ShapeDtypeStruct((m, N_Q_HEADS * HEAD_DIM), BF16),
        scratch_shapes=[pltpu.VMEM((cols, HEAD_DIM), BF16),
                        pltpu.VMEM((1, cols), F32),
                        pltpu.VMEM((acc_rows, cols), F32)],
        compiler_params=_cparams(("parallel", "parallel", "arbitrary"), vm),
        name="gqa_flash_attention",
    )(*args)


def _conv3(z, w_ref):
    n = z.shape[0]
    row = lax.broadcasted_iota(jnp.int32, z.shape, 0)
    prev = jnp.where(row == 0, 0.0, pltpu.roll(z, 1, 0))
    nxt = jnp.where(row == n - 1, 0.0, pltpu.roll(z, n - 1, 0))
    return prev * w_ref[0:1, :] + z * w_ref[1:2, :] + nxt * w_ref[2:3, :]


def _convmix_kernel(u_ref, b_ref, c_ref, w_ref, o_ref):
    o_ref[...] = (b_ref[...] * _conv3(c_ref[...] * u_ref[...], w_ref)).astype(o_ref.dtype)


def _short_conv_mixer(p, col0, conv_w):
    m = p.shape[0]
    c = conv_w.shape[1]
    ct = LANES
    nb = c // ct
    b0 = col0 // ct
    vm = 8 * _nbytes((m, ct), F32) + 8 * _nbytes((m, ct), F32) + (4 << 20)
    return pl.pallas_call(
        _convmix_kernel,
        grid=(nb,),
        in_specs=[pl.BlockSpec((m, ct), lambda j: (0, b0 + j)),
                  pl.BlockSpec((m, ct), lambda j: (0, b0 + nb + j)),
                  pl.BlockSpec((m, ct), lambda j: (0, b0 + 2 * nb + j)),
                  pl.BlockSpec((3, ct), lambda j: (0, j))],
        out_specs=pl.BlockSpec((m, ct), lambda j: (0, j)),
        out_shape=jax.ShapeDtypeStruct((m, c), BF16),
        compiler_params=_cparams(("parallel",), vm),
        name="short_conv_mixer",
    )(p, p, p, conv_w)


def _hyena_prep_kernel(p0_ref, p1_ref, p2_ref, w0_ref, w1_ref, w2_ref, b0_ref, b1_ref, b2_ref,
                       skip_ref, u_ref, x0_ref, usk_ref):
    x0 = _conv3(p0_ref[...], w0_ref) + b0_ref[...]
    x1 = _conv3(p1_ref[...], w1_ref) + b1_ref[...]
    v = _conv3(p2_ref[...], w2_ref) + b2_ref[...]
    u = x1 * v
    u_ref[...] = u
    x0_ref[...] = x0
    usk_ref[...] = x0 * (u * skip_ref[...])


def _hyena_prep(p, conv_w, conv_b, skip):
    m = p.shape[0]
    c = skip.shape[0]
    ct = LANES
    nb = c // ct
    vm = 12 * _nbytes((m, ct), F32) + 10 * _nbytes((m, ct), F32) + (4 << 20)
    blk = lambda off: pl.BlockSpec((m, ct), functools.partial(lambda j, o: (0, o + j), o=off))
    wblk = lambda off: pl.BlockSpec((3, ct), functools.partial(lambda j, o: (0, o + j), o=off))
    bblk = lambda off: pl.BlockSpec((1, ct), functools.partial(lambda j, o: (0, o + j), o=off))
    out = jax.ShapeDtypeStruct((m, c), F32)
    return pl.pallas_call(
        _hyena_prep_kernel,
        grid=(nb,),
        in_specs=[blk(0), blk(nb), blk(2 * nb), wblk(0), wblk(nb), wblk(2 * nb),
                  bblk(0), bblk(nb), bblk(2 * nb), pl.BlockSpec((1, ct), lambda j: (0, j))],
        out_specs=[pl.BlockSpec((m, ct), lambda j: (0, j))] * 3,
        out_shape=[out, out, out],
        compiler_params=_cparams(("parallel",), vm),
        name="hyena_short_conv_gate",
    )(p, p, p, conv_w, conv_w, conv_w, conv_b.reshape(1, -1), conv_b.reshape(1, -1),
      conv_b.reshape(1, -1), skip.reshape(1, c))


def _filter_feats(n, order):
    t = np.asarray(order, np.float64)
    t_norm = t / max(n - 1, 1)
    bands = np.arange(1, FILTER_BANDS + 1, dtype=np.float64)
    ang = (2 * math.pi / n) * t[:, None] * bands[None, :]
    feats = np.concatenate([t_norm[:, None], np.cos(ang), np.sin(ang)], axis=-1)
    out = np.zeros((len(t), LANES), np.float32)
    out[:, :feats.shape[1]] = feats
    return out


def _filter_kernel(f_ref, w1_ref, b1_ref, fr_ref, w2_ref, b2_ref, w3_ref, dec_ref, o_ref, h_ref):
    @pl.when(pl.program_id(2) == 0)
    def _():
        fr = fr_ref[...]
        h = jnp.sin(fr * (_dot_hi(f_ref[0], w1_ref[...]) + b1_ref[...]))
        h_ref[...] = jnp.sin(fr * (_dot_hi(h, w2_ref[...]) + b2_ref[...]))

    t_norm = f_ref[0][:, 0:1]
    window = jnp.exp(-t_norm * jnp.abs(dec_ref[...])) + FILTER_MOD_SHIFT
    o_ref[...] = (_dot_hi(h_ref[...], w3_ref[...]) * window).astype(o_ref.dtype)


def _hyena_filter(n, ch, f_w1, f_b1, f_freq, f_w2, f_b2, f_w3, f_decay, tr, tc):
    hid = f_w1.shape[1]
    feats = np.stack([_filter_feats(n, np.arange(n)), _filter_feats(n, np.arange(n)[::-1])])
    w1p = jnp.zeros((LANES, hid), F32).at[:f_w1.shape[0]].set(f_w1)
    tr = min(tr, n)
    nrb = n // tr
    ncb = ch // tc
    vm = 4 * _nbytes((tr, tc), F32) + 2 * _nbytes((tr, LANES), F32) + (8 << 20)
    return pl.pallas_call(
        _filter_kernel,
        grid=(2, nrb, ncb),
        in_specs=[pl.BlockSpec((1, tr, LANES), lambda s, i, j: (s, i, 0)),
                  pl.BlockSpec((LANES, hid), lambda s, i, j: (0, 0)),
                  pl.BlockSpec((1, hid), lambda s, i, j: (0, 0)),
                  pl.BlockSpec((1, hid), lambda s, i, j: (0, 0)),
                  pl.BlockSpec((hid, hid), lambda s, i, j: (0, 0)),
                  pl.BlockSpec((1, hid), lambda s, i, j: (0, 0)),
                  pl.BlockSpec((hid, tc), lambda s, i, j: (0, s * ncb + j)),
                  pl.BlockSpec((1, tc), lambda s, i, j: (0, s * ncb + j))],
        out_specs=pl.BlockSpec((tr, tc), lambda s, i, j: (s * nrb + i, j)),
        out_shape=jax.ShapeDtypeStruct((2 * n, ch), F32),
        scratch_shapes=[pltpu.VMEM((tr, hid), F32)],
        compiler_params=_cparams(("parallel", "parallel", "arbitrary"), vm),
        name="hyena_filter",
    )(jnp.asarray(feats), w1p, f_b1.reshape(1, hid), f_freq.reshape(1, hid), f_w2,
      f_b2.reshape(1, hid), f_w3, f_decay.reshape(1, -1))


def _dft_geometry(n):
    big_n = 2 * n
    n2 = big_n // DFT_N1
    nh = -(-(n2 // 2 + 1) // SUBLANES) * SUBLANES
    return big_n, n2, nh


@functools.lru_cache(maxsize=None)
def _stage1_matrix(n, rows_n2):
    big_n, _, nh = _dft_geometry(n)
    groups = DFT_N1 // SUBLANES
    out = np.zeros((groups, 2, nh, SUBLANES, rows_n2, SUBLANES), np.float32)
    k2 = np.arange(nh, dtype=np.float64)[:, None]
    n2v = np.arange(rows_n2, dtype=np.float64)[None, :]
    for g in range(groups):
        for j in range(SUBLANES):
            t = SUBLANES * g + j + DFT_N1 * n2v
            ang = 2 * math.pi * np.mod(k2 * t, big_n) / big_n
            out[g, 0, :, j, :, j] = np.cos(ang)
            out[g, 1, :, j, :, j] = -np.sin(ang)
    return out.reshape(groups, 2 * nh * SUBLANES, rows_n2 * SUBLANES).astype(BF16)


@functools.lru_cache(maxsize=None)
def _stage2_matrices():
    idx = np.arange(DFT_N1, dtype=np.float64)
    ang = 2 * math.pi * np.mod(np.outer(idx, idx), DFT_N1) / DFT_N1
    c, s = np.cos(ang), np.sin(ang)
    fwd = np.block([[c, s], [-s, c]])
    inv = np.block([[c, -s], [s, c]])
    return fwd.astype(BF16), inv.astype(BF16)


@functools.lru_cache(maxsize=None)
def _stage4_matrix(n):
    big_n, n2c, nh = _dft_geometry(n)
    groups = DFT_N1 // SUBLANES
    rows_n2 = n2c // 2
    out = np.zeros((groups, rows_n2, SUBLANES, 2, nh, SUBLANES), np.float32)
    k2 = np.arange(nh, dtype=np.float64)[None, :]
    wgt = np.where(k2 <= n2c // 2, 2.0, 0.0)
    wgt[0, 0] = 1.0
    wgt[0, n2c // 2] = 1.0
    n2v = np.arange(rows_n2, dtype=np.float64)[:, None]
    for g in range(groups):
        for j in range(SUBLANES):
            t = SUBLANES * g + j + DFT_N1 * n2v
            ang = 2 * math.pi * np.mod(k2 * t, big_n) / big_n
            out[g, :, j, 0, :, j] = wgt * np.cos(ang) / big_n
            out[g, :, j, 1, :, j] = -wgt * np.sin(ang) / big_n
    return out.reshape(groups, rows_n2 * SUBLANES, 2 * nh * SUBLANES).astype(BF16)


def _dft_stage1_kernel(x_ref, l_ref, o_ref, *, gsteps):
    nb, _, ct = x_ref.shape
    rows = o_ref.shape[0]
    for gi in range(gsteps):
        sl = slice(SUBLANES * gi, SUBLANES * (gi + 1))
        xv = x_ref[:, sl, :].reshape(nb * SUBLANES, ct).astype(BF16)
        out = _dot(l_ref[gi], xv)
        o_ref[:, sl, :] = out.reshape(rows, SUBLANES, ct)


def _dft_stage1(x3, n, gsteps, ct):
    rows_n2, _, c = x3.shape
    _, _, nh = _dft_geometry(n)
    mat = _stage1_matrix(n, rows_n2)
    groups = mat.shape[0]
    ct = min(ct, c)
    gw = SUBLANES * gsteps
    vm = (2 * _nbytes((rows_n2, gw, ct), F32) + 2 * _nbytes((gsteps,) + mat.shape[1:], BF16)
          + 2 * _nbytes((2 * nh, gw, ct), F32) + 3 * _nbytes((2 * nh * SUBLANES, ct), F32) + (4 << 20))
    return pl.pallas_call(
        functools.partial(_dft_stage1_kernel, gsteps=gsteps),
        grid=(groups // gsteps, c // ct),
        in_specs=[pl.BlockSpec((rows_n2, gw, ct), lambda g, j: (0, g, j)),
                  pl.BlockSpec((gsteps,) + mat.shape[1:], lambda g, j: (g, 0, 0))],
        out_specs=pl.BlockSpec((2 * nh, gw, ct), lambda g, j: (0, g, j)),
        out_shape=jax.ShapeDtypeStruct((2 * nh, DFT_N1, c), F32),
        compiler_params=_cparams(("parallel", "parallel"), vm),
        name="dft_stage_n2",
    )(x3, mat)


def _stack_parts(a_ref, kb):
    cols = [jnp.concatenate([a_ref[0, k], a_ref[1, k]], axis=0) for k in range(kb)]
    return jnp.concatenate(cols, axis=1).astype(BF16)


def _filter_spectrum_kernel(a_ref, l2_ref, o_ref, *, kb):
    ct = a_ref.shape[-1]
    x = _dot(l2_ref[...], _stack_parts(a_ref, kb))
    for k in range(kb):
        o_ref[0, k] = x[:DFT_N1, k * ct:(k + 1) * ct]
        o_ref[1, k] = x[DFT_N1:, k * ct:(k + 1) * ct]


def _spectral_product_kernel(a_ref, kf_ref, l2_ref, l3_ref, o_ref, *, kb):
    ct = a_ref.shape[-1]
    x = _dot(l2_ref[...], _stack_parts(a_ref, kb))
    ys = []
    for k in range(kb):
        xr = x[:DFT_N1, k * ct:(k + 1) * ct]
        xi = x[DFT_N1:, k * ct:(k + 1) * ct]
        kr, ki = kf_ref[0, k], kf_ref[1, k]
        ys.append(jnp.concatenate([xr * kr - xi * ki, xr * ki + xi * kr], axis=0))
    v = _dot(l3_ref[...], jnp.concatenate(ys, axis=1).astype(BF16))
    for k in range(kb):
        o_ref[0, k] = v[:DFT_N1, k * ct:(k + 1) * ct]
        o_ref[1, k] = v[DFT_N1:, k * ct:(k + 1) * ct]


def _dft_stage2(a4, kf4, kb, ct):
    _, nh, _, c = a4.shape
    fwd, inv = _stage2_matrices()
    ct = min(ct, c)
    blk = pl.BlockSpec((2, kb, DFT_N1, ct), lambda g, j: (0, g, 0, j))
    mblk = pl.BlockSpec((2 * DFT_N1, 2 * DFT_N1), lambda g, j: (0, 0))
    nblk = 2 if kf4 is None else 3
    vm = 2 * nblk * _nbytes((2, kb, DFT_N1, ct), F32) + 8 * _nbytes((2 * DFT_N1, kb * ct), F32) + (4 << 20)
    if kf4 is None:
        kern, in_specs, args = _filter_spectrum_kernel, [blk, mblk], (a4, fwd)
        name = "dft_stage_n1_filter"
    else:
        kern, in_specs, args = _spectral_product_kernel, [blk, blk, mblk, mblk], (a4, kf4, fwd, inv)
        name = "dft_stage_n1_product"
    return pl.pallas_call(
        functools.partial(kern, kb=kb),
        grid=(nh // kb, c // ct),
        in_specs=in_specs,
        out_specs=blk,
        out_shape=jax.ShapeDtypeStruct(a4.shape, F32),
        compiler_params=_cparams(("parallel", "parallel"), vm),
        name=name,
    )(*args)


def _dft_stage4_kernel(v_ref, l_ref, x0_ref, usk_ref, o_ref, *, gsteps):
    rows, _, ct = v_ref.shape
    nb = o_ref.shape[0]
    for gi in range(gsteps):
        sl = slice(SUBLANES * gi, SUBLANES * (gi + 1))
        vv = v_ref[:, sl, :].reshape(rows * SUBLANES, ct).astype(BF16)
        y = _dot(l_ref[gi], vv).reshape(nb, SUBLANES, ct)
        o_ref[:, sl, :] = x0_ref[:, sl, :] * y + usk_ref[:, sl, :]


def _dft_stage4(v3, x03, usk3, n, gsteps, ct):
    rows, _, c = v3.shape
    mat = _stage4_matrix(n)
    groups = mat.shape[0]
    nb = x03.shape[0]
    ct = min(ct, c)
    gw = SUBLANES * gsteps
    vm = (2 * _nbytes((rows, gw, ct), F32) + 2 * _nbytes((gsteps,) + mat.shape[1:], BF16)
          + 6 * _nbytes((nb, gw, ct), F32) + 3 * _nbytes((rows * SUBLANES, ct), F32) + (4 << 20))
    oblk = pl.BlockSpec((nb, gw, ct), lambda g, j: (0, g, j))
    return pl.pallas_call(
        functools.partial(_dft_stage4_kernel, gsteps=gsteps),
        grid=(groups // gsteps, c // ct),
        in_specs=[pl.BlockSpec((rows, gw, ct), lambda g, j: (0, g, j)),
                  pl.BlockSpec((gsteps,) + mat.shape[1:], lambda g, j: (g, 0, 0)),
                  oblk, oblk],
        out_specs=oblk,
        out_shape=jax.ShapeDtypeStruct((nb, DFT_N1, c), F32),
        compiler_params=_cparams(("parallel", "parallel"), vm),
        name="dft_stage_k2_inverse",
    )(v3, mat, x03, usk3)


def _long_conv_gate(u, x0, usk, filt2, n):
    c = u.shape[1]
    _, _, nh = _dft_geometry(n)
    as3 = lambda a: a.reshape(a.shape[0] // DFT_N1, DFT_N1, c)
    kf = _dft_stage1(as3(filt2), n, gsteps=2, ct=512)
    kf = _dft_stage2(kf.reshape(2, nh, DFT_N1, c), None, kb=SUBLANES, ct=256)
    a = _dft_stage1(as3(u), n, gsteps=2, ct=512)
    v = _dft_stage2(a.reshape(2, nh, DFT_N1, c), kf, kb=SUBLANES, ct=256)
    y = _dft_stage4(v.reshape(2 * nh, DFT_N1, c), as3(x0), as3(usk), n, gsteps=2, ct=512)
    return y.reshape(n, c)


@functools.lru_cache(maxsize=None)
def _dense_dft_matrices(n):
    big_n = 2 * n
    nf = n + 1
    nfp = -(-nf // SUBLANES) * SUBLANES
    k = np.arange(nf, dtype=np.float64)[:, None]
    t = np.arange(big_n, dtype=np.float64)[None, :]
    ang = 2 * math.pi * np.mod(k * t, big_n) / big_n
    fwd = np.zeros((2, nfp, big_n), np.float32)
    fwd[0, :nf] = np.cos(ang)
    fwd[1, :nf] = -np.sin(ang)
    wgt = np.full((nf, 1), 2.0)
    wgt[0] = 1.0
    wgt[n] = 1.0
    inv = np.zeros((2, nfp, n), np.float32)
    inv[0, :nf] = (wgt * np.cos(ang) / big_n)[:, :n]
    inv[1, :nf] = (-wgt * np.sin(ang) / big_n)[:, :n]
    fwd = fwd.reshape(2 * nfp, big_n)
    inv = inv.reshape(2 * nfp, n).T
    return (np.ascontiguousarray(fwd[:, :n]).astype(BF16), fwd.astype(BF16),
            np.ascontiguousarray(inv).astype(BF16))


def _short_conv_gate_kernel(u_ref, f_ref, x0_ref, usk_ref, lu_ref, lf_ref, li_ref, o_ref):
    half = lu_ref.shape[0] // 2
    xs = _dot(lu_ref[...], u_ref[...].astype(BF16))
    ks = _dot(lf_ref[...], f_ref[...].astype(BF16))
    xr, xi, kr, ki = xs[:half], xs[half:], ks[:half], ks[half:]
    y = jnp.concatenate([xr * kr - xi * ki, xr * ki + xi * kr], axis=0).astype(BF16)
    o_ref[...] = x0_ref[...] * _dot(li_ref[...], y) + usk_ref[...]


def _long_conv_gate_short(u, x0, usk, filt2, n):
    c = u.shape[1]
    lu, lf, li = _dense_dft_matrices(n)
    ct = 256
    blk = pl.BlockSpec((n, ct), lambda j: (0, j))
    full = lambda a: pl.BlockSpec(a.shape, lambda j: (0, 0))
    vm = 32 << 20
    return pl.pallas_call(
        _short_conv_gate_kernel,
        grid=(c // ct,),
        in_specs=[blk, pl.BlockSpec((2 * n, ct), lambda j: (0, j)), blk, blk, full(lu), full(lf), full(li)],
        out_specs=blk,
        out_shape=jax.ShapeDtypeStruct((n, c), F32),
        compiler_params=_cparams(("parallel",), vm),
        name="context_long_conv",
    )(u, filt2, x0, usk, lu, lf, li)


def _final_norm_kernel(x_ref, g_ref, o_ref):
    o_ref[...] = _rms_rows(x_ref[...]) * g_ref[...]


def _final_norm(x, g, tm):
    m, d = x.shape
    return pl.pallas_call(
        _final_norm_kernel,
        grid=(m // tm,),
        in_specs=[pl.BlockSpec((tm, d), lambda i: (i, 0)), pl.BlockSpec((1, d), lambda i: (0, 0))],
        out_specs=pl.BlockSpec((tm, d), lambda i: (i, 0)),
        out_shape=jax.ShapeDtypeStruct((m, d), F32),
        compiler_params=_cparams(("parallel",), 6 * _nbytes((tm, d), F32) + (4 << 20)),
        name="final_rmsnorm",
    )(x, g.reshape(1, d))


def _rope_tables(n):
    t = jnp.arange(n, dtype=jnp.int32)
    row = (t // GRID_W).astype(F32)
    col = (t % GRID_W).astype(F32)
    axis_dim = HEAD_DIM // 2
    inv = 1.0 / (ROPE_THETA ** (jnp.arange(0, axis_dim, 2, dtype=F32) / axis_dim))
    ar, ac = row[:, None] * inv, col[:, None] * inv
    cos = jnp.concatenate([jnp.cos(ar), jnp.cos(ar), jnp.cos(ac), jnp.cos(ac)], axis=-1)
    sin = jnp.concatenate([-jnp.sin(ar), jnp.sin(ar), -jnp.sin(ac), jnp.sin(ac)], axis=-1)
    return cos, sin


TM_PROJ = 512
TF_FFN = 1024
TQ_ATTN = 256
TK_ATTN = 1024


def kernel(x, c, ctx, c_ctx, mod_w, mod_b, norm_mix_g, norm_ffn_g, ffn_w1, ffn_w2, ev_w_in, ev_q_norm, ev_k_norm, ev_conv_w, ev_w_out, od_w_in, od_conv_w, od_conv_b, od_f_w1, od_f_b1, od_f_freq, od_f_w2, od_f_b2, od_f_w3, od_f_decay, od_skip, od_w_out, final_g):
    batch, n, d = x.shape
    assert batch == 1
    n_ctx = ctx.shape[1]
    depth = mod_w.shape[0]
    last_ctx_read = ((depth - 1) // 2) * 2
    attn_w = N_Q_HEADS * HEAD_DIM
    kv_w = N_KV_HEADS * HEAD_DIM

    xs = x[0]
    xc = ctx[0]
    cond = jnp.zeros((SUBLANES, d), F32).at[0].set(c[0]).at[1].set(c_ctx)
    modv_all = _mod_vectors(cond, mod_w, mod_b)
    rope = _rope_tables(n)

    for l in range(depth):
        modv = modv_all[l]
        ctx_full = l < last_ctx_read
        w1 = ffn_w1[l].astype(BF16)
        w2 = ffn_w2[l].astype(BF16)
        if l % 2 == 0:
            e = l // 2
            w_in = ev_w_in[e].astype(BF16)
            w_out = ev_w_out[e].astype(BF16)
            p_l = _modmm(xs, norm_mix_g[l], modv, 0, 0, w_in, TM_PROJ, 1536)
            q_l, k_l, v_l = _qkv_prep(p_l, 0, attn_w // (2 * kv_w), ev_q_norm[e], ev_k_norm[e], rope, TM_PROJ)
            p_c = _modmm(xc, norm_mix_g[l], modv, 1, 0, w_in, TM_PROJ, 1536)
            q_c, k_c, v_c = _qkv_prep(p_c, 0, attn_w // (2 * kv_w), ev_q_norm[e], ev_k_norm[e], None, TM_PROJ)
            att_l = _attention(q_l, k_l, v_l, k_c, v_c, TQ_ATTN, TK_ATTN)
            conv_l = _short_conv_mixer(p_l, attn_w + 2 * kv_w, ev_conv_w[e])
            if ctx_full:
                att_c = _attention(q_c, k_c, v_c, None, None, TQ_ATTN, TK_ATTN)
                conv_c = _short_conv_mixer(p_c, attn_w + 2 * kv_w, ev_conv_w[e])
                xc = _outproj([att_c, conv_c], w_out, xc, modv, 1, 2, TM_PROJ)
            xs = _outproj([att_l, conv_l], w_out, xs, modv, 0, 2, TM_PROJ)
        else:
            o = l // 2
            w_in = od_w_in[o].astype(BF16)
            w_out = od_w_out[o].astype(BF16)
            fargs = (od_f_w1[o], od_f_b1[o], od_f_freq[o], od_f_w2[o], od_f_b2[o], od_f_w3[o], od_f_decay[o])
            p_l = _modmm(xs, norm_mix_g[l], modv, 0, 0, w_in, TM_PROJ, 1536)
            u, x0, usk = _hyena_prep(p_l, od_conv_w[o], od_conv_b[o], od_skip[o])
            filt2 = _hyena_filter(n, d, *fargs, tr=1024, tc=1024)
            y = _long_conv_gate(u, x0, usk, filt2, n)
            xs = _outproj([y], w_out, xs, modv, 0, 2, TM_PROJ)
            if ctx_full:
                p_c = _modmm(xc, norm_mix_g[l], modv, 1, 0, w_in, TM_PROJ, 1536)
                u, x0, usk = _hyena_prep(p_c, od_conv_w[o], od_conv_b[o], od_skip[o])
                filt2 = _hyena_filter(n_ctx, d, *fargs, tr=1024, tc=1024)
                y = _long_conv_gate_short(u, x0, usk, filt2, n_ctx)
                xc = _outproj([y], w_out, xc, modv, 1, 2, TM_PROJ)
        xs = _ffn(xs, norm_ffn_g[l], modv, 0, w1, w2, TM_PROJ, TF_FFN)
        if ctx_full:
            xc = _ffn(xc, norm_ffn_g[l], modv, 1, w1, w2, TM_PROJ, TF_FFN)

    return _final_norm(xs, final_g, TM_PROJ)[None]
```

```python
import functools
import math

import numpy as np
import jax
import jax.numpy as jnp
from jax import lax
from jax.experimental import pallas as pl
from jax.experimental.pallas import tpu as pltpu

F32 = jnp.float32
BF16 = jnp.bfloat16

HEAD_DIM = 128
N_Q_HEADS = 8
N_KV_HEADS = 2
Q_PER_KV = N_Q_HEADS // N_KV_HEADS
GRID_W = 64
ROPE_THETA = 10000.0
FILTER_BANDS = 16
FILTER_MOD_SHIFT = 0.05
EPS = 1e-6

LANES = 128
SUBLANES = 8
VMEM_BUDGET = 56 * 1024 * 1024
DFT_N1 = 128
ONES_ROWS = 16


def _cparams(sem, vmem_bytes):
    return pltpu.CompilerParams(dimension_semantics=sem,
                                vmem_limit_bytes=int(min(max(vmem_bytes, 16 << 20), VMEM_BUDGET)))


def _nbytes(shape, dtype):
    return int(np.prod(shape)) * jnp.dtype(dtype).itemsize


def _dot(a, b):
    return jnp.dot(a, b, preferred_element_type=F32)


def _rms_rows(x):
    return x * lax.rsqrt(jnp.mean(x * x, axis=-1, keepdims=True) + EPS)


def _mod_kernel(c_ref, w_ref, b_ref, o_ref):
    c = c_ref[...]
    s = c * (1.0 / (1.0 + jnp.exp(-c)))
    o_ref[0] = _dot(s.astype(BF16), w_ref[0].astype(BF16)) + b_ref[0]


def _mod_vectors(cond, mod_w, mod_b):
    depth, d, n6 = mod_w.shape
    tn = 1536
    rows = cond.shape[0]
    vm = 2 * _nbytes((d, tn), F32) + _nbytes((d, tn), BF16) + (4 << 20)
    return pl.pallas_call(
        _mod_kernel,
        grid=(depth, n6 // tn),
        in_specs=[pl.BlockSpec((rows, d), lambda l, j: (0, 0)),
                  pl.BlockSpec((1, d, tn), lambda l, j: (l, 0, j)),
                  pl.BlockSpec((1, 1, tn), lambda l, j: (l, 0, j))],
        out_specs=pl.BlockSpec((1, rows, tn), lambda l, j: (l, 0, j)),
        out_shape=jax.ShapeDtypeStruct((depth, rows, n6), F32),
        compiler_params=_cparams(("parallel", "parallel"), vm),
        name="adaln_vectors",
    )(cond, mod_w, mod_b.reshape(depth, 1, n6))


def _modmm_kernel(x_ref, g_ref, sh_ref, sc_ref, w_ref, o_ref, h_ref, *, row):
    @pl.when(pl.program_id(1) == 0)
    def _():
        h = _rms_rows(x_ref[...]) * g_ref[...]
        h = h * (1.0 + sc_ref[row:row + 1, :]) + sh_ref[row:row + 1, :]
        h_ref[...] = h.astype(BF16)

    o_ref[...] = _dot(h_ref[...], w_ref[...]).astype(o_ref.dtype)


def _modmm(x, gains, modv, l, row, col0, w, wi, tm, tn):
    m, d = x.shape
    n = w.shape[2]
    tm = min(tm, m)
    vm = (2 * _nbytes((tm, d), F32) + _nbytes((tm, d), BF16) + 2 * _nbytes((d, tn), BF16)
          + 2 * _nbytes((tm, tn), F32) + 2 * _nbytes((tm, d), F32) + (4 << 20))
    return pl.pallas_call(
        functools.partial(_modmm_kernel, row=row),
        grid=(m // tm, n // tn),
        in_specs=[pl.BlockSpec((tm, d), lambda i, j: (i, 0)),
                  pl.BlockSpec((None, 1, d), lambda i, j: (l, 0, 0)),
                  pl.BlockSpec((None, SUBLANES, d), lambda i, j: (l, 0, col0)),
                  pl.BlockSpec((None, SUBLANES, d), lambda i, j: (l, 0, col0 + 1)),
                  pl.BlockSpec((None, d, tn), lambda i, j: (wi, 0, j))],
        out_specs=pl.BlockSpec((tm, tn), lambda i, j: (i, j)),
        out_shape=jax.ShapeDtypeStruct((m, n), F32),
        scratch_shapes=[pltpu.VMEM((tm, d), BF16)],
        compiler_params=_cparams(("parallel", "arbitrary"), vm),
        name="modulated_projection",
    )(x, gains, modv, modv, w)


def _outproj_kernel(*refs, n_in, row):
    a_refs = refs[:n_in]
    w_refs = refs[n_in:2 * n_in]
    x_ref, g_ref, o_ref = refs[2 * n_in:]
    acc = _dot(a_refs[0][...].astype(BF16), w_refs[0][...])
    for a_ref, w_ref in zip(a_refs[1:], w_refs[1:]):
        acc = acc + _dot(a_ref[...].astype(BF16), w_ref[...])
    o_ref[...] = x_ref[...] + g_ref[row:row + 1, :] * acc


def _outproj(acts, w, wi, x, modv, l, row, gate_col, tm):
    m, d = x.shape
    tm = min(tm, m)
    n_in = len(acts)
    in_specs, args = [], []
    for a in acts:
        in_specs.append(pl.BlockSpec((tm, a.shape[1]), lambda i: (i, 0)))
        args.append(a)
    row0 = 0
    for a in acts:
        k = a.shape[1]
        in_specs.append(pl.BlockSpec((None, k, d), functools.partial(lambda i, b: (wi, b, 0), b=row0 // k)))
        args.append(w)
        row0 += k
    in_specs += [pl.BlockSpec((tm, d), lambda i: (i, 0)),
                 pl.BlockSpec((None, SUBLANES, d), lambda i: (l, 0, gate_col))]
    args += [x, modv]
    vm = (2 * sum(_nbytes((tm, a.shape[1]), a.dtype) for a in acts) + 2 * _nbytes(w.shape[1:], BF16)
          + 6 * _nbytes((tm, d), F32) + (4 << 20))
    return pl.pallas_call(
        functools.partial(_outproj_kernel, n_in=n_in, row=row),
        grid=(m // tm,),
        in_specs=in_specs,
        out_specs=pl.BlockSpec((tm, d), lambda i: (i, 0)),
        out_shape=jax.ShapeDtypeStruct((m, d), F32),
        compiler_params=_cparams(("parallel",), vm),
        name="gated_out_projection",
    )(*args)


def _ffn_kernel(*refs, row, final_norm):
    if final_norm:
        x_ref, g_ref, sh_ref, sc_ref, gate_ref, w1_ref, w2_ref, fg_ref, o_ref, h_ref, acc_ref = refs
    else:
        x_ref, g_ref, sh_ref, sc_ref, gate_ref, w1_ref, w2_ref, o_ref, h_ref, acc_ref = refs
    j = pl.program_id(1)

    @pl.when(j == 0)
    def _():
        h = _rms_rows(x_ref[...]) * g_ref[...]
        h = h * (1.0 + sc_ref[row:row + 1, :]) + sh_ref[row:row + 1, :]
        h_ref[...] = h.astype(BF16)
        acc_ref[...] = jnp.zeros(acc_ref.shape, F32)

    a = jnp.maximum(_dot(h_ref[...], w1_ref[...]), 0.0)
    acc_ref[...] += _dot((a * a).astype(BF16), w2_ref[...])

    @pl.when(j == pl.num_programs(1) - 1)
    def _():
        y = x_ref[...] + gate_ref[row:row + 1, :] * acc_ref[...]
        if final_norm:
            y = _rms_rows(y) * fg_ref[...]
        o_ref[...] = y


def _ffn(x, gains, modv, l, row, w1, w2, tm, tf, final_gain=None):
    m, d = x.shape
    f = w1.shape[2]
    tm = min(tm, m)
    vm = (4 * _nbytes((tm, d), F32) + _nbytes((tm, d), BF16) + _nbytes((tm, d), F32)
          + 4 * _nbytes((d, tf), BF16) + 3 * _nbytes((tm, tf), F32) + (4 << 20))
    in_specs = [pl.BlockSpec((tm, d), lambda i, j: (i, 0)),
                pl.BlockSpec((None, 1, d), lambda i, j: (l, 0, 0)),
                pl.BlockSpec((None, SUBLANES, d), lambda i, j: (l, 0, 3)),
                pl.BlockSpec((None, SUBLANES, d), lambda i, j: (l, 0, 4)),
                pl.BlockSpec((None, SUBLANES, d), lambda i, j: (l, 0, 5)),
                pl.BlockSpec((None, d, tf), lambda i, j: (l, 0, j)),
                pl.BlockSpec((None, tf, d), lambda i, j: (l, j, 0))]
    args = [x, gains, modv, modv, modv, w1, w2]
    if final_gain is not None:
        in_specs.append(pl.BlockSpec((1, d), lambda i, j: (0, 0)))
        args.append(final_gain.reshape(1, d))
    return pl.pallas_call(
        functools.partial(_ffn_kernel, row=row, final_norm=final_gain is not None),
        grid=(m // tm, f // tf),
        in_specs=in_specs,
        out_specs=pl.BlockSpec((tm, d), lambda i, j: (i, 0)),
        out_shape=jax.ShapeDtypeStruct((m, d), F32),
        scratch_shapes=[pltpu.VMEM((tm, d), BF16), pltpu.VMEM((tm, d), F32)],
        compiler_params=_cparams(("parallel", "arbitrary"), vm),
        name="gated_ffn",
    )(*args)


def _rope(x, cos, sin):
    lane = lax.broadcasted_iota(jnp.int32, x.shape, 1)
    first_half = (lane % (HEAD_DIM // 2)) < (HEAD_DIM // 4)
    partner = jnp.where(first_half,
                        pltpu.roll(x, HEAD_DIM - HEAD_DIM // 4, 1),
                        pltpu.roll(x, HEAD_DIM // 4, 1))
    return x * cos + partner * sin


def _qkv_kernel(*refs, use_rope, q_scale):
    if use_rope:
        pq_ref, pkv_ref, qg_ref, kg_ref, cos_ref, sin_ref, q_ref, k_ref, v_ref = refs
        cos, sin = cos_ref[...], sin_ref[...]
    else:
        pq_ref, pkv_ref, qg_ref, kg_ref, q_ref, k_ref, v_ref = refs
    kv_w = N_KV_HEADS * HEAD_DIM
    for h in range(N_Q_HEADS):
        sl = slice(h * HEAD_DIM, (h + 1) * HEAD_DIM)
        xn = _rms_rows(pq_ref[:, sl]) * qg_ref[...]
        if use_rope:
            xn = _rope(xn, cos, sin)
        q_ref[:, sl] = (xn * q_scale).astype(BF16)
    for h in range(N_KV_HEADS):
        sl = slice(h * HEAD_DIM, (h + 1) * HEAD_DIM)
        xn = _rms_rows(pkv_ref[:, sl]) * kg_ref[...]
        if use_rope:
            xn = _rope(xn, cos, sin)
        k_ref[:, sl] = xn.astype(BF16)
    v_ref[...] = pkv_ref[:, kv_w:2 * kv_w].T.astype(BF16)


def _qkv_prep(p, q_col_block, kv_col_block, q_gain, k_gain, rope, tm):
    m = p.shape[0]
    tm = min(tm, m)
    attn_w = N_Q_HEADS * HEAD_DIM
    kv_w = N_KV_HEADS * HEAD_DIM
    in_specs = [pl.BlockSpec((tm, attn_w), lambda i: (i, q_col_block)),
                pl.BlockSpec((tm, 2 * kv_w), lambda i: (i, kv_col_block)),
                pl.BlockSpec((1, HEAD_DIM), lambda i: (0, 0)),
                pl.BlockSpec((1, HEAD_DIM), lambda i: (0, 0))]
    args = [p, p, q_gain.reshape(1, HEAD_DIM), k_gain.reshape(1, HEAD_DIM)]
    if rope is not None:
        in_specs += [pl.BlockSpec((tm, HEAD_DIM), lambda i: (i, 0))] * 2
        args += list(rope)
    vm = 4 * _nbytes((tm, attn_w + 2 * kv_w), F32) + (8 << 20)
    return pl.pallas_call(
        functools.partial(_qkv_kernel, use_rope=rope is not None, q_scale=HEAD_DIM ** -0.5),
        grid=(m // tm,),
        in_specs=in_specs,
        out_specs=[pl.BlockSpec((tm, attn_w), lambda i: (i, 0)),
                   pl.BlockSpec((tm, kv_w), lambda i: (i, 0)),
                   pl.BlockSpec((kv_w, tm), lambda i: (0, i))],
        out_shape=[jax.ShapeDtypeStruct((m, attn_w), BF16),
                   jax.ShapeDtypeStruct((m, kv_w), BF16),
                   jax.ShapeDtypeStruct((kv_w, m), BF16)],
        compiler_params=_cparams(("parallel",), vm),
        name="qkv_norm_rope",
    )(*args)


def _attn_kernel(*refs, has_extra, tq):
    if has_extra:
        q_ref, k_ref, vt_ref, ke_ref, vte_ref, o_ref, qs_ref, m_ref, acc_ref = refs
    else:
        q_ref, k_ref, vt_ref, o_ref, qs_ref, m_ref, acc_ref = refs
    kv = pl.program_id(2)

    def update(k, vt):
        s = lax.dot_general(k, qs_ref[...], (((1,), (1,)), ((), ())), preferred_element_type=F32)
        m_prev = m_ref[...]
        m_new = jnp.maximum(m_prev, jnp.max(s, axis=0, keepdims=True))
        alpha = jnp.exp(m_prev - m_new)
        p = jnp.exp(s - m_new).astype(BF16)
        vt1 = jnp.concatenate([vt, jnp.ones((ONES_ROWS, vt.shape[1]), BF16)], axis=0)
        acc_ref[...] = alpha * acc_ref[...] + _dot(vt1, p)
        m_ref[...] = m_new

    @pl.when(kv == 0)
    def _():
        for h in range(Q_PER_KV):
            qs_ref[h * tq:(h + 1) * tq, :] = q_ref[:, h * HEAD_DIM:(h + 1) * HEAD_DIM]
        m_ref[...] = jnp.full(m_ref.shape, -jnp.inf, F32)
        acc_ref[...] = jnp.zeros(acc_ref.shape, F32)
        if has_extra:
            update(ke_ref[...], vte_ref[...])

    update(k_ref[...], vt_ref[...])

    @pl.when(kv == pl.num_programs(2) - 1)
    def _():
        o_t = acc_ref[0:HEAD_DIM, :] / acc_ref[HEAD_DIM:HEAD_DIM + 1, :]
        for h in range(Q_PER_KV):
            o_ref[:, h * HEAD_DIM:(h + 1) * HEAD_DIM] = o_t[:, h * tq:(h + 1) * tq].T.astype(o_ref.dtype)


def _attention(q, k, vt, k_extra, vt_extra, tq, tk):
    m = q.shape[0]
    s_len = k.shape[0]
    tq = min(tq, m)
    tk = min(tk, s_len)
    gw = Q_PER_KV * HEAD_DIM
    cols = Q_PER_KV * tq
    has_extra = k_extra is not None
    in_specs = [pl.BlockSpec((tq, gw), lambda g, i, j: (i, g)),
                pl.BlockSpec((tk, HEAD_DIM), lambda g, i, j: (j, g)),
                pl.BlockSpec((HEAD_DIM, tk), lambda g, i, j: (g, j))]
    args = [q, k, vt]
    if has_extra:
        e = k_extra.shape[0]
        in_specs += [pl.BlockSpec((e, HEAD_DIM), lambda g, i, j: (0, g)),
                     pl.BlockSpec((HEAD_DIM, e), lambda g, i, j: (g, 0))]
        args += [k_extra, vt_extra]
    acc_rows = HEAD_DIM + ONES_ROWS
    vm = 4 * _nbytes((tk, cols), F32) + 6 * _nbytes((acc_rows, cols), F32) + (8 << 20)
    return pl.pallas_call(
        functools.partial(_attn_kernel, has_extra=has_extra, tq=tq),
        grid=(N_KV_HEADS, m // tq, s_len // tk),
        in_specs=in_specs,
        out_specs=pl.BlockSpec((tq, gw), lambda g, i, j: (i, g)),
        out_shape=jax.ShapeDtypeStruct((m, N_Q_HEADS * HEAD_DIM), BF16),
        scratch_shapes=[pltpu.VMEM((cols, HEAD_DIM), BF16),
                        pltpu.VMEM((1, cols), F32),
                        pltpu.VMEM((acc_rows, cols), F32)],
        compiler_params=_cparams(("parallel", "parallel", "arbitrary"), vm),
        name="gqa_flash_attention",
    )(*args)


def _conv3(z, w_ref):
    n = z.shape[0]
    row = lax.broadcasted_iota(jnp.int32, z.shape, 0)
    prev = jnp.where(row == 0, 0.0, pltpu.roll(z, 1, 0))
    nxt = jnp.where(row == n - 1, 0.0, pltpu.roll(z, n - 1, 0))
    return prev * w_ref[0:1, :] + z * w_ref[1:2, :] + nxt * w_ref[2:3, :]


def _convmix_kernel(u_ref, b_ref, c_ref, w_ref, o_ref):
    o_ref[...] = (b_ref[...] * _conv3(c_ref[...] * u_ref[...], w_ref)).astype(o_ref.dtype)


def _short_conv_mixer(p, col0, conv_w):
    m = p.shape[0]
    c = conv_w.shape[1]
    ct = LANES
    nb = c // ct
    b0 = col0 // ct
    vm = 8 * _nbytes((m, ct), F32) + 8 * _nbytes((m, ct), F32) + (4 << 20)
    return pl.pallas_call(
        _convmix_kernel,
        grid=(nb,),
        in_specs=[pl.BlockSpec((m, ct), lambda j: (0, b0 + j)),
                  pl.BlockSpec((m, ct), lambda j: (0, b0 + nb + j)),
                  pl.BlockSpec((m, ct), lambda j: (0, b0 + 2 * nb + j)),
                  pl.BlockSpec((3, ct), lambda j: (0, j))],
        out_specs=pl.BlockSpec((m, ct), lambda j: (0, j)),
        out_shape=jax.ShapeDtypeStruct((m, c), BF16),
        compiler_params=_cparams(("parallel",), vm),
        name="short_conv_mixer",
    )(p, p, p, conv_w)


def _hyena_prep_kernel(p0_ref, p1_ref, p2_ref, w0_ref, w1_ref, w2_ref, b0_ref, b1_ref, b2_ref,
                       skip_ref, u_ref, x0_ref, usk_ref):
    x0 = _conv3(p0_ref[...], w0_ref) + b0_ref[...]
    x1 = _conv3(p1_ref[...], w1_ref) + b1_ref[...]
    v = _conv3(p2_ref[...], w2_ref) + b2_ref[...]
    u = x1 * v
    u_ref[...] = u.astype(u_ref.dtype)
    x0_ref[...] = x0.astype(x0_ref.dtype)
    usk_ref[...] = (x0 * (u * skip_ref[...])).astype(usk_ref.dtype)


def _hyena_prep(p, conv_w, conv_b, skip):
    m = p.shape[0]
    c = skip.shape[0]
    ct = LANES
    nb = c // ct
    vm = 12 * _nbytes((m, ct), F32) + 10 * _nbytes((m, ct), F32) + (4 << 20)
    blk = lambda off: pl.BlockSpec((m, ct), functools.partial(lambda j, o: (0, o + j), o=off))
    wblk = lambda off: pl.BlockSpec((3, ct), functools.partial(lambda j, o: (0, o + j), o=off))
    bblk = lambda off: pl.BlockSpec((1, ct), functools.partial(lambda j, o: (0, o + j), o=off))
    out = jax.ShapeDtypeStruct((m, c), BF16)
    return pl.pallas_call(
        _hyena_prep_kernel,
        grid=(nb,),
        in_specs=[blk(0), blk(nb), blk(2 * nb), wblk(0), wblk(nb), wblk(2 * nb),
                  bblk(0), bblk(nb), bblk(2 * nb), pl.BlockSpec((1, ct), lambda j: (0, j))],
        out_specs=[pl.BlockSpec((m, ct), lambda j: (0, j))] * 3,
        out_shape=[out, out, out],
        compiler_params=_cparams(("parallel",), vm),
        name="hyena_short_conv_gate",
    )(p, p, p, conv_w, conv_w, conv_w, conv_b.reshape(1, -1), conv_b.reshape(1, -1),
      conv_b.reshape(1, -1), skip.reshape(1, c))


def _filter_feats_t(n, order):
    t = np.asarray(order, np.float64)
    t_norm = t / max(n - 1, 1)
    bands = np.arange(1, FILTER_BANDS + 1, dtype=np.float64)
    ang = (2 * math.pi / n) * bands[:, None] * t[None, :]
    feats = np.concatenate([t_norm[None, :], np.cos(ang), np.sin(ang)], axis=0)
    out = np.zeros((LANES, len(t)), np.float32)
    out[:feats.shape[0]] = feats
    return out


def _split_bf16(a):
    hi = a.astype(BF16).astype(F32)
    return hi, a - hi


def _dot_split(a, b):
    a_hi, a_lo = _split_bf16(a)
    b_hi, b_lo = _split_bf16(b)
    lhs = jnp.concatenate([a_hi, a_hi, a_lo], axis=1).astype(BF16)
    rhs = jnp.concatenate([b_hi, b_lo, b_hi], axis=0).astype(BF16)
    return _dot(lhs, rhs)


def _filter_kernel(f_ref, w1_ref, b1_ref, fr_ref, w2_ref, b2_ref, w3_ref, dec_ref, o_ref, hs_ref, tn_ref,
                   *, n):
    tr = tn_ref.shape[0]
    hid = w3_ref.shape[0]

    @pl.when(pl.program_id(2) == 0)
    def _():
        fr = fr_ref[...]
        h = jnp.sin(fr * (_dot_split(w1_ref[...], f_ref[...]) + b1_ref[...]))
        h = jnp.concatenate([h, jnp.zeros((LANES - hid, tr), F32)], axis=0)
        h = jnp.sin(fr * (_dot_split(w2_ref[...], h) + b2_ref[...]))
        hi, lo = _split_bf16(h)
        pad = jnp.zeros((2 * LANES - 3 * hid, tr), F32)
        hs_ref[...] = jnp.concatenate([hi, hi, lo, pad], axis=0).T.astype(BF16)
        pos = pl.program_id(1) * tr + lax.broadcasted_iota(jnp.int32, (tr, LANES), 0)
        pos = jnp.where(pl.program_id(0) == 0, pos, n - 1 - pos)
        tn_ref[...] = pos.astype(F32) / float(max(n - 1, 1))

    w_hi, w_lo = _split_bf16(w3_ref[...])
    pad = jnp.zeros((2 * LANES - 3 * hid, w_hi.shape[1]), F32)
    rhs = jnp.concatenate([w_hi, w_lo, w_hi, pad], axis=0).astype(BF16)
    hw = _dot(hs_ref[...], rhs)
    t_norm = jnp.concatenate([tn_ref[...]] * (hw.shape[1] // LANES), axis=1)
    window = jnp.exp(-t_norm * jnp.abs(dec_ref[...])) + FILTER_MOD_SHIFT
    o_ref[...] = (hw * window).astype(o_ref.dtype)


def _hyena_filter(n, ch, f_w1, f_b1, f_freq, f_w2, f_b2, f_w3, f_decay, tr, tc):
    hid = f_w1.shape[1]
    assert 3 * hid <= 2 * LANES and hid <= LANES
    feats = np.stack([_filter_feats_t(n, np.arange(n)), _filter_feats_t(n, np.arange(n)[::-1])])
    w1t = jnp.zeros((hid, LANES), F32).at[:, :f_w1.shape[0]].set(f_w1.T)
    w2t = jnp.zeros((hid, LANES), F32).at[:, :hid].set(f_w2.T)
    tr = min(tr, n)
    nrb = n // tr
    ncb = ch // tc
    vm = 6 * _nbytes((tr, tc), F32) + 4 * _nbytes((LANES, tr), F32) + (8 << 20)
    col = lambda a: a.reshape(hid, 1)
    small = lambda shape: pl.BlockSpec(shape, lambda s, i, j: (0, 0))
    return pl.pallas_call(
        functools.partial(_filter_kernel, n=n),
        grid=(2, nrb, ncb),
        in_specs=[pl.BlockSpec((None, LANES, tr), lambda s, i, j: (s, 0, i)),
                  small((hid, LANES)), small((hid, 1)), small((hid, 1)),
                  small((hid, LANES)), small((hid, 1)),
                  pl.BlockSpec((hid, tc), lambda s, i, j: (0, s * ncb + j)),
                  pl.BlockSpec((1, tc), lambda s, i, j: (0, s * ncb + j))],
        out_specs=pl.BlockSpec((tr, tc), lambda s, i, j: (s * nrb + i, j)),
        out_shape=jax.ShapeDtypeStruct((2 * n, ch), BF16),
        scratch_shapes=[pltpu.VMEM((tr, 2 * LANES), BF16), pltpu.VMEM((tr, LANES), F32)],
        compiler_params=_cparams(("parallel", "parallel", "arbitrary"), vm),
        name="hyena_filter",
    )(jnp.asarray(feats), w1t, col(f_b1), col(f_freq), w2t, col(f_b2), f_w3, f_decay.reshape(1, -1))


def _dft_geometry(n):
    big_n = 2 * n
    n2 = big_n // DFT_N1
    nh = -(-(n2 // 2 + 1) // SUBLANES) * SUBLANES
    return big_n, n2, nh


@functools.lru_cache(maxsize=None)
def _stage1_matrix(n, rows_n2):
    big_n, _, nh = _dft_geometry(n)
    groups = DFT_N1 // SUBLANES
    out = np.zeros((groups, 2, nh, SUBLANES, rows_n2, SUBLANES), np.float32)
    k2 = np.arange(nh, dtype=np.float64)[:, None]
    n2v = np.arange(rows_n2, dtype=np.float64)[None, :]
    for g in range(groups):
        for j in range(SUBLANES):
            t = SUBLANES * g + j + DFT_N1 * n2v
            ang = 2 * math.pi * np.mod(k2 * t, big_n) / big_n
            out[g, 0, :, j, :, j] = np.cos(ang)
            out[g, 1, :, j, :, j] = -np.sin(ang)
    return out.reshape(groups, 2 * nh * SUBLANES, rows_n2 * SUBLANES).astype(BF16)


@functools.lru_cache(maxsize=None)
def _stage2_matrices():
    idx = np.arange(DFT_N1, dtype=np.float64)
    ang = 2 * math.pi * np.mod(np.outer(idx, idx), DFT_N1) / DFT_N1
    c, s = np.cos(ang), np.sin(ang)
    fwd = np.block([[c, s], [-s, c]])
    inv = np.block([[c, -s], [s, c]])
    return fwd.astype(BF16), inv.astype(BF16)


@functools.lru_cache(maxsize=None)
def _stage4_matrix(n):
    big_n, n2c, nh = _dft_geometry(n)
    groups = DFT_N1 // SUBLANES
    rows_n2 = n2c // 2
    out = np.zeros((groups, rows_n2, SUBLANES, 2, nh, SUBLANES), np.float32)
    k2 = np.arange(nh, dtype=np.float64)[None, :]
    wgt = np.where(k2 <= n2c // 2, 2.0, 0.0)
    wgt[0, 0] = 1.0
    wgt[0, n2c // 2] = 1.0
    n2v = np.arange(rows_n2, dtype=np.float64)[:, None]
    for g in range(groups):
        for j in range(SUBLANES):
            t = SUBLANES * g + j + DFT_N1 * n2v
            ang = 2 * math.pi * np.mod(k2 * t, big_n) / big_n
            out[g, :, j, 0, :, j] = wgt * np.cos(ang) / big_n
            out[g, :, j, 1, :, j] = -wgt * np.sin(ang) / big_n
    return out.reshape(groups, rows_n2 * SUBLANES, 2 * nh * SUBLANES).astype(BF16)


def _dft_stage1_kernel(x_ref, l_ref, o_ref, *, gsteps):
    nb, _, ct = x_ref.shape
    rows = o_ref.shape[0]
    xin = x_ref[...].astype(F32)
    outs = []
    for gi in range(gsteps):
        sl = slice(SUBLANES * gi, SUBLANES * (gi + 1))
        xv = xin[:, sl, :].reshape(nb * SUBLANES, ct).astype(BF16)
        outs.append(_dot(l_ref[gi], xv).reshape(rows, SUBLANES, ct))
    o_ref[...] = jnp.concatenate(outs, axis=1).astype(o_ref.dtype)


def _dft_stage1(x3, n, gsteps, ct):
    rows_n2, _, c = x3.shape
    _, _, nh = _dft_geometry(n)
    mat = _stage1_matrix(n, rows_n2)
    groups = mat.shape[0]
    ct = min(ct, c)
    gw = SUBLANES * gsteps
    vm = (2 * _nbytes((rows_n2, gw, ct), F32) + 2 * _nbytes((gsteps,) + mat.shape[1:], BF16)
          + 2 * _nbytes((2 * nh, gw, ct), F32) + 3 * _nbytes((2 * nh * SUBLANES, ct), F32) + (4 << 20))
    return pl.pallas_call(
        functools.partial(_dft_stage1_kernel, gsteps=gsteps),
        grid=(groups // gsteps, c // ct),
        in_specs=[pl.BlockSpec((rows_n2, gw, ct), lambda g, j: (0, g, j)),
                  pl.BlockSpec((gsteps,) + mat.shape[1:], lambda g, j: (g, 0, 0))],
        out_specs=pl.BlockSpec((2 * nh, gw, ct), lambda g, j: (0, g, j)),
        out_shape=jax.ShapeDtypeStruct((2 * nh, DFT_N1, c), BF16),
        compiler_params=_cparams(("parallel", "parallel"), vm),
        name="dft_stage_n2",
    )(x3, mat)


def _stack_parts(a_ref, kb):
    cols = [jnp.concatenate([a_ref[0, k], a_ref[1, k]], axis=0) for k in range(kb)]
    return jnp.concatenate(cols, axis=1).astype(BF16)


def _store_parts(o_ref, x, kb):
    ct = o_ref.shape[-1]
    for k in range(kb):
        o_ref[0, k] = x[:DFT_N1, k * ct:(k + 1) * ct].astype(o_ref.dtype)
        o_ref[1, k] = x[DFT_N1:, k * ct:(k + 1) * ct].astype(o_ref.dtype)


def _filter_spectrum_kernel(a_ref, l2_ref, o_ref, *, kb):
    _store_parts(o_ref, _dot(l2_ref[...], _stack_parts(a_ref, kb)), kb)


def _spectral_product_kernel(a_ref, kf_ref, l2_ref, l3_ref, o_ref, *, kb):
    ct = a_ref.shape[-1]
    x = _dot(l2_ref[...], _stack_parts(a_ref, kb))
    ys = []
    for k in range(kb):
        xr = x[:DFT_N1, k * ct:(k + 1) * ct]
        xi = x[DFT_N1:, k * ct:(k + 1) * ct]
        kr, ki = kf_ref[0, k].astype(F32), kf_ref[1, k].astype(F32)
        ys.append(jnp.concatenate([xr * kr - xi * ki, xr * ki + xi * kr], axis=0))
    _store_parts(o_ref, _dot(l3_ref[...], jnp.concatenate(ys, axis=1).astype(BF16)), kb)


def _dft_stage2(a4, kf4, kb, ct):
    _, nh, _, c = a4.shape
    fwd, inv = _stage2_matrices()
    ct = min(ct, c)
    blk = pl.BlockSpec((2, kb, DFT_N1, ct), lambda g, j: (0, g, 0, j))
    mblk = pl.BlockSpec((2 * DFT_N1, 2 * DFT_N1), lambda g, j: (0, 0))
    nblk = 2 if kf4 is None else 3
    vm = 2 * nblk * _nbytes((2, kb, DFT_N1, ct), F32) + 8 * _nbytes((2 * DFT_N1, kb * ct), F32) + (4 << 20)
    if kf4 is None:
        kern, in_specs, args = _filter_spectrum_kernel, [blk, mblk], (a4, fwd)
        name = "dft_stage_n1_filter"
    else:
        kern, in_specs, args = _spectral_product_kernel, [blk, blk, mblk, mblk], (a4, kf4, fwd, inv)
        name = "dft_stage_n1_product"
    return pl.pallas_call(
        functools.partial(kern, kb=kb),
        grid=(nh // kb, c // ct),
        in_specs=in_specs,
        out_specs=blk,
        out_shape=jax.ShapeDtypeStruct(a4.shape, BF16),
        compiler_params=_cparams(("parallel", "parallel"), vm),
        name=name,
    )(*args)


def _dft_stage4_kernel(v_ref, l_ref, x0_ref, usk_ref, o_ref, *, gsteps):
    rows, _, ct = v_ref.shape
    nb = o_ref.shape[0]
    vin = v_ref[...].astype(F32)
    ys = []
    for gi in range(gsteps):
        sl = slice(SUBLANES * gi, SUBLANES * (gi + 1))
        vv = vin[:, sl, :].reshape(rows * SUBLANES, ct).astype(BF16)
        ys.append(_dot(l_ref[gi], vv).reshape(nb, SUBLANES, ct))
    y = jnp.concatenate(ys, axis=1)
    o_ref[...] = (x0_ref[...].astype(F32) * y + usk_ref[...].astype(F32)).astype(o_ref.dtype)


def _dft_stage4(v3, x03, usk3, n, gsteps, ct):
    rows, _, c = v3.shape
    mat = _stage4_matrix(n)
    groups = mat.shape[0]
    nb = x03.shape[0]
    ct = min(ct, c)
    gw = SUBLANES * gsteps
    vm = (2 * _nbytes((rows, gw, ct), F32) + 2 * _nbytes((gsteps,) + mat.shape[1:], BF16)
          + 6 * _nbytes((nb, gw, ct), F32) + 3 * _nbytes((rows * SUBLANES, ct), F32) + (4 << 20))
    oblk = pl.BlockSpec((nb, gw, ct), lambda g, j: (0, g, j))
    return pl.pallas_call(
        functools.partial(_dft_stage4_kernel, gsteps=gsteps),
        grid=(groups // gsteps, c // ct),
        in_specs=[pl.BlockSpec((rows, gw, ct), lambda g, j: (0, g, j)),
                  pl.BlockSpec((gsteps,) + mat.shape[1:], lambda g, j: (g, 0, 0)),
                  oblk, oblk],
        out_specs=oblk,
        out_shape=jax.ShapeDtypeStruct((nb, DFT_N1, c), BF16),
        compiler_params=_cparams(("parallel", "parallel"), vm),
        name="dft_stage_k2_inverse",
    )(v3, mat, x03, usk3)


def _long_conv_gate(u, x0, usk, filt2, n):
    c = u.shape[1]
    _, _, nh = _dft_geometry(n)
    as3 = lambda a: a.reshape(a.shape[0] // DFT_N1, DFT_N1, c)
    kf = _dft_stage1(as3(filt2), n, gsteps=2, ct=512)
    kf = _dft_stage2(kf.reshape(2, nh, DFT_N1, c), None, kb=SUBLANES, ct=512)
    a = _dft_stage1(as3(u), n, gsteps=2, ct=512)
    v = _dft_stage2(a.reshape(2, nh, DFT_N1, c), kf, kb=SUBLANES, ct=512)
    y = _dft_stage4(v.reshape(2 * nh, DFT_N1, c), as3(x0), as3(usk), n, gsteps=2, ct=512)
    return y.reshape(n, c)


@functools.lru_cache(maxsize=None)
def _dense_dft_matrices(n):
    big_n = 2 * n
    nf = n + 1
    nfp = -(-nf // SUBLANES) * SUBLANES
    k = np.arange(nf, dtype=np.float64)[:, None]
    t = np.arange(big_n, dtype=np.float64)[None, :]
    ang = 2 * math.pi * np.mod(k * t, big_n) / big_n
    fwd = np.zeros((2, nfp, big_n), np.float32)
    fwd[0, :nf] = np.cos(ang)
    fwd[1, :nf] = -np.sin(ang)
    wgt = np.full((nf, 1), 2.0)
    wgt[0] = 1.0
    wgt[n] = 1.0
    inv = np.zeros((2, nfp, n), np.float32)
    inv[0, :nf] = (wgt * np.cos(ang) / big_n)[:, :n]
    inv[1, :nf] = (-wgt * np.sin(ang) / big_n)[:, :n]
    fwd = fwd.reshape(2 * nfp, big_n)
    inv = inv.reshape(2 * nfp, n).T
    return (np.ascontiguousarray(fwd[:, :n]).astype(BF16), fwd.astype(BF16),
            np.ascontiguousarray(inv).astype(BF16))


def _short_conv_gate_kernel(u_ref, f_ref, x0_ref, usk_ref, lu_ref, lf_ref, li_ref, o_ref):
    half = lu_ref.shape[0] // 2
    xs = _dot(lu_ref[...], u_ref[...].astype(BF16))
    ks = _dot(lf_ref[...], f_ref[...].astype(BF16))
    xr, xi, kr, ki = xs[:half], xs[half:], ks[:half], ks[half:]
    y = jnp.concatenate([xr * kr - xi * ki, xr * ki + xi * kr], axis=0).astype(BF16)
    conv = _dot(li_ref[...], y)
    o_ref[...] = (x0_ref[...].astype(F32) * conv + usk_ref[...].astype(F32)).astype(o_ref.dtype)


def _long_conv_gate_short(u, x0, usk, filt2, n):
    c = u.shape[1]
    lu, lf, li = _dense_dft_matrices(n)
    ct = 256
    blk = pl.BlockSpec((n, ct), lambda j: (0, j))
    full = lambda a: pl.BlockSpec(a.shape, lambda j: (0, 0))
    vm = 32 << 20
    return pl.pallas_call(
        _short_conv_gate_kernel,
        grid=(c // ct,),
        in_specs=[blk, pl.BlockSpec((2 * n, ct), lambda j: (0, j)), blk, blk, full(lu), full(lf), full(li)],
        out_specs=blk,
        out_shape=jax.ShapeDtypeStruct((n, c), BF16),
        compiler_params=_cparams(("parallel",), vm),
        name="context_long_conv",
    )(u, filt2, x0, usk, lu, lf, li)


def _rope_tables(n):
    t = jnp.arange(n, dtype=jnp.int32)
    row = (t // GRID_W).astype(F32)
    col = (t % GRID_W).astype(F32)
    axis_dim = HEAD_DIM // 2
    inv = 1.0 / (ROPE_THETA ** (jnp.arange(0, axis_dim, 2, dtype=F32) / axis_dim))
    ar, ac = row[:, None] * inv, col[:, None] * inv
    cos = jnp.concatenate([jnp.cos(ar), jnp.cos(ar), jnp.cos(ac), jnp.cos(ac)], axis=-1)
    sin = jnp.concatenate([-jnp.sin(ar), jnp.sin(ar), -jnp.sin(ac), jnp.sin(ac)], axis=-1)
    return cos, sin


TM_PROJ = 512
TF_FFN = 1024
TQ_ATTN = 256
TK_ATTN = 1024


def kernel(x, c, ctx, c_ctx, mod_w, mod_b, norm_mix_g, norm_ffn_g, ffn_w1, ffn_w2, ev_w_in, ev_q_norm, ev_k_norm, ev_conv_w, ev_w_out, od_w_in, od_conv_w, od_conv_b, od_f_w1, od_f_b1, od_f_freq, od_f_w2, od_f_b2, od_f_w3, od_f_decay, od_skip, od_w_out, final_g):
    batch, n, d = x.shape
    assert batch == 1
    n_ctx = ctx.shape[1]
    depth = mod_w.shape[0]
    last_ctx_read = ((depth - 1) // 2) * 2
    attn_w = N_Q_HEADS * HEAD_DIM
    kv_w = N_KV_HEADS * HEAD_DIM

    xs = x[0]
    xc = ctx[0]
    cond = jnp.zeros((SUBLANES, d), F32).at[0].set(c[0]).at[1].set(c_ctx)
    modv = _mod_vectors(cond, mod_w, mod_b)
    rope = _rope_tables(n)

    w1_all, w2_all = ffn_w1.astype(BF16), ffn_w2.astype(BF16)
    ev_in, ev_out = ev_w_in.astype(BF16), ev_w_out.astype(BF16)
    od_in, od_out = od_w_in.astype(BF16), od_w_out.astype(BF16)
    g_mix = norm_mix_g.reshape(depth, 1, d)
    g_ffn = norm_ffn_g.reshape(depth, 1, d)
    kv_block = attn_w // (2 * kv_w)
    conv_col = attn_w + 2 * kv_w

    for l in range(depth):
        ctx_full = l < last_ctx_read
        if l % 2 == 0:
            e = l // 2
            p_l = _modmm(xs, g_mix, modv, l, 0, 0, ev_in, e, TM_PROJ, 1536)
            q_l, k_l, v_l = _qkv_prep(p_l, 0, kv_block, ev_q_norm[e], ev_k_norm[e], rope, TM_PROJ)
            p_c = _modmm(xc, g_mix, modv, l, 1, 0, ev_in, e, TM_PROJ, 1536)
            q_c, k_c, v_c = _qkv_prep(p_c, 0, kv_block, ev_q_norm[e], ev_k_norm[e], None, TM_PROJ)
            att_l = _attention(q_l, k_l, v_l, k_c, v_c, TQ_ATTN, TK_ATTN)
            conv_l = _short_conv_mixer(p_l, conv_col, ev_conv_w[e])
            if ctx_full:
                att_c = _attention(q_c, k_c, v_c, None, None, TQ_ATTN, TK_ATTN)
                conv_c = _short_conv_mixer(p_c, conv_col, ev_conv_w[e])
                xc = _outproj([att_c, conv_c], ev_out, e, xc, modv, l, 1, 2, TM_PROJ)
            xs = _outproj([att_l, conv_l], ev_out, e, xs, modv, l, 0, 2, TM_PROJ)
        else:
            o = l // 2
            fargs = (od_f_w1[o], od_f_b1[o], od_f_freq[o], od_f_w2[o], od_f_b2[o], od_f_w3[o], od_f_decay[o])
            p_l = _modmm(xs, g_mix, modv, l, 0, 0, od_in, o, TM_PROJ, 1536)
            u, x0, usk = _hyena_prep(p_l, od_conv_w[o], od_conv_b[o], od_skip[o])
            filt2 = _hyena_filter(n, d, *fargs, tr=1024, tc=1024)
            y = _long_conv_gate(u, x0, usk, filt2, n)
            xs = _outproj([y], od_out, o, xs, modv, l, 0, 2, TM_PROJ)
            if ctx_full:
                p_c = _modmm(xc, g_mix, modv, l, 1, 0, od_in, o, TM_PROJ, 1536)
                u, x0, usk = _hyena_prep(p_c, od_conv_w[o], od_conv_b[o], od_skip[o])
                filt2 = _hyena_filter(n_ctx, d, *fargs, tr=1024, tc=1024)
                y = _long_conv_gate_short(u, x0, usk, filt2, n_ctx)
                xc = _outproj([y], od_out, o, xc, modv, l, 1, 2, TM_PROJ)
        last = l == depth - 1
        xs = _ffn(xs, g_ffn, modv, l, 0, w1_all, w2_all, TM_PROJ, TF_FFN, final_gain=final_g if last else None)
        if ctx_full:
            xc = _ffn(xc, g_ffn, modv, l, 1, w1_all, w2_all, TM_PROJ, TF_FFN)

    return xs[None]
```

```python
import functools
import math

import numpy as np
import jax
import jax.numpy as jnp
from jax import lax
from jax.experimental import pallas as pl
from jax.experimental.pallas import tpu as pltpu

F32 = jnp.float32
BF16 = jnp.bfloat16

HEAD_DIM = 128
N_Q_HEADS = 8
N_KV_HEADS = 2
Q_PER_KV = N_Q_HEADS // N_KV_HEADS
GRID_W = 64
ROPE_THETA = 10000.0
FILTER_BANDS = 16
FILTER_MOD_SHIFT = 0.05
EPS = 1e-6

LANES = 128
SUBLANES = 8
VMEM_BUDGET = 56 * 1024 * 1024
DFT_N1 = 128
ONES_ROWS = 16


def _cparams(sem, vmem_bytes):
    return pltpu.CompilerParams(dimension_semantics=sem,
                                vmem_limit_bytes=int(min(max(vmem_bytes, 16 << 20), VMEM_BUDGET)))


def _nbytes(shape, dtype):
    return int(np.prod(shape)) * jnp.dtype(dtype).itemsize


def _dot(a, b):
    return jnp.dot(a, b, preferred_element_type=F32)


def _rms_rows(x):
    return x * lax.rsqrt(jnp.mean(x * x, axis=-1, keepdims=True) + EPS)


def _mod_kernel(c_ref, w_ref, b_ref, o_ref):
    c = c_ref[...]
    s = c * (1.0 / (1.0 + jnp.exp(-c)))
    o_ref[0] = _dot(s.astype(BF16), w_ref[0].astype(BF16)) + b_ref[0]


def _mod_vectors(cond, mod_w, mod_b):
    depth, d, n6 = mod_w.shape
    tn = 1536
    rows = cond.shape[0]
    vm = 2 * _nbytes((d, tn), F32) + _nbytes((d, tn), BF16) + (4 << 20)
    return pl.pallas_call(
        _mod_kernel,
        grid=(depth, n6 // tn),
        in_specs=[pl.BlockSpec((rows, d), lambda l, j: (0, 0)),
                  pl.BlockSpec((1, d, tn), lambda l, j: (l, 0, j)),
                  pl.BlockSpec((1, 1, tn), lambda l, j: (l, 0, j))],
        out_specs=pl.BlockSpec((1, rows, tn), lambda l, j: (l, 0, j)),
        out_shape=jax.ShapeDtypeStruct((depth, rows, n6), F32),
        compiler_params=_cparams(("parallel", "parallel"), vm),
        name="adaln_vectors",
    )(cond, mod_w, mod_b.reshape(depth, 1, n6))


def _modmm_kernel(x_ref, g_ref, sh_ref, sc_ref, w_ref, o_ref, h_ref, *, row):
    @pl.when(pl.program_id(1) == 0)
    def _():
        h = _rms_rows(x_ref[...]) * g_ref[...]
        h = h * (1.0 + sc_ref[row:row + 1, :]) + sh_ref[row:row + 1, :]
        h_ref[...] = h.astype(BF16)

    o_ref[...] = _dot(h_ref[...], w_ref[...]).astype(o_ref.dtype)


def _modmm(x, gains, modv, l, row, col0, w, wi, tm, tn, out_dtype=F32):
    m, d = x.shape
    n = w.shape[2]
    tm = min(tm, m)
    vm = (2 * _nbytes((tm, d), F32) + _nbytes((tm, d), BF16) + 2 * _nbytes((d, tn), BF16)
          + 2 * _nbytes((tm, tn), F32) + 2 * _nbytes((tm, d), F32) + (4 << 20))
    return pl.pallas_call(
        functools.partial(_modmm_kernel, row=row),
        grid=(m // tm, n // tn),
        in_specs=[pl.BlockSpec((tm, d), lambda i, j: (i, 0)),
                  pl.BlockSpec((None, 1, d), lambda i, j: (l, 0, 0)),
                  pl.BlockSpec((None, SUBLANES, d), lambda i, j: (l, 0, col0)),
                  pl.BlockSpec((None, SUBLANES, d), lambda i, j: (l, 0, col0 + 1)),
                  pl.BlockSpec((None, d, tn), lambda i, j: (wi, 0, j))],
        out_specs=pl.BlockSpec((tm, tn), lambda i, j: (i, j)),
        out_shape=jax.ShapeDtypeStruct((m, n), out_dtype),
        scratch_shapes=[pltpu.VMEM((tm, d), BF16)],
        compiler_params=_cparams(("parallel", "arbitrary"), vm),
        name="modulated_projection",
    )(x, gains, modv, modv, w)


def _outproj_kernel(*refs, n_in, row):
    a_refs = refs[:n_in]
    w_refs = refs[n_in:2 * n_in]
    x_ref, g_ref, o_ref = refs[2 * n_in:]
    acc = _dot(a_refs[0][...].astype(BF16), w_refs[0][...])
    for a_ref, w_ref in zip(a_refs[1:], w_refs[1:]):
        acc = acc + _dot(a_ref[...].astype(BF16), w_ref[...])
    o_ref[...] = x_ref[...] + g_ref[row:row + 1, :] * acc


def _outproj(acts, w, wi, x, modv, l, row, gate_col, tm):
    m, d = x.shape
    tm = min(tm, m)
    n_in = len(acts)
    in_specs, args = [], []
    for a in acts:
        in_specs.append(pl.BlockSpec((tm, a.shape[1]), lambda i: (i, 0)))
        args.append(a)
    row0 = 0
    for a in acts:
        k = a.shape[1]
        in_specs.append(pl.BlockSpec((None, k, d), functools.partial(lambda i, b: (wi, b, 0), b=row0 // k)))
        args.append(w)
        row0 += k
    in_specs += [pl.BlockSpec((tm, d), lambda i: (i, 0)),
                 pl.BlockSpec((None, SUBLANES, d), lambda i: (l, 0, gate_col))]
    args += [x, modv]
    vm = (2 * sum(_nbytes((tm, a.shape[1]), a.dtype) for a in acts) + 2 * _nbytes(w.shape[1:], BF16)
          + 6 * _nbytes((tm, d), F32) + (4 << 20))
    return pl.pallas_call(
        functools.partial(_outproj_kernel, n_in=n_in, row=row),
        grid=(m // tm,),
        in_specs=in_specs,
        out_specs=pl.BlockSpec((tm, d), lambda i: (i, 0)),
        out_shape=jax.ShapeDtypeStruct((m, d), F32),
        compiler_params=_cparams(("parallel",), vm),
        name="gated_out_projection",
    )(*args)


def _ffn_kernel(*refs, row, final_norm):
    if final_norm:
        x_ref, g_ref, sh_ref, sc_ref, gate_ref, w1_ref, w2_ref, fg_ref, o_ref, h_ref, acc_ref = refs
    else:
        x_ref, g_ref, sh_ref, sc_ref, gate_ref, w1_ref, w2_ref, o_ref, h_ref, acc_ref = refs
    j = pl.program_id(1)

    @pl.when(j == 0)
    def _():
        h = _rms_rows(x_ref[...]) * g_ref[...]
        h = h * (1.0 + sc_ref[row:row + 1, :]) + sh_ref[row:row + 1, :]
        h_ref[...] = h.astype(BF16)
        acc_ref[...] = jnp.zeros(acc_ref.shape, F32)

    a = jnp.maximum(_dot(h_ref[...], w1_ref[...]), 0.0)
    acc_ref[...] += _dot((a * a).astype(BF16), w2_ref[...])

    @pl.when(j == pl.num_programs(1) - 1)
    def _():
        y = x_ref[...] + gate_ref[row:row + 1, :] * acc_ref[...]
        if final_norm:
            y = _rms_rows(y) * fg_ref[...]
        o_ref[...] = y


def _ffn(x, gains, modv, l, row, w1, w2, tm, tf, final_gain=None):
    m, d = x.shape
    f = w1.shape[2]
    tm = min(tm, m)
    vm = (4 * _nbytes((tm, d), F32) + _nbytes((tm, d), BF16) + _nbytes((tm, d), F32)
          + 4 * _nbytes((d, tf), BF16) + 3 * _nbytes((tm, tf), F32) + (4 << 20))
    in_specs = [pl.BlockSpec((tm, d), lambda i, j: (i, 0)),
                pl.BlockSpec((None, 1, d), lambda i, j: (l, 0, 0)),
                pl.BlockSpec((None, SUBLANES, d), lambda i, j: (l, 0, 3)),
                pl.BlockSpec((None, SUBLANES, d), lambda i, j: (l, 0, 4)),
                pl.BlockSpec((None, SUBLANES, d), lambda i, j: (l, 0, 5)),
                pl.BlockSpec((None, d, tf), lambda i, j: (l, 0, j)),
                pl.BlockSpec((None, tf, d), lambda i, j: (l, j, 0))]
    args = [x, gains, modv, modv, modv, w1, w2]
    if final_gain is not None:
        in_specs.append(pl.BlockSpec((1, d), lambda i, j: (0, 0)))
        args.append(final_gain.reshape(1, d))
    return pl.pallas_call(
        functools.partial(_ffn_kernel, row=row, final_norm=final_gain is not None),
        grid=(m // tm, f // tf),
        in_specs=in_specs,
        out_specs=pl.BlockSpec((tm, d), lambda i, j: (i, 0)),
        out_shape=jax.ShapeDtypeStruct((m, d), F32),
        scratch_shapes=[pltpu.VMEM((tm, d), BF16), pltpu.VMEM((tm, d), F32)],
        compiler_params=_cparams(("parallel", "arbitrary"), vm),
        name="gated_ffn",
    )(*args)


def _rope(x, cos, sin):
    lane = lax.broadcasted_iota(jnp.int32, x.shape, 1)
    first_half = (lane % (HEAD_DIM // 2)) < (HEAD_DIM // 4)
    partner = jnp.where(first_half,
                        pltpu.roll(x, HEAD_DIM - HEAD_DIM // 4, 1),
                        pltpu.roll(x, HEAD_DIM // 4, 1))
    return x * cos + partner * sin


def _qkv_kernel(*refs, use_rope, q_scale):
    if use_rope:
        pq_ref, pkv_ref, qg_ref, kg_ref, cos_ref, sin_ref, q_ref, k_ref, v_ref = refs
        cos, sin = cos_ref[...], sin_ref[...]
    else:
        pq_ref, pkv_ref, qg_ref, kg_ref, q_ref, k_ref, v_ref = refs
    kv_w = N_KV_HEADS * HEAD_DIM
    for h in range(N_Q_HEADS):
        sl = slice(h * HEAD_DIM, (h + 1) * HEAD_DIM)
        xn = _rms_rows(pq_ref[:, sl]) * qg_ref[...]
        if use_rope:
            xn = _rope(xn, cos, sin)
        q_ref[:, sl] = (xn * q_scale).astype(BF16)
    for h in range(N_KV_HEADS):
        sl = slice(h * HEAD_DIM, (h + 1) * HEAD_DIM)
        xn = _rms_rows(pkv_ref[:, sl]) * kg_ref[...]
        if use_rope:
            xn = _rope(xn, cos, sin)
        k_ref[:, sl] = xn.astype(BF16)
    v_ref[...] = pkv_ref[:, kv_w:2 * kv_w].T.astype(BF16)


def _qkv_prep(p, q_col_block, kv_col_block, q_gain, k_gain, rope, tm):
    m = p.shape[0]
    tm = min(tm, m)
    attn_w = N_Q_HEADS * HEAD_DIM
    kv_w = N_KV_HEADS * HEAD_DIM
    in_specs = [pl.BlockSpec((tm, attn_w), lambda i: (i, q_col_block)),
                pl.BlockSpec((tm, 2 * kv_w), lambda i: (i, kv_col_block)),
                pl.BlockSpec((1, HEAD_DIM), lambda i: (0, 0)),
                pl.BlockSpec((1, HEAD_DIM), lambda i: (0, 0))]
    args = [p, p, q_gain.reshape(1, HEAD_DIM), k_gain.reshape(1, HEAD_DIM)]
    if rope is not None:
        in_specs += [pl.BlockSpec((tm, HEAD_DIM), lambda i: (i, 0))] * 2
        args += list(rope)
    vm = 4 * _nbytes((tm, attn_w + 2 * kv_w), F32) + (8 << 20)
    return pl.pallas_call(
        functools.partial(_qkv_kernel, use_rope=rope is not None, q_scale=HEAD_DIM ** -0.5),
        grid=(m // tm,),
        in_specs=in_specs,
        out_specs=[pl.BlockSpec((tm, attn_w), lambda i: (i, 0)),
                   pl.BlockSpec((tm, kv_w), lambda i: (i, 0)),
                   pl.BlockSpec((kv_w, tm), lambda i: (0, i))],
        out_shape=[jax.ShapeDtypeStruct((m, attn_w), BF16),
                   jax.ShapeDtypeStruct((m, kv_w), BF16),
                   jax.ShapeDtypeStruct((kv_w, m), BF16)],
        compiler_params=_cparams(("parallel",), vm),
        name="qkv_norm_rope",
    )(*args)


def _attn_kernel(*refs, has_extra, tq):
    if has_extra:
        q_ref, k_ref, vt_ref, ke_ref, vte_ref, o_ref, qs_ref, m_ref, acc_ref = refs
    else:
        q_ref, k_ref, vt_ref, o_ref, qs_ref, m_ref, acc_ref = refs
    kv = pl.program_id(2)

    def update(k, vt):
        vt1 = jnp.concatenate([vt, jnp.ones((ONES_ROWS, vt.shape[1]), BF16)], axis=0)
        cols = [slice(h * tq, (h + 1) * tq) for h in range(Q_PER_KV)]
        s, p, alpha = {}, {}, {}
        for t in range(Q_PER_KV + 2):
            if t < Q_PER_KV:
                s[t] = lax.dot_general(k, qs_ref[cols[t], :], (((1,), (1,)), ((), ())),
                                       preferred_element_type=F32)
            h = t - 1
            if 0 <= h < Q_PER_KV:
                m_prev = m_ref[:, cols[h]]
                m_new = jnp.maximum(m_prev, jnp.max(s[h], axis=0, keepdims=True))
                alpha[h] = jnp.exp(m_prev - m_new)
                p[h] = jnp.exp(s.pop(h) - m_new).astype(BF16)
                m_ref[:, cols[h]] = m_new
            h = t - 2
            if 0 <= h < Q_PER_KV:
                acc_ref[:, cols[h]] = alpha.pop(h) * acc_ref[:, cols[h]] + _dot(vt1, p.pop(h))

    @pl.when(kv == 0)
    def _():
        for h in range(Q_PER_KV):
            qs_ref[h * tq:(h + 1) * tq, :] = q_ref[:, h * HEAD_DIM:(h + 1) * HEAD_DIM]
        m_ref[...] = jnp.full(m_ref.shape, -jnp.inf, F32)
        acc_ref[...] = jnp.zeros(acc_ref.shape, F32)
        if has_extra:
            update(ke_ref[...], vte_ref[...])

    update(k_ref[...], vt_ref[...])

    @pl.when(kv == pl.num_programs(2) - 1)
    def _():
        o_t = acc_ref[0:HEAD_DIM, :] / acc_ref[HEAD_DIM:HEAD_DIM + 1, :]
        for h in range(Q_PER_KV):
            o_ref[:, h * HEAD_DIM:(h + 1) * HEAD_DIM] = o_t[:, h * tq:(h + 1) * tq].T.astype(o_ref.dtype)


def _attention(q, k, vt, k_extra, vt_extra, tq, tk):
    m = q.shape[0]
    s_len = k.shape[0]
    tq = min(tq, m)
    tk = min(tk, s_len)
    gw = Q_PER_KV * HEAD_DIM
    cols = Q_PER_KV * tq
    has_extra = k_extra is not None
    in_specs = [pl.BlockSpec((tq, gw), lambda g, i, j: (i, g)),
                pl.BlockSpec((tk, HEAD_DIM), lambda g, i, j: (j, g)),
                pl.BlockSpec((HEAD_DIM, tk), lambda g, i, j: (g, j))]
    args = [q, k, vt]
    if has_extra:
        e = k_extra.shape[0]
        in_specs += [pl.BlockSpec((e, HEAD_DIM), lambda g, i, j: (0, g)),
                     pl.BlockSpec((HEAD_DIM, e), lambda g, i, j: (g, 0))]
        args += [k_extra, vt_extra]
    acc_rows = HEAD_DIM + ONES_ROWS
    vm = 4 * _nbytes((tk, cols), F32) + 6 * _nbytes((acc_rows, cols), F32) + (8 << 20)
    return pl.pallas_call(
        functools.partial(_attn_kernel, has_extra=has_extra, tq=tq),
        grid=(N_KV_HEADS, m // tq, s_len // tk),
        in_specs=in_specs,
        out_specs=pl.BlockSpec((tq, gw), lambda g, i, j: (i, g)),
        out_shape=jax.ShapeDtypeStruct((m, N_Q_HEADS * HEAD_DIM), BF16),
        scratch_shapes=[pltpu.VMEM((cols, HEAD_DIM), BF16),
                        pltpu.VMEM((1, cols), F32),
                        pltpu.VMEM((acc_rows, cols), F32)],
        compiler_params=_cparams(("parallel", "parallel", "arbitrary"), vm),
        name="gqa_flash_attention",
    )(*args)


def _conv3(z, w_ref):
    n = z.shape[0]
    row = lax.broadcasted_iota(jnp.int32, z.shape, 0)
    prev = jnp.where(row == 0, 0.0, pltpu.roll(z, 1, 0))
    nxt = jnp.where(row == n - 1, 0.0, pltpu.roll(z, n - 1, 0))
    return prev * w_ref[0:1, :] + z * w_ref[1:2, :] + nxt * w_ref[2:3, :]


def _convmix_kernel(u_ref, b_ref, c_ref, w_ref, o_ref):
    o_ref[...] = (b_ref[...] * _conv3(c_ref[...] * u_ref[...], w_ref)).astype(o_ref.dtype)


def _short_conv_mixer(p, col0, conv_w):
    m = p.shape[0]
    c = conv_w.shape[1]
    ct = LANES
    nb = c // ct
    b0 = col0 // ct
    vm = 8 * _nbytes((m, ct), F32) + 8 * _nbytes((m, ct), F32) + (4 << 20)
    return pl.pallas_call(
        _convmix_kernel,
        grid=(nb,),
        in_specs=[pl.BlockSpec((m, ct), lambda j: (0, b0 + j)),
                  pl.BlockSpec((m, ct), lambda j: (0, b0 + nb + j)),
                  pl.BlockSpec((m, ct), lambda j: (0, b0 + 2 * nb + j)),
                  pl.BlockSpec((3, ct), lambda j: (0, j))],
        out_specs=pl.BlockSpec((m, ct), lambda j: (0, j)),
        out_shape=jax.ShapeDtypeStruct((m, c), BF16),
        compiler_params=_cparams(("parallel",), vm),
        name="short_conv_mixer",
    )(p, p, p, conv_w)


def _hyena_prep_kernel(p0_ref, p1_ref, p2_ref, w0_ref, w1_ref, w2_ref, b0_ref, b1_ref, b2_ref,
                       skip_ref, u_ref, x0_ref, usk_ref):
    x0 = _conv3(p0_ref[...].astype(F32), w0_ref) + b0_ref[...]
    x1 = _conv3(p1_ref[...].astype(F32), w1_ref) + b1_ref[...]
    v = _conv3(p2_ref[...].astype(F32), w2_ref) + b2_ref[...]
    u = x1 * v
    u_ref[...] = u.astype(u_ref.dtype)
    x0_ref[...] = x0.astype(x0_ref.dtype)
    usk_ref[...] = (x0 * (u * skip_ref[...])).astype(usk_ref.dtype)


def _hyena_prep(p, conv_w, conv_b, skip):
    m = p.shape[0]
    c = skip.shape[0]
    ct = LANES
    nb = c // ct
    vm = 12 * _nbytes((m, ct), F32) + 10 * _nbytes((m, ct), F32) + (4 << 20)
    blk = lambda off: pl.BlockSpec((m, ct), functools.partial(lambda j, o: (0, o + j), o=off))
    wblk = lambda off: pl.BlockSpec((3, ct), functools.partial(lambda j, o: (0, o + j), o=off))
    bblk = lambda off: pl.BlockSpec((1, ct), functools.partial(lambda j, o: (0, o + j), o=off))
    out = jax.ShapeDtypeStruct((m, c), BF16)
    return pl.pallas_call(
        _hyena_prep_kernel,
        grid=(nb,),
        in_specs=[blk(0), blk(nb), blk(2 * nb), wblk(0), wblk(nb), wblk(2 * nb),
                  bblk(0), bblk(nb), bblk(2 * nb), pl.BlockSpec((1, ct), lambda j: (0, j))],
        out_specs=[pl.BlockSpec((m, ct), lambda j: (0, j))] * 3,
        out_shape=[out, out, out],
        compiler_params=_cparams(("parallel",), vm),
        name="hyena_short_conv_gate",
    )(p, p, p, conv_w, conv_w, conv_w, conv_b.reshape(1, -1), conv_b.reshape(1, -1),
      conv_b.reshape(1, -1), skip.reshape(1, c))


def _filter_feats_t(n, order):
    t = np.asarray(order, np.float64)
    t_norm = t / max(n - 1, 1)
    bands = np.arange(1, FILTER_BANDS + 1, dtype=np.float64)
    ang = (2 * math.pi / n) * bands[:, None] * t[None, :]
    feats = np.concatenate([t_norm[None, :], np.cos(ang), np.sin(ang)], axis=0)
    out = np.zeros((LANES, len(t)), np.float32)
    out[:feats.shape[0]] = feats
    return out


def _split_bf16(a):
    hi = a.astype(BF16).astype(F32)
    return hi, a - hi


def _dot_split(a, b):
    a_hi, a_lo = _split_bf16(a)
    b_hi, b_lo = _split_bf16(b)
    lhs = jnp.concatenate([a_hi, a_hi, a_lo], axis=1).astype(BF16)
    rhs = jnp.concatenate([b_hi, b_lo, b_hi], axis=0).astype(BF16)
    return _dot(lhs, rhs)


def _filter_kernel(f_ref, w1_ref, b1_ref, fr_ref, w2_ref, b2_ref, w3_ref, dec_ref, o_ref, hs_ref, tn_ref,
                   *, n):
    tr = tn_ref.shape[0]
    hid = w3_ref.shape[0]

    @pl.when(pl.program_id(2) == 0)
    def _():
        fr = fr_ref[...]
        h = jnp.sin(fr * (_dot_split(w1_ref[...], f_ref[...]) + b1_ref[...]))
        h = jnp.concatenate([h, jnp.zeros((LANES - hid, tr), F32)], axis=0)
        h = jnp.sin(fr * (_dot_split(w2_ref[...], h) + b2_ref[...]))
        hi, lo = _split_bf16(h)
        pad = jnp.zeros((2 * LANES - 3 * hid, tr), F32)
        hs_ref[...] = jnp.concatenate([hi, hi, lo, pad], axis=0).T.astype(BF16)
        pos = pl.program_id(1) * tr + lax.broadcasted_iota(jnp.int32, (tr, LANES), 0)
        pos = jnp.where(pl.program_id(0) == 0, pos, n - 1 - pos)
        tn_ref[...] = pos.astype(F32) / float(max(n - 1, 1))

    w_hi, w_lo = _split_bf16(w3_ref[...])
    pad = jnp.zeros((2 * LANES - 3 * hid, w_hi.shape[1]), F32)
    rhs = jnp.concatenate([w_hi, w_lo, w_hi, pad], axis=0).astype(BF16)
    hw = _dot(hs_ref[...], rhs)
    t_norm = jnp.concatenate([tn_ref[...]] * (hw.shape[1] // LANES), axis=1)
    window = jnp.exp(-t_norm * jnp.abs(dec_ref[...])) + FILTER_MOD_SHIFT
    o_ref[...] = (hw * window).astype(o_ref.dtype)


def _hyena_filter(n, ch, f_w1, f_b1, f_freq, f_w2, f_b2, f_w3, f_decay, tr, tc):
    hid = f_w1.shape[1]
    assert 3 * hid <= 2 * LANES and hid <= LANES
    feats = np.stack([_filter_feats_t(n, np.arange(n)), _filter_feats_t(n, np.arange(n)[::-1])])
    w1t = jnp.zeros((hid, LANES), F32).at[:, :f_w1.shape[0]].set(f_w1.T)
    w2t = jnp.zeros((hid, LANES), F32).at[:, :hid].set(f_w2.T)
    tr = min(tr, n)
    nrb = n // tr
    ncb = ch // tc
    vm = 6 * _nbytes((tr, tc), F32) + 4 * _nbytes((LANES, tr), F32) + (8 << 20)
    col = lambda a: a.reshape(hid, 1)
    small = lambda shape: pl.BlockSpec(shape, lambda s, i, j: (0, 0))
    return pl.pallas_call(
        functools.partial(_filter_kernel, n=n),
        grid=(2, nrb, ncb),
        in_specs=[pl.BlockSpec((None, LANES, tr), lambda s, i, j: (s, 0, i)),
                  small((hid, LANES)), small((hid, 1)), small((hid, 1)),
                  small((hid, LANES)), small((hid, 1)),
                  pl.BlockSpec((hid, tc), lambda s, i, j: (0, s * ncb + j)),
                  pl.BlockSpec((1, tc), lambda s, i, j: (0, s * ncb + j))],
        out_specs=pl.BlockSpec((tr, tc), lambda s, i, j: (s * nrb + i, j)),
        out_shape=jax.ShapeDtypeStruct((2 * n, ch), BF16),
        scratch_shapes=[pltpu.VMEM((tr, 2 * LANES), BF16), pltpu.VMEM((tr, LANES), F32)],
        compiler_params=_cparams(("parallel", "parallel", "arbitrary"), vm),
        name="hyena_filter",
    )(jnp.asarray(feats), w1t, col(f_b1), col(f_freq), w2t, col(f_b2), f_w3, f_decay.reshape(1, -1))


def _dft_geometry(n):
    big_n = 2 * n
    n2 = big_n // DFT_N1
    nh = -(-(n2 // 2 + 1) // SUBLANES) * SUBLANES
    return big_n, n2, nh


@functools.lru_cache(maxsize=None)
def _stage1_matrix(n, rows_n2):
    big_n, _, nh = _dft_geometry(n)
    groups = DFT_N1 // SUBLANES
    out = np.zeros((groups, 2, nh, SUBLANES, rows_n2, SUBLANES), np.float32)
    k2 = np.arange(nh, dtype=np.float64)[:, None]
    n2v = np.arange(rows_n2, dtype=np.float64)[None, :]
    for g in range(groups):
        for j in range(SUBLANES):
            t = SUBLANES * g + j + DFT_N1 * n2v
            ang = 2 * math.pi * np.mod(k2 * t, big_n) / big_n
            out[g, 0, :, j, :, j] = np.cos(ang)
            out[g, 1, :, j, :, j] = -np.sin(ang)
    return out.reshape(groups, 2 * nh * SUBLANES, rows_n2 * SUBLANES).astype(BF16)


@functools.lru_cache(maxsize=None)
def _stage2_matrices():
    idx = np.arange(DFT_N1, dtype=np.float64)
    ang = 2 * math.pi * np.mod(np.outer(idx, idx), DFT_N1) / DFT_N1
    c, s = np.cos(ang), np.sin(ang)
    fwd = np.block([[c, s], [-s, c]])
    inv = np.block([[c, -s], [s, c]])
    return fwd.astype(BF16), inv.astype(BF16)


@functools.lru_cache(maxsize=None)
def _stage4_matrix(n):
    big_n, n2c, nh = _dft_geometry(n)
    groups = DFT_N1 // SUBLANES
    rows_n2 = n2c // 2
    out = np.zeros((groups, rows_n2, SUBLANES, 2, nh, SUBLANES), np.float32)
    k2 = np.arange(nh, dtype=np.float64)[None, :]
    wgt = np.where(k2 <= n2c // 2, 2.0, 0.0)
    wgt[0, 0] = 1.0
    wgt[0, n2c // 2] = 1.0
    n2v = np.arange(rows_n2, dtype=np.float64)[:, None]
    for g in range(groups):
        for j in range(SUBLANES):
            t = SUBLANES * g + j + DFT_N1 * n2v
            ang = 2 * math.pi * np.mod(k2 * t, big_n) / big_n
            out[g, :, j, 0, :, j] = wgt * np.cos(ang) / big_n
            out[g, :, j, 1, :, j] = -wgt * np.sin(ang) / big_n
    return out.reshape(groups, rows_n2 * SUBLANES, 2 * nh * SUBLANES).astype(BF16)


def _dft_stage1_kernel(x_ref, l_ref, o_ref, *, gsteps):
    nb, _, ct = x_ref.shape
    rows = o_ref.shape[0]
    xin = x_ref[...].astype(F32)
    outs = []
    for gi in range(gsteps):
        sl = slice(SUBLANES * gi, SUBLANES * (gi + 1))
        xv = xin[:, sl, :].reshape(nb * SUBLANES, ct).astype(BF16)
        outs.append(_dot(l_ref[gi], xv).reshape(rows, SUBLANES, ct))
    o_ref[...] = jnp.concatenate(outs, axis=1).astype(o_ref.dtype)


def _dft_stage1(x3, n, gsteps, ct):
    rows_n2, _, c = x3.shape
    _, _, nh = _dft_geometry(n)
    mat = _stage1_matrix(n, rows_n2)
    groups = mat.shape[0]
    ct = min(ct, c)
    gw = SUBLANES * gsteps
    vm = (2 * _nbytes((rows_n2, gw, ct), F32) + 2 * _nbytes((gsteps,) + mat.shape[1:], BF16)
          + 2 * _nbytes((2 * nh, gw, ct), F32) + 3 * _nbytes((2 * nh * SUBLANES, ct), F32) + (4 << 20))
    return pl.pallas_call(
        functools.partial(_dft_stage1_kernel, gsteps=gsteps),
        grid=(groups // gsteps, c // ct),
        in_specs=[pl.BlockSpec((rows_n2, gw, ct), lambda g, j: (0, g, j)),
                  pl.BlockSpec((gsteps,) + mat.shape[1:], lambda g, j: (g, 0, 0))],
        out_specs=pl.BlockSpec((2 * nh, gw, ct), lambda g, j: (0, g, j)),
        out_shape=jax.ShapeDtypeStruct((2 * nh, DFT_N1, c), BF16),
        compiler_params=_cparams(("parallel", "parallel"), vm),
        name="dft_stage_n2",
    )(x3, mat)


def _stack_parts(a_ref, kb):
    cols = [jnp.concatenate([a_ref[0, k], a_ref[1, k]], axis=0) for k in range(kb)]
    return jnp.concatenate(cols, axis=1).astype(BF16)


def _store_parts(o_ref, x, kb):
    ct = o_ref.shape[-1]
    for k in range(kb):
        o_ref[0, k] = x[:DFT_N1, k * ct:(k + 1) * ct].astype(o_ref.dtype)
        o_ref[1, k] = x[DFT_N1:, k * ct:(k + 1) * ct].astype(o_ref.dtype)


def _spectral_product_kernel(a_ref, fa_ref, l2_ref, l3_ref, o_ref, *, kb):
    ct = a_ref.shape[-1]
    x = _dot(l2_ref[...], _stack_parts(a_ref, kb))
    f = _dot(l2_ref[...], _stack_parts(fa_ref, kb))
    ys = []
    for k in range(kb):
        cols = slice(k * ct, (k + 1) * ct)
        xr, xi = x[:DFT_N1, cols], x[DFT_N1:, cols]
        kr, ki = f[:DFT_N1, cols], f[DFT_N1:, cols]
        ys.append(jnp.concatenate([xr * kr - xi * ki, xr * ki + xi * kr], axis=0))
    _store_parts(o_ref, _dot(l3_ref[...], jnp.concatenate(ys, axis=1).astype(BF16)), kb)


def _dft_stage2(a4, fa4, kb, ct):
    _, nh, _, c = a4.shape
    fwd, inv = _stage2_matrices()
    ct = min(ct, c)
    blk = pl.BlockSpec((2, kb, DFT_N1, ct), lambda g, j: (0, g, 0, j))
    mblk = pl.BlockSpec((2 * DFT_N1, 2 * DFT_N1), lambda g, j: (0, 0))
    vm = 6 * _nbytes((2, kb, DFT_N1, ct), BF16) + 8 * _nbytes((2 * DFT_N1, kb * ct), F32) + (4 << 20)
    return pl.pallas_call(
        functools.partial(_spectral_product_kernel, kb=kb),
        grid=(nh // kb, c // ct),
        in_specs=[blk, blk, mblk, mblk],
        out_specs=blk,
        out_shape=jax.ShapeDtypeStruct(a4.shape, BF16),
        compiler_params=_cparams(("parallel", "parallel"), vm),
        name="dft_stage_n1_product",
    )(a4, fa4, fwd, inv)


def _dft_stage4_kernel(v_ref, l_ref, x0_ref, usk_ref, o_ref, *, gsteps):
    rows, _, ct = v_ref.shape
    nb = o_ref.shape[0]
    vin = v_ref[...].astype(F32)
    ys = []
    for gi in range(gsteps):
        sl = slice(SUBLANES * gi, SUBLANES * (gi + 1))
        vv = vin[:, sl, :].reshape(rows * SUBLANES, ct).astype(BF16)
        ys.append(_dot(l_ref[gi], vv).reshape(nb, SUBLANES, ct))
    y = jnp.concatenate(ys, axis=1)
    o_ref[...] = (x0_ref[...].astype(F32) * y + usk_ref[...].astype(F32)).astype(o_ref.dtype)


def _dft_stage4(v3, x03, usk3, n, gsteps, ct):
    rows, _, c = v3.shape
    mat = _stage4_matrix(n)
    groups = mat.shape[0]
    nb = x03.shape[0]
    ct = min(ct, c)
    gw = SUBLANES * gsteps
    vm = (2 * _nbytes((rows, gw, ct), F32) + 2 * _nbytes((gsteps,) + mat.shape[1:], BF16)
          + 6 * _nbytes((nb, gw, ct), F32) + 3 * _nbytes((rows * SUBLANES, ct), F32) + (4 << 20))
    oblk = pl.BlockSpec((nb, gw, ct), lambda g, j: (0, g, j))
    return pl.pallas_call(
        functools.partial(_dft_stage4_kernel, gsteps=gsteps),
        grid=(groups // gsteps, c // ct),
        in_specs=[pl.BlockSpec((rows, gw, ct), lambda g, j: (0, g, j)),
                  pl.BlockSpec((gsteps,) + mat.shape[1:], lambda g, j: (g, 0, 0)),
                  oblk, oblk],
        out_specs=oblk,
        out_shape=jax.ShapeDtypeStruct((nb, DFT_N1, c), BF16),
        compiler_params=_cparams(("parallel", "parallel"), vm),
        name="dft_stage_k2_inverse",
    )(v3, mat, x03, usk3)


def _long_conv_gate(u, x0, usk, filt2, n):
    c = u.shape[1]
    _, _, nh = _dft_geometry(n)
    as3 = lambda a: a.reshape(a.shape[0] // DFT_N1, DFT_N1, c)
    fa = _dft_stage1(as3(filt2), n, gsteps=2, ct=512)
    a = _dft_stage1(as3(u), n, gsteps=2, ct=512)
    v = _dft_stage2(a.reshape(2, nh, DFT_N1, c), fa.reshape(2, nh, DFT_N1, c), kb=SUBLANES, ct=512)
    y = _dft_stage4(v.reshape(2 * nh, DFT_N1, c), as3(x0), as3(usk), n, gsteps=2, ct=512)
    return y.reshape(n, c)


@functools.lru_cache(maxsize=None)
def _dense_dft_matrices(n):
    big_n = 2 * n
    nf = n + 1
    nfp = -(-nf // SUBLANES) * SUBLANES
    k = np.arange(nf, dtype=np.float64)[:, None]
    t = np.arange(big_n, dtype=np.float64)[None, :]
    ang = 2 * math.pi * np.mod(k * t, big_n) / big_n
    fwd = np.zeros((2, nfp, big_n), np.float32)
    fwd[0, :nf] = np.cos(ang)
    fwd[1, :nf] = -np.sin(ang)
    wgt = np.full((nf, 1), 2.0)
    wgt[0] = 1.0
    wgt[n] = 1.0
    inv = np.zeros((2, nfp, n), np.float32)
    inv[0, :nf] = (wgt * np.cos(ang) / big_n)[:, :n]
    inv[1, :nf] = (-wgt * np.sin(ang) / big_n)[:, :n]
    fwd = fwd.reshape(2 * nfp, big_n)
    inv = inv.reshape(2 * nfp, n).T
    return (np.ascontiguousarray(fwd[:, :n]).astype(BF16), fwd.astype(BF16),
            np.ascontiguousarray(inv).astype(BF16))


def _short_conv_gate_kernel(u_ref, f_ref, x0_ref, usk_ref, lu_ref, lf_ref, li_ref, o_ref):
    half = lu_ref.shape[0] // 2
    xs = _dot(lu_ref[...], u_ref[...].astype(BF16))
    ks = _dot(lf_ref[...], f_ref[...].astype(BF16))
    xr, xi, kr, ki = xs[:half], xs[half:], ks[:half], ks[half:]
    y = jnp.concatenate([xr * kr - xi * ki, xr * ki + xi * kr], axis=0).astype(BF16)
    conv = _dot(li_ref[...], y)
    o_ref[...] = (x0_ref[...].astype(F32) * conv + usk_ref[...].astype(F32)).astype(o_ref.dtype)


def _long_conv_gate_short(u, x0, usk, filt2, n):
    c = u.shape[1]
    lu, lf, li = _dense_dft_matrices(n)
    ct = 256
    blk = pl.BlockSpec((n, ct), lambda j: (0, j))
    full = lambda a: pl.BlockSpec(a.shape, lambda j: (0, 0))
    vm = 32 << 20
    return pl.pallas_call(
        _short_conv_gate_kernel,
        grid=(c // ct,),
        in_specs=[blk, pl.BlockSpec((2 * n, ct), lambda j: (0, j)), blk, blk, full(lu), full(lf), full(li)],
        out_specs=blk,
        out_shape=jax.ShapeDtypeStruct((n, c), BF16),
        compiler_params=_cparams(("parallel",), vm),
        name="context_long_conv",
    )(u, filt2, x0, usk, lu, lf, li)


@functools.lru_cache(maxsize=None)
def _rope_tables(n):
    t = np.arange(n)
    row = (t // GRID_W).astype(np.float64)
    col = (t % GRID_W).astype(np.float64)
    axis_dim = HEAD_DIM // 2
    inv = 1.0 / (ROPE_THETA ** (np.arange(0, axis_dim, 2, dtype=np.float64) / axis_dim))
    ar, ac = row[:, None] * inv, col[:, None] * inv
    cos = np.concatenate([np.cos(ar), np.cos(ar), np.cos(ac), np.cos(ac)], axis=-1)
    sin = np.concatenate([-np.sin(ar), np.sin(ar), -np.sin(ac), np.sin(ac)], axis=-1)
    return cos.astype(np.float32), sin.astype(np.float32)


TM_PROJ = 512
TM_IN = 1024
TN_IN = 768
TF_FFN = 1024
TQ_ATTN = 512
TK_ATTN = 2048


def kernel(x, c, ctx, c_ctx, mod_w, mod_b, norm_mix_g, norm_ffn_g, ffn_w1, ffn_w2, ev_w_in, ev_q_norm, ev_k_norm, ev_conv_w, ev_w_out, od_w_in, od_conv_w, od_conv_b, od_f_w1, od_f_b1, od_f_freq, od_f_w2, od_f_b2, od_f_w3, od_f_decay, od_skip, od_w_out, final_g):
    batch, n, d = x.shape
    assert batch == 1
    n_ctx = ctx.shape[1]
    depth = mod_w.shape[0]
    last_ctx_read = ((depth - 1) // 2) * 2
    attn_w = N_Q_HEADS * HEAD_DIM
    kv_w = N_KV_HEADS * HEAD_DIM

    xs = x[0]
    xc = ctx[0]
    cond = jnp.zeros((SUBLANES, d), F32).at[0].set(c[0]).at[1].set(c_ctx)
    modv = _mod_vectors(cond, mod_w, mod_b)
    rope = _rope_tables(n)

    w1_all, w2_all = ffn_w1.astype(BF16), ffn_w2.astype(BF16)
    ev_in, ev_out = ev_w_in.astype(BF16), ev_w_out.astype(BF16)
    od_in, od_out = od_w_in.astype(BF16), od_w_out.astype(BF16)
    g_mix = norm_mix_g.reshape(depth, 1, d)
    g_ffn = norm_ffn_g.reshape(depth, 1, d)
    kv_block = attn_w // (2 * kv_w)
    conv_col = attn_w + 2 * kv_w

    for l in range(depth):
        ctx_full = l < last_ctx_read
        if l % 2 == 0:
            e = l // 2
            p_l = _modmm(xs, g_mix, modv, l, 0, 0, ev_in, e, TM_IN, TN_IN)
            q_l, k_l, v_l = _qkv_prep(p_l, 0, kv_block, ev_q_norm[e], ev_k_norm[e], rope, TM_PROJ)
            p_c = _modmm(xc, g_mix, modv, l, 1, 0, ev_in, e, TM_IN, TN_IN)
            q_c, k_c, v_c = _qkv_prep(p_c, 0, kv_block, ev_q_norm[e], ev_k_norm[e], None, TM_PROJ)
            att_l = _attention(q_l, k_l, v_l, k_c, v_c, TQ_ATTN, TK_ATTN)
            conv_l = _short_conv_mixer(p_l, conv_col, ev_conv_w[e])
            if ctx_full:
                att_c = _attention(q_c, k_c, v_c, None, None, TQ_ATTN, TK_ATTN)
                conv_c = _short_conv_mixer(p_c, conv_col, ev_conv_w[e])
                xc = _outproj([att_c, conv_c], ev_out, e, xc, modv, l, 1, 2, TM_PROJ)
            xs = _outproj([att_l, conv_l], ev_out, e, xs, modv, l, 0, 2, TM_PROJ)
        else:
            o = l // 2
            fargs = (od_f_w1[o], od_f_b1[o], od_f_freq[o], od_f_w2[o], od_f_b2[o], od_f_w3[o], od_f_decay[o])
            p_l = _modmm(xs, g_mix, modv, l, 0, 0, od_in, o, TM_IN, TN_IN, out_dtype=BF16)
            u, x0, usk = _hyena_prep(p_l, od_conv_w[o], od_conv_b[o], od_skip[o])
            filt2 = _hyena_filter(n, d, *fargs, tr=1024, tc=1024)
            y = _long_conv_gate(u, x0, usk, filt2, n)
            xs = _outproj([y], od_out, o, xs, modv, l, 0, 2, TM_PROJ)
            if ctx_full:
                p_c = _modmm(xc, g_mix, modv, l, 1, 0, od_in, o, TM_IN, TN_IN, out_dtype=BF16)
                u, x0, usk = _hyena_prep(p_c, od_conv_w[o], od_conv_b[o], od_skip[o])
                filt2 = _hyena_filter(n_ctx, d, *fargs, tr=1024, tc=1024)
                y = _long_conv_gate_short(u, x0, usk, filt2, n_ctx)
                xc = _outproj([y], od_out, o, xc, modv, l, 1, 2, TM_PROJ)
        last = l == depth - 1
        xs = _ffn(xs, g_ffn, modv, l, 0, w1_all, w2_all, TM_PROJ, TF_FFN, final_gain=final_g if last else None)
        if ctx_full:
            xc = _ffn(xc, g_ffn, modv, l, 1, w1_all, w2_all, TM_PROJ, TF_FFN)

    return xs[None]
```

```python
import functools
import math

import numpy as np
import jax
import jax.numpy as jnp
from jax import lax
from jax.experimental import pallas as pl
from jax.experimental.pallas import tpu as pltpu

F32 = jnp.float32
BF16 = jnp.bfloat16

HEAD_DIM = 128
N_Q_HEADS = 8
N_KV_HEADS = 2
Q_PER_KV = N_Q_HEADS // N_KV_HEADS
GRID_W = 64
ROPE_THETA = 10000.0
FILTER_BANDS = 16
FILTER_MOD_SHIFT = 0.05
EPS = 1e-6

LANES = 128
SUBLANES = 8
VMEM_BUDGET = 56 * 1024 * 1024
DFT_N1 = 128
ONES_ROWS = 16


def _cparams(sem, vmem_bytes):
    return pltpu.CompilerParams(dimension_semantics=sem,
                                vmem_limit_bytes=int(min(max(vmem_bytes, 16 << 20), VMEM_BUDGET)))


def _nbytes(shape, dtype):
    return int(np.prod(shape)) * jnp.dtype(dtype).itemsize


def _dot(a, b):
    return jnp.dot(a, b, preferred_element_type=F32)


def _rms_rows(x):
    return x * lax.rsqrt(jnp.mean(x * x, axis=-1, keepdims=True) + EPS)


def _mod_kernel(c_ref, w_ref, b_ref, o_ref):
    c = c_ref[...]
    s = c * (1.0 / (1.0 + jnp.exp(-c)))
    o_ref[0] = _dot(s.astype(BF16), w_ref[0].astype(BF16)) + b_ref[0]


def _mod_vectors(cond, mod_w, mod_b):
    depth, d, n6 = mod_w.shape
    tn = 1536
    rows = cond.shape[0]
    vm = 2 * _nbytes((d, tn), F32) + _nbytes((d, tn), BF16) + (4 << 20)
    return pl.pallas_call(
        _mod_kernel,
        grid=(depth, n6 // tn),
        in_specs=[pl.BlockSpec((rows, d), lambda l, j: (0, 0)),
                  pl.BlockSpec((1, d, tn), lambda l, j: (l, 0, j)),
                  pl.BlockSpec((1, 1, tn), lambda l, j: (l, 0, j))],
        out_specs=pl.BlockSpec((1, rows, tn), lambda l, j: (l, 0, j)),
        out_shape=jax.ShapeDtypeStruct((depth, rows, n6), F32),
        compiler_params=_cparams(("parallel", "parallel"), vm),
        name="adaln_vectors",
    )(cond, mod_w, mod_b.reshape(depth, 1, n6))


def _modmm_kernel(x_ref, g_ref, sh_ref, sc_ref, w_ref, o_ref, h_ref, *, row):
    @pl.when(pl.program_id(1) == 0)
    def _():
        h = _rms_rows(x_ref[...]) * g_ref[...]
        h = h * (1.0 + sc_ref[row:row + 1, :]) + sh_ref[row:row + 1, :]
        h_ref[...] = h.astype(BF16)

    o_ref[...] = _dot(h_ref[...], w_ref[...]).astype(o_ref.dtype)


def _modmm(x, gains, modv, l, row, col0, w, wi, tm, tn, out_dtype=F32):
    m, d = x.shape
    n = w.shape[2]
    tm = min(tm, m)
    vm = (2 * _nbytes((tm, d), F32) + _nbytes((tm, d), BF16) + 2 * _nbytes((d, tn), BF16)
          + 2 * _nbytes((tm, tn), F32) + 2 * _nbytes((tm, d), F32) + (4 << 20))
    return pl.pallas_call(
        functools.partial(_modmm_kernel, row=row),
        grid=(m // tm, n // tn),
        in_specs=[pl.BlockSpec((tm, d), lambda i, j: (i, 0)),
                  pl.BlockSpec((None, 1, d), lambda i, j: (l, 0, 0)),
                  pl.BlockSpec((None, SUBLANES, d), lambda i, j: (l, 0, col0)),
                  pl.BlockSpec((None, SUBLANES, d), lambda i, j: (l, 0, col0 + 1)),
                  pl.BlockSpec((None, d, tn), lambda i, j: (wi, 0, j))],
        out_specs=pl.BlockSpec((tm, tn), lambda i, j: (i, j)),
        out_shape=jax.ShapeDtypeStruct((m, n), out_dtype),
        scratch_shapes=[pltpu.VMEM((tm, d), BF16)],
        compiler_params=_cparams(("parallel", "arbitrary"), vm),
        name="modulated_projection",
    )(x, gains, modv, modv, w)


def _outproj_kernel(*refs, n_in, row):
    a_refs = refs[:n_in]
    w_refs = refs[n_in:2 * n_in]
    x_ref, g_ref, o_ref = refs[2 * n_in:]
    acc = _dot(a_refs[0][...].astype(BF16), w_refs[0][...])
    for a_ref, w_ref in zip(a_refs[1:], w_refs[1:]):
        acc = acc + _dot(a_ref[...].astype(BF16), w_ref[...])
    o_ref[...] = x_ref[...] + g_ref[row:row + 1, :] * acc


def _outproj(acts, w, wi, x, modv, l, row, gate_col, tm):
    m, d = x.shape
    tm = min(tm, m)
    n_in = len(acts)
    in_specs, args = [], []
    for a in acts:
        in_specs.append(pl.BlockSpec((tm, a.shape[1]), lambda i: (i, 0)))
        args.append(a)
    row0 = 0
    for a in acts:
        k = a.shape[1]
        in_specs.append(pl.BlockSpec((None, k, d), functools.partial(lambda i, b: (wi, b, 0), b=row0 // k)))
        args.append(w)
        row0 += k
    in_specs += [pl.BlockSpec((tm, d), lambda i: (i, 0)),
                 pl.BlockSpec((None, SUBLANES, d), lambda i: (l, 0, gate_col))]
    args += [x, modv]
    vm = (2 * sum(_nbytes((tm, a.shape[1]), a.dtype) for a in acts) + 2 * _nbytes(w.shape[1:], BF16)
          + 6 * _nbytes((tm, d), F32) + (4 << 20))
    return pl.pallas_call(
        functools.partial(_outproj_kernel, n_in=n_in, row=row),
        grid=(m // tm,),
        in_specs=in_specs,
        out_specs=pl.BlockSpec((tm, d), lambda i: (i, 0)),
        out_shape=jax.ShapeDtypeStruct((m, d), F32),
        compiler_params=_cparams(("parallel",), vm),
        name="gated_out_projection",
    )(*args)


def _ffn_kernel(*refs, row, final_norm):
    if final_norm:
        x_ref, g_ref, sh_ref, sc_ref, gate_ref, w1_ref, w2_ref, fg_ref, o_ref, h_ref = refs
    else:
        x_ref, g_ref, sh_ref, sc_ref, gate_ref, w1_ref, w2_ref, o_ref, h_ref = refs
    j = pl.program_id(1)

    @pl.when(j == 0)
    def _():
        h = _rms_rows(x_ref[...]) * g_ref[...]
        h = h * (1.0 + sc_ref[row:row + 1, :]) + sh_ref[row:row + 1, :]
        h_ref[...] = h.astype(BF16)
        o_ref[...] = jnp.zeros(o_ref.shape, F32)

    a = jnp.maximum(_dot(h_ref[...], w1_ref[...]), 0.0)
    o_ref[...] += _dot((a * a).astype(BF16), w2_ref[...])

    @pl.when(j == pl.num_programs(1) - 1)
    def _():
        y = x_ref[...] + gate_ref[row:row + 1, :] * o_ref[...]
        if final_norm:
            y = _rms_rows(y) * fg_ref[...]
        o_ref[...] = y


def _ffn(x, gains, modv, l, row, w1, w2, tm, tf, final_gain=None):
    m, d = x.shape
    f = w1.shape[2]
    tm = min(tm, m)
    vm = (4 * _nbytes((tm, d), F32) + _nbytes((tm, d), BF16)
          + 4 * _nbytes((d, tf), BF16) + 3 * _nbytes((tm, tf), F32) + (6 << 20))
    in_specs = [pl.BlockSpec((tm, d), lambda i, j: (i, 0)),
                pl.BlockSpec((None, 1, d), lambda i, j: (l, 0, 0)),
                pl.BlockSpec((None, SUBLANES, d), lambda i, j: (l, 0, 3)),
                pl.BlockSpec((None, SUBLANES, d), lambda i, j: (l, 0, 4)),
                pl.BlockSpec((None, SUBLANES, d), lambda i, j: (l, 0, 5)),
                pl.BlockSpec((None, d, tf), lambda i, j: (l, 0, j)),
                pl.BlockSpec((None, tf, d), lambda i, j: (l, j, 0))]
    args = [x, gains, modv, modv, modv, w1, w2]
    if final_gain is not None:
        in_specs.append(pl.BlockSpec((1, d), lambda i, j: (0, 0)))
        args.append(final_gain.reshape(1, d))
    return pl.pallas_call(
        functools.partial(_ffn_kernel, row=row, final_norm=final_gain is not None),
        grid=(m // tm, f // tf),
        in_specs=in_specs,
        out_specs=pl.BlockSpec((tm, d), lambda i, j: (i, 0)),
        out_shape=jax.ShapeDtypeStruct((m, d), F32),
        scratch_shapes=[pltpu.VMEM((tm, d), BF16)],
        compiler_params=_cparams(("parallel", "arbitrary"), vm),
        name="gated_ffn",
    )(*args)


def _rope(x, cos, sin):
    lane = lax.broadcasted_iota(jnp.int32, x.shape, 1)
    first_half = (lane % (HEAD_DIM // 2)) < (HEAD_DIM // 4)
    partner = jnp.where(first_half,
                        pltpu.roll(x, HEAD_DIM - HEAD_DIM // 4, 1),
                        pltpu.roll(x, HEAD_DIM // 4, 1))
    return x * cos + partner * sin


def _qkv_kernel(*refs, use_rope, q_scale):
    if use_rope:
        pq_ref, pkv_ref, qg_ref, kg_ref, cos_ref, sin_ref, q_ref, k_ref, v_ref = refs
        cos, sin = cos_ref[...], sin_ref[...]
    else:
        pq_ref, pkv_ref, qg_ref, kg_ref, q_ref, k_ref, v_ref = refs
    kv_w = N_KV_HEADS * HEAD_DIM
    for h in range(N_Q_HEADS):
        sl = slice(h * HEAD_DIM, (h + 1) * HEAD_DIM)
        xn = _rms_rows(pq_ref[:, sl]) * qg_ref[...]
        if use_rope:
            xn = _rope(xn, cos, sin)
        q_ref[:, sl] = (xn * q_scale).astype(BF16)
    for h in range(N_KV_HEADS):
        sl = slice(h * HEAD_DIM, (h + 1) * HEAD_DIM)
        xn = _rms_rows(pkv_ref[:, sl]) * kg_ref[...]
        if use_rope:
            xn = _rope(xn, cos, sin)
        k_ref[:, sl] = xn.astype(BF16)
    v_ref[...] = pkv_ref[:, kv_w:2 * kv_w].T.astype(BF16)


def _qkv_prep(p, q_col_block, kv_col_block, q_gain, k_gain, rope, tm):
    m = p.shape[0]
    tm = min(tm, m)
    attn_w = N_Q_HEADS * HEAD_DIM
    kv_w = N_KV_HEADS * HEAD_DIM
    in_specs = [pl.BlockSpec((tm, attn_w), lambda i: (i, q_col_block)),
                pl.BlockSpec((tm, 2 * kv_w), lambda i: (i, kv_col_block)),
                pl.BlockSpec((1, HEAD_DIM), lambda i: (0, 0)),
                pl.BlockSpec((1, HEAD_DIM), lambda i: (0, 0))]
    args = [p, p, q_gain.reshape(1, HEAD_DIM), k_gain.reshape(1, HEAD_DIM)]
    if rope is not None:
        in_specs += [pl.BlockSpec((tm, HEAD_DIM), lambda i: (i, 0))] * 2
        args += list(rope)
    vm = 4 * _nbytes((tm, attn_w + 2 * kv_w), F32) + (8 << 20)
    return pl.pallas_call(
        functools.partial(_qkv_kernel, use_rope=rope is not None, q_scale=HEAD_DIM ** -0.5),
        grid=(m // tm,),
        in_specs=in_specs,
        out_specs=[pl.BlockSpec((tm, attn_w), lambda i: (i, 0)),
                   pl.BlockSpec((tm, kv_w), lambda i: (i, 0)),
                   pl.BlockSpec((kv_w, tm), lambda i: (0, i))],
        out_shape=[jax.ShapeDtypeStruct((m, attn_w), BF16),
                   jax.ShapeDtypeStruct((m, kv_w), BF16),
                   jax.ShapeDtypeStruct((kv_w, m), BF16)],
        compiler_params=_cparams(("parallel",), vm),
        name="qkv_norm_rope",
    )(*args)


def _attn_kernel(*refs, has_extra, tq):
    if has_extra:
        q_ref, k_ref, vt_ref, ke_ref, vte_ref, o_ref, qs_ref, m_ref, acc_ref = refs
    else:
        q_ref, k_ref, vt_ref, o_ref, qs_ref, m_ref, acc_ref = refs
    kv = pl.program_id(2)

    def update(k, vt):
        vt1 = jnp.concatenate([vt, jnp.ones((ONES_ROWS, vt.shape[1]), BF16)], axis=0)
        cols = [slice(h * tq, (h + 1) * tq) for h in range(Q_PER_KV)]
        s, p, alpha = {}, {}, {}
        for t in range(Q_PER_KV + 2):
            if t < Q_PER_KV:
                s[t] = lax.dot_general(k, qs_ref[cols[t], :], (((1,), (1,)), ((), ())),
                                       preferred_element_type=F32)
            h = t - 1
            if 0 <= h < Q_PER_KV:
                m_prev = m_ref[:, cols[h]]
                m_new = jnp.maximum(m_prev, jnp.max(s[h], axis=0, keepdims=True))
                alpha[h] = jnp.exp(m_prev - m_new)
                p[h] = jnp.exp(s.pop(h) - m_new).astype(BF16)
                m_ref[:, cols[h]] = m_new
            h = t - 2
            if 0 <= h < Q_PER_KV:
                acc_ref[:, cols[h]] = alpha.pop(h) * acc_ref[:, cols[h]] + _dot(vt1, p.pop(h))

    @pl.when(kv == 0)
    def _():
        for h in range(Q_PER_KV):
            qs_ref[h * tq:(h + 1) * tq, :] = q_ref[:, h * HEAD_DIM:(h + 1) * HEAD_DIM]
        m_ref[...] = jnp.full(m_ref.shape, -jnp.inf, F32)
        acc_ref[...] = jnp.zeros(acc_ref.shape, F32)
        if has_extra:
            update(ke_ref[...], vte_ref[...])

    update(k_ref[...], vt_ref[...])

    @pl.when(kv == pl.num_programs(2) - 1)
    def _():
        o_t = acc_ref[0:HEAD_DIM, :] / acc_ref[HEAD_DIM:HEAD_DIM + 1, :]
        for h in range(Q_PER_KV):
            o_ref[:, h * HEAD_DIM:(h + 1) * HEAD_DIM] = o_t[:, h * tq:(h + 1) * tq].T.astype(o_ref.dtype)


def _attention(q, k, vt, k_extra, vt_extra, tq, tk):
    m = q.shape[0]
    s_len = k.shape[0]
    tq = min(tq, m)
    tk = min(tk, s_len)
    gw = Q_PER_KV * HEAD_DIM
    cols = Q_PER_KV * tq
    has_extra = k_extra is not None
    in_specs = [pl.BlockSpec((tq, gw), lambda g, i, j: (i, g)),
                pl.BlockSpec((tk, HEAD_DIM), lambda g, i, j: (j, g)),
                pl.BlockSpec((HEAD_DIM, tk), lambda g, i, j: (g, j))]
    args = [q, k, vt]
    if has_extra:
        e = k_extra.shape[0]
        in_specs += [pl.BlockSpec((e, HEAD_DIM), lambda g, i, j: (0, g)),
                     pl.BlockSpec((HEAD_DIM, e), lambda g, i, j: (g, 0))]
        args += [k_extra, vt_extra]
    acc_rows = HEAD_DIM + ONES_ROWS
    vm = 4 * _nbytes((tk, cols), F32) + 6 * _nbytes((acc_rows, cols), F32) + (8 << 20)
    return pl.pallas_call(
        functools.partial(_attn_kernel, has_extra=has_extra, tq=tq),
        grid=(N_KV_HEADS, m // tq, s_len // tk),
        in_specs=in_specs,
        out_specs=pl.BlockSpec((tq, gw), lambda g, i, j: (i, g)),
        out_shape=jax.ShapeDtypeStruct((m, N_Q_HEADS * HEAD_DIM), BF16),
        scratch_shapes=[pltpu.VMEM((cols, HEAD_DIM), BF16),
                        pltpu.VMEM((1, cols), F32),
                        pltpu.VMEM((acc_rows, cols), F32)],
        compiler_params=_cparams(("parallel", "parallel", "arbitrary"), vm),
        name="gqa_flash_attention",
    )(*args)


def _conv3(z, w_ref):
    n = z.shape[0]
    prev = pltpu.roll(z, 1, 0)
    nxt = pltpu.roll(z, n - 1, 0)
    row = lax.broadcasted_iota(jnp.int32, (SUBLANES, z.shape[1]), 0)
    prev = jnp.concatenate([jnp.where(row == 0, 0.0, prev[:SUBLANES]), prev[SUBLANES:]], axis=0)
    nxt = jnp.concatenate([nxt[:n - SUBLANES], jnp.where(row == SUBLANES - 1, 0.0, nxt[n - SUBLANES:])], axis=0)
    return prev * w_ref[0:1, :] + z * w_ref[1:2, :] + nxt * w_ref[2:3, :]


def _convmix_kernel(u_ref, b_ref, c_ref, w_ref, o_ref):
    o_ref[...] = (b_ref[...] * _conv3(c_ref[...] * u_ref[...], w_ref)).astype(o_ref.dtype)


def _short_conv_mixer(p, col0, conv_w):
    m = p.shape[0]
    c = conv_w.shape[1]
    ct = LANES
    nb = c // ct
    b0 = col0 // ct
    vm = 8 * _nbytes((m, ct), F32) + 8 * _nbytes((m, ct), F32) + (4 << 20)
    return pl.pallas_call(
        _convmix_kernel,
        grid=(nb,),
        in_specs=[pl.BlockSpec((m, ct), lambda j: (0, b0 + j)),
                  pl.BlockSpec((m, ct), lambda j: (0, b0 + nb + j)),
                  pl.BlockSpec((m, ct), lambda j: (0, b0 + 2 * nb + j)),
                  pl.BlockSpec((3, ct), lambda j: (0, j))],
        out_specs=pl.BlockSpec((m, ct), lambda j: (0, j)),
        out_shape=jax.ShapeDtypeStruct((m, c), BF16),
        compiler_params=_cparams(("parallel",), vm),
        name="short_conv_mixer",
    )(p, p, p, conv_w)


def _hyena_prep_kernel(p0_ref, p1_ref, p2_ref, w0_ref, w1_ref, w2_ref, b0_ref, b1_ref, b2_ref,
                       skip_ref, u_ref, x0_ref, usk_ref):
    x0 = _conv3(p0_ref[...].astype(F32), w0_ref) + b0_ref[...]
    x1 = _conv3(p1_ref[...].astype(F32), w1_ref) + b1_ref[...]
    v = _conv3(p2_ref[...].astype(F32), w2_ref) + b2_ref[...]
    u = x1 * v
    u_ref[...] = u.astype(u_ref.dtype)
    x0_ref[...] = x0.astype(x0_ref.dtype)
    usk_ref[...] = (x0 * (u * skip_ref[...])).astype(usk_ref.dtype)


def _hyena_prep(p, conv_w, conv_b, skip):
    m = p.shape[0]
    c = skip.shape[0]
    ct = LANES
    nb = c // ct
    vm = 12 * _nbytes((m, ct), F32) + 10 * _nbytes((m, ct), F32) + (4 << 20)
    blk = lambda off: pl.BlockSpec((m, ct), functools.partial(lambda j, o: (0, o + j), o=off))
    wblk = lambda off: pl.BlockSpec((3, ct), functools.partial(lambda j, o: (0, o + j), o=off))
    bblk = lambda off: pl.BlockSpec((1, ct), functools.partial(lambda j, o: (0, o + j), o=off))
    out = jax.ShapeDtypeStruct((m, c), BF16)
    return pl.pallas_call(
        _hyena_prep_kernel,
        grid=(nb,),
        in_specs=[blk(0), blk(nb), blk(2 * nb), wblk(0), wblk(nb), wblk(2 * nb),
                  bblk(0), bblk(nb), bblk(2 * nb), pl.BlockSpec((1, ct), lambda j: (0, j))],
        out_specs=[pl.BlockSpec((m, ct), lambda j: (0, j))] * 3,
        out_shape=[out, out, out],
        compiler_params=_cparams(("parallel",), vm),
        name="hyena_short_conv_gate",
    )(p, p, p, conv_w, conv_w, conv_w, conv_b.reshape(1, -1), conv_b.reshape(1, -1),
      conv_b.reshape(1, -1), skip.reshape(1, c))


def _filter_feats_t(n, order):
    t = np.asarray(order, np.float64)
    t_norm = t / max(n - 1, 1)
    bands = np.arange(1, FILTER_BANDS + 1, dtype=np.float64)
    ang = (2 * math.pi / n) * bands[:, None] * t[None, :]
    feats = np.concatenate([t_norm[None, :], np.cos(ang), np.sin(ang)], axis=0)
    out = np.zeros((LANES, len(t)), np.float32)
    out[:feats.shape[0]] = feats
    return out


def _split_bf16(a):
    hi = a.astype(BF16).astype(F32)
    return hi, a - hi


def _dot_split(a, b):
    a_hi, a_lo = _split_bf16(a)
    b_hi, b_lo = _split_bf16(b)
    lhs = jnp.concatenate([a_hi, a_hi, a_lo], axis=1).astype(BF16)
    rhs = jnp.concatenate([b_hi, b_lo, b_hi], axis=0).astype(BF16)
    return _dot(lhs, rhs)


def _filter_kernel(f_ref, w1_ref, b1_ref, fr_ref, w2_ref, b2_ref, w3_ref, dec_ref, o_ref, hs_ref, tn_ref,
                   *, n):
    tr = tn_ref.shape[0]
    hid = w3_ref.shape[0]

    @pl.when(pl.program_id(2) == 0)
    def _():
        fr = fr_ref[...]
        h = jnp.sin(fr * (_dot_split(w1_ref[...], f_ref[...]) + b1_ref[...]))
        h = jnp.concatenate([h, jnp.zeros((LANES - hid, tr), F32)], axis=0)
        h = jnp.sin(fr * (_dot_split(w2_ref[...], h) + b2_ref[...]))
        hi, lo = _split_bf16(h)
        pad = jnp.zeros((2 * LANES - 3 * hid, tr), F32)
        hs_ref[...] = jnp.concatenate([hi, hi, lo, pad], axis=0).T.astype(BF16)
        pos = pl.program_id(1) * tr + lax.broadcasted_iota(jnp.int32, (tr, LANES), 0)
        pos = jnp.where(pl.program_id(0) == 0, pos, n - 1 - pos)
        tn_ref[...] = pos.astype(F32) / float(max(n - 1, 1))

    w_hi, w_lo = _split_bf16(w3_ref[...])
    pad = jnp.zeros((2 * LANES - 3 * hid, w_hi.shape[1]), F32)
    rhs = jnp.concatenate([w_hi, w_lo, w_hi, pad], axis=0).astype(BF16)
    hw = _dot(hs_ref[...], rhs)
    t_norm = jnp.concatenate([tn_ref[...]] * (hw.shape[1] // LANES), axis=1)
    window = jnp.exp(-t_norm * jnp.abs(dec_ref[...])) + FILTER_MOD_SHIFT
    o_ref[...] = (hw * window).astype(o_ref.dtype)


def _hyena_filter(n, ch, f_w1, f_b1, f_freq, f_w2, f_b2, f_w3, f_decay, tr, tc):
    hid = f_w1.shape[1]
    assert 3 * hid <= 2 * LANES and hid <= LANES
    feats = np.stack([_filter_feats_t(n, np.arange(n)), _filter_feats_t(n, np.arange(n)[::-1])])
    w1t = jnp.zeros((hid, LANES), F32).at[:, :f_w1.shape[0]].set(f_w1.T)
    w2t = jnp.zeros((hid, LANES), F32).at[:, :hid].set(f_w2.T)
    tr = min(tr, n)
    nrb = n // tr
    ncb = ch // tc
    vm = 6 * _nbytes((tr, tc), F32) + 4 * _nbytes((LANES, tr), F32) + (8 << 20)
    col = lambda a: a.reshape(hid, 1)
    small = lambda shape: pl.BlockSpec(shape, lambda s, i, j: (0, 0))
    return pl.pallas_call(
        functools.partial(_filter_kernel, n=n),
        grid=(2, nrb, ncb),
        in_specs=[pl.BlockSpec((None, LANES, tr), lambda s, i, j: (s, 0, i)),
                  small((hid, LANES)), small((hid, 1)), small((hid, 1)),
                  small((hid, LANES)), small((hid, 1)),
                  pl.BlockSpec((hid, tc), lambda s, i, j: (0, s * ncb + j)),
                  pl.BlockSpec((1, tc), lambda s, i, j: (0, s * ncb + j))],
        out_specs=pl.BlockSpec((tr, tc), lambda s, i, j: (s * nrb + i, j)),
        out_shape=jax.ShapeDtypeStruct((2 * n, ch), BF16),
        scratch_shapes=[pltpu.VMEM((tr, 2 * LANES), BF16), pltpu.VMEM((tr, LANES), F32)],
        compiler_params=_cparams(("parallel", "parallel", "arbitrary"), vm),
        name="hyena_filter",
    )(jnp.asarray(feats), w1t, col(f_b1), col(f_freq), w2t, col(f_b2), f_w3, f_decay.reshape(1, -1))


def _dft_geometry(n):
    big_n = 2 * n
    n2 = big_n // DFT_N1
    nh = n2 // 2 + 1
    return big_n, n2, nh


def _k2_block(nh, limit=13):
    return max(k for k in range(1, limit + 1) if nh % k == 0)


@functools.lru_cache(maxsize=None)
def _stage1_matrix(n, rows_n2):
    big_n, _, nh = _dft_geometry(n)
    groups = DFT_N1 // SUBLANES
    out = np.zeros((groups, 2, nh, SUBLANES, rows_n2, SUBLANES), np.float32)
    k2 = np.arange(nh, dtype=np.float64)[:, None]
    n2v = np.arange(rows_n2, dtype=np.float64)[None, :]
    for g in range(groups):
        for j in range(SUBLANES):
            t = SUBLANES * g + j + DFT_N1 * n2v
            ang = 2 * math.pi * np.mod(k2 * t, big_n) / big_n
            out[g, 0, :, j, :, j] = np.cos(ang)
            out[g, 1, :, j, :, j] = -np.sin(ang)
    return out.reshape(groups, 2 * nh * SUBLANES, rows_n2 * SUBLANES).astype(BF16)


@functools.lru_cache(maxsize=None)
def _stage2_matrices():
    idx = np.arange(DFT_N1, dtype=np.float64)
    ang = 2 * math.pi * np.mod(np.outer(idx, idx), DFT_N1) / DFT_N1
    c, s = np.cos(ang), np.sin(ang)
    fwd = np.block([[c, s], [-s, c]])
    inv = np.block([[c, -s], [s, c]])
    return fwd.astype(BF16), inv.astype(BF16)


@functools.lru_cache(maxsize=None)
def _stage4_matrix(n):
    big_n, n2c, nh = _dft_geometry(n)
    groups = DFT_N1 // SUBLANES
    rows_n2 = n2c // 2
    out = np.zeros((groups, rows_n2, SUBLANES, 2, nh, SUBLANES), np.float32)
    k2 = np.arange(nh, dtype=np.float64)[None, :]
    wgt = np.where(k2 <= n2c // 2, 2.0, 0.0)
    wgt[0, 0] = 1.0
    wgt[0, n2c // 2] = 1.0
    n2v = np.arange(rows_n2, dtype=np.float64)[:, None]
    for g in range(groups):
        for j in range(SUBLANES):
            t = SUBLANES * g + j + DFT_N1 * n2v
            ang = 2 * math.pi * np.mod(k2 * t, big_n) / big_n
            out[g, :, j, 0, :, j] = wgt * np.cos(ang) / big_n
            out[g, :, j, 1, :, j] = -wgt * np.sin(ang) / big_n
    return out.reshape(groups, rows_n2 * SUBLANES, 2 * nh * SUBLANES).astype(BF16)


def _dft_stage1_kernel(x_ref, l_ref, o_ref, *, gsteps):
    nb, _, ct = x_ref.shape
    rows = o_ref.shape[0]
    xin = x_ref[...].astype(F32)
    outs = []
    for gi in range(gsteps):
        sl = slice(SUBLANES * gi, SUBLANES * (gi + 1))
        xv = xin[:, sl, :].reshape(nb * SUBLANES, ct).astype(BF16)
        outs.append(_dot(l_ref[gi], xv).reshape(rows, SUBLANES, ct))
    o_ref[...] = jnp.concatenate(outs, axis=1).astype(o_ref.dtype)


def _dft_stage1(x3, n, gsteps, ct):
    rows_n2, _, c = x3.shape
    _, _, nh = _dft_geometry(n)
    mat = _stage1_matrix(n, rows_n2)
    groups = mat.shape[0]
    ct = min(ct, c)
    gw = SUBLANES * gsteps
    vm = (2 * _nbytes((rows_n2, gw, ct), F32) + 2 * _nbytes((gsteps,) + mat.shape[1:], BF16)
          + 2 * _nbytes((2 * nh, gw, ct), F32) + 3 * _nbytes((2 * nh * SUBLANES, ct), F32) + (4 << 20))
    return pl.pallas_call(
        functools.partial(_dft_stage1_kernel, gsteps=gsteps),
        grid=(groups // gsteps, c // ct),
        in_specs=[pl.BlockSpec((rows_n2, gw, ct), lambda g, j: (0, g, j)),
                  pl.BlockSpec((gsteps,) + mat.shape[1:], lambda g, j: (g, 0, 0))],
        out_specs=pl.BlockSpec((2 * nh, gw, ct), lambda g, j: (0, g, j)),
        out_shape=jax.ShapeDtypeStruct((2 * nh, DFT_N1, c), BF16),
        compiler_params=_cparams(("parallel", "parallel"), vm),
        name="dft_stage_n2",
    )(x3, mat)


def _stack_parts(a_ref, kb):
    cols = [jnp.concatenate([a_ref[0, k], a_ref[1, k]], axis=0) for k in range(kb)]
    return jnp.concatenate(cols, axis=1).astype(BF16)


def _store_parts(o_ref, x, kb):
    ct = o_ref.shape[-1]
    for k in range(kb):
        o_ref[0, k] = x[:DFT_N1, k * ct:(k + 1) * ct].astype(o_ref.dtype)
        o_ref[1, k] = x[DFT_N1:, k * ct:(k + 1) * ct].astype(o_ref.dtype)


def _spectral_product_kernel(a_ref, fa_ref, l2_ref, l3_ref, o_ref, *, kb):
    ct = a_ref.shape[-1]
    x = _dot(l2_ref[...], _stack_parts(a_ref, kb))
    f = _dot(l2_ref[...], _stack_parts(fa_ref, kb))
    ys = []
    for k in range(kb):
        cols = slice(k * ct, (k + 1) * ct)
        xr, xi = x[:DFT_N1, cols], x[DFT_N1:, cols]
        kr, ki = f[:DFT_N1, cols], f[DFT_N1:, cols]
        ys.append(jnp.concatenate([xr * kr - xi * ki, xr * ki + xi * kr], axis=0))
    _store_parts(o_ref, _dot(l3_ref[...], jnp.concatenate(ys, axis=1).astype(BF16)), kb)


def _dft_stage2(a4, fa4, kb, ct):
    _, nh, _, c = a4.shape
    fwd, inv = _stage2_matrices()
    ct = min(ct, c)
    blk = pl.BlockSpec((2, kb, DFT_N1, ct), lambda g, j: (0, g, 0, j))
    mblk = pl.BlockSpec((2 * DFT_N1, 2 * DFT_N1), lambda g, j: (0, 0))
    vm = 6 * _nbytes((2, kb, DFT_N1, ct), BF16) + 8 * _nbytes((2 * DFT_N1, kb * ct), F32) + (4 << 20)
    return pl.pallas_call(
        functools.partial(_spectral_product_kernel, kb=kb),
        grid=(nh // kb, c // ct),
        in_specs=[blk, blk, mblk, mblk],
        out_specs=blk,
        out_shape=jax.ShapeDtypeStruct(a4.shape, BF16),
        compiler_params=_cparams(("parallel", "parallel"), vm),
        name="dft_stage_n1_product",
    )(a4, fa4, fwd, inv)


def _dft_stage4_kernel(v_ref, l_ref, x0_ref, usk_ref, o_ref, *, gsteps):
    rows, _, ct = v_ref.shape
    nb = o_ref.shape[0]
    vin = v_ref[...].astype(F32)
    ys = []
    for gi in range(gsteps):
        sl = slice(SUBLANES * gi, SUBLANES * (gi + 1))
        vv = vin[:, sl, :].reshape(rows * SUBLANES, ct).astype(BF16)
        ys.append(_dot(l_ref[gi], vv).reshape(nb, SUBLANES, ct))
    y = jnp.concatenate(ys, axis=1)
    o_ref[...] = (x0_ref[...].astype(F32) * y + usk_ref[...].astype(F32)).astype(o_ref.dtype)


def _dft_stage4(v3, x03, usk3, n, gsteps, ct):
    rows, _, c = v3.shape
    mat = _stage4_matrix(n)
    groups = mat.shape[0]
    nb = x03.shape[0]
    ct = min(ct, c)
    gw = SUBLANES * gsteps
    vm = (2 * _nbytes((rows, gw, ct), F32) + 2 * _nbytes((gsteps,) + mat.shape[1:], BF16)
          + 6 * _nbytes((nb, gw, ct), F32) + 3 * _nbytes((rows * SUBLANES, ct), F32) + (4 << 20))
    oblk = pl.BlockSpec((nb, gw, ct), lambda g, j: (0, g, j))
    return pl.pallas_call(
        functools.partial(_dft_stage4_kernel, gsteps=gsteps),
        grid=(groups // gsteps, c // ct),
        in_specs=[pl.BlockSpec((rows, gw, ct), lambda g, j: (0, g, j)),
                  pl.BlockSpec((gsteps,) + mat.shape[1:], lambda g, j: (g, 0, 0)),
                  oblk, oblk],
        out_specs=oblk,
        out_shape=jax.ShapeDtypeStruct((nb, DFT_N1, c), BF16),
        compiler_params=_cparams(("parallel", "parallel"), vm),
        name="dft_stage_k2_inverse",
    )(v3, mat, x03, usk3)


def _long_conv_gate(u, x0, usk, filt2, n):
    c = u.shape[1]
    _, _, nh = _dft_geometry(n)
    as3 = lambda a: a.reshape(a.shape[0] // DFT_N1, DFT_N1, c)
    fa = _dft_stage1(as3(filt2), n, gsteps=2, ct=512)
    a = _dft_stage1(as3(u), n, gsteps=2, ct=512)
    v = _dft_stage2(a.reshape(2, nh, DFT_N1, c), fa.reshape(2, nh, DFT_N1, c), kb=_k2_block(nh), ct=256)
    y = _dft_stage4(v.reshape(2 * nh, DFT_N1, c), as3(x0), as3(usk), n, gsteps=2, ct=512)
    return y.reshape(n, c)


@functools.lru_cache(maxsize=None)
def _dense_dft_matrices(n):
    big_n = 2 * n
    nf = n + 1
    nfp = -(-nf // SUBLANES) * SUBLANES
    k = np.arange(nf, dtype=np.float64)[:, None]
    t = np.arange(big_n, dtype=np.float64)[None, :]
    ang = 2 * math.pi * np.mod(k * t, big_n) / big_n
    fwd = np.zeros((2, nfp, big_n), np.float32)
    fwd[0, :nf] = np.cos(ang)
    fwd[1, :nf] = -np.sin(ang)
    wgt = np.full((nf, 1), 2.0)
    wgt[0] = 1.0
    wgt[n] = 1.0
    inv = np.zeros((2, nfp, n), np.float32)
    inv[0, :nf] = (wgt * np.cos(ang) / big_n)[:, :n]
    inv[1, :nf] = (-wgt * np.sin(ang) / big_n)[:, :n]
    fwd = fwd.reshape(2 * nfp, big_n)
    inv = inv.reshape(2 * nfp, n).T
    return (np.ascontiguousarray(fwd[:, :n]).astype(BF16), fwd.astype(BF16),
            np.ascontiguousarray(inv).astype(BF16))


def _short_conv_gate_kernel(u_ref, f_ref, x0_ref, usk_ref, lu_ref, lf_ref, li_ref, o_ref):
    half = lu_ref.shape[0] // 2
    xs = _dot(lu_ref[...], u_ref[...].astype(BF16))
    ks = _dot(lf_ref[...], f_ref[...].astype(BF16))
    xr, xi, kr, ki = xs[:half], xs[half:], ks[:half], ks[half:]
    y = jnp.concatenate([xr * kr - xi * ki, xr * ki + xi * kr], axis=0).astype(BF16)
    conv = _dot(li_ref[...], y)
    o_ref[...] = (x0_ref[...].astype(F32) * conv + usk_ref[...].astype(F32)).astype(o_ref.dtype)


def _long_conv_gate_short(u, x0, usk, filt2, n):
    c = u.shape[1]
    lu, lf, li = _dense_dft_matrices(n)
    ct = 256
    blk = pl.BlockSpec((n, ct), lambda j: (0, j))
    full = lambda a: pl.BlockSpec(a.shape, lambda j: (0, 0))
    vm = 32 << 20
    return pl.pallas_call(
        _short_conv_gate_kernel,
        grid=(c // ct,),
        in_specs=[blk, pl.BlockSpec((2 * n, ct), lambda j: (0, j)), blk, blk, full(lu), full(lf), full(li)],
        out_specs=blk,
        out_shape=jax.ShapeDtypeStruct((n, c), BF16),
        compiler_params=_cparams(("parallel",), vm),
        name="context_long_conv",
    )(u, filt2, x0, usk, lu, lf, li)


@functools.lru_cache(maxsize=None)
def _rope_tables(n):
    t = np.arange(n)
    row = (t // GRID_W).astype(np.float64)
    col = (t % GRID_W).astype(np.float64)
    axis_dim = HEAD_DIM // 2
    inv = 1.0 / (ROPE_THETA ** (np.arange(0, axis_dim, 2, dtype=np.float64) / axis_dim))
    ar, ac = row[:, None] * inv, col[:, None] * inv
    cos = np.concatenate([np.cos(ar), np.cos(ar), np.cos(ac), np.cos(ac)], axis=-1)
    sin = np.concatenate([-np.sin(ar), np.sin(ar), -np.sin(ac), np.sin(ac)], axis=-1)
    return cos.astype(np.float32), sin.astype(np.float32)


TM_PROJ = 512
TM_IN = 1024
TN_IN = 768
TM_FFN = 512
TF_FFN = 1024
TQ_ATTN = 512
TK_ATTN = 2048


def kernel(x, c, ctx, c_ctx, mod_w, mod_b, norm_mix_g, norm_ffn_g, ffn_w1, ffn_w2, ev_w_in, ev_q_norm, ev_k_norm, ev_conv_w, ev_w_out, od_w_in, od_conv_w, od_conv_b, od_f_w1, od_f_b1, od_f_freq, od_f_w2, od_f_b2, od_f_w3, od_f_decay, od_skip, od_w_out, final_g):
    batch, n, d = x.shape
    assert batch == 1
    n_ctx = ctx.shape[1]
    depth = mod_w.shape[0]
    last_ctx_read = ((depth - 1) // 2) * 2
    attn_w = N_Q_HEADS * HEAD_DIM
    kv_w = N_KV_HEADS * HEAD_DIM

    xs = x[0]
    xc = ctx[0]
    cond = jnp.zeros((SUBLANES, d), F32).at[0].set(c[0]).at[1].set(c_ctx)
    modv = _mod_vectors(cond, mod_w, mod_b)
    rope = _rope_tables(n)

    w1_all, w2_all = ffn_w1.astype(BF16), ffn_w2.astype(BF16)
    ev_in, ev_out = ev_w_in.astype(BF16), ev_w_out.astype(BF16)
    od_in, od_out = od_w_in.astype(BF16), od_w_out.astype(BF16)
    g_mix = norm_mix_g.reshape(depth, 1, d)
    g_ffn = norm_ffn_g.reshape(depth, 1, d)
    kv_block = attn_w // (2 * kv_w)
    conv_col = attn_w + 2 * kv_w

    for l in range(depth):
        ctx_full = l < last_ctx_read
        if l % 2 == 0:
            e = l // 2
            p_l = _modmm(xs, g_mix, modv, l, 0, 0, ev_in, e, TM_IN, TN_IN)
            q_l, k_l, v_l = _qkv_prep(p_l, 0, kv_block, ev_q_norm[e], ev_k_norm[e], rope, TM_PROJ)
            p_c = _modmm(xc, g_mix, modv, l, 1, 0, ev_in, e, TM_IN, TN_IN)
            q_c, k_c, v_c = _qkv_prep(p_c, 0, kv_block, ev_q_norm[e], ev_k_norm[e], None, TM_PROJ)
            att_l = _attention(q_l, k_l, v_l, k_c, v_c, TQ_ATTN, TK_ATTN)
            conv_l = _short_conv_mixer(p_l, conv_col, ev_conv_w[e])
            if ctx_full:
                att_c = _attention(q_c, k_c, v_c, None, None, TQ_ATTN, TK_ATTN)
                conv_c = _short_conv_mixer(p_c, conv_col, ev_conv_w[e])
                xc = _outproj([att_c, conv_c], ev_out, e, xc, modv, l, 1, 2, TM_PROJ)
            xs = _outproj([att_l, conv_l], ev_out, e, xs, modv, l, 0, 2, TM_PROJ)
        else:
            o = l // 2
            fargs = (od_f_w1[o], od_f_b1[o], od_f_freq[o], od_f_w2[o], od_f_b2[o], od_f_w3[o], od_f_decay[o])
            p_l = _modmm(xs, g_mix, modv, l, 0, 0, od_in, o, TM_IN, TN_IN, out_dtype=BF16)
            u, x0, usk = _hyena_prep(p_l, od_conv_w[o], od_conv_b[o], od_skip[o])
            filt2 = _hyena_filter(n, d, *fargs, tr=1024, tc=1024)
            y = _long_conv_gate(u, x0, usk, filt2, n)
            xs = _outproj([y], od_out, o, xs, modv, l, 0, 2, TM_PROJ)
            if ctx_full:
                p_c = _modmm(xc, g_mix, modv, l, 1, 0, od_in, o, TM_IN, TN_IN, out_dtype=BF16)
                u, x0, usk = _hyena_prep(p_c, od_conv_w[o], od_conv_b[o], od_skip[o])
                filt2 = _hyena_filter(n_ctx, d, *fargs, tr=1024, tc=1024)
                y = _long_conv_gate_short(u, x0, usk, filt2, n_ctx)
                xc = _outproj([y], od_out, o, xc, modv, l, 1, 2, TM_PROJ)
        last = l == depth - 1
        xs = _ffn(xs, g_ffn, modv, l, 0, w1_all, w2_all, TM_FFN, TF_FFN, final_gain=final_g if last else None)
        if ctx_full:
            xc = _ffn(xc, g_ffn, modv, l, 1, w1_all, w2_all, TM_FFN, TF_FFN)

    return xs[None]
```

```python
import functools
import math

import numpy as np
import jax
import jax.numpy as jnp
from jax import lax
from jax.experimental import pallas as pl
from jax.experimental.pallas import tpu as pltpu

F32 = jnp.float32
BF16 = jnp.bfloat16

HEAD_DIM = 128
N_Q_HEADS = 8
N_KV_HEADS = 2
Q_PER_KV = N_Q_HEADS // N_KV_HEADS
GRID_W = 64
ROPE_THETA = 10000.0
FILTER_BANDS = 16
FILTER_MOD_SHIFT = 0.05
EPS = 1e-6

LANES = 128
SUBLANES = 8
VMEM_BUDGET = 56 * 1024 * 1024
DFT_N1 = 128
ONES_ROWS = 16


def _cparams(sem, vmem_bytes):
    return pltpu.CompilerParams(dimension_semantics=sem,
                                vmem_limit_bytes=int(min(max(vmem_bytes, 16 << 20), VMEM_BUDGET)))


def _nbytes(shape, dtype):
    return int(np.prod(shape)) * jnp.dtype(dtype).itemsize


def _dot(a, b):
    return jnp.dot(a, b, preferred_element_type=F32)


def _rms_rows(x):
    return x * lax.rsqrt(jnp.mean(x * x, axis=-1, keepdims=True) + EPS)


def _mod_kernel(c_ref, w_ref, b_ref, o_ref):
    c = c_ref[...]
    s = c * (1.0 / (1.0 + jnp.exp(-c)))
    o_ref[0] = _dot(s.astype(BF16), w_ref[0].astype(BF16)) + b_ref[0]


def _mod_vectors(cond, mod_w, mod_b):
    depth, d, n6 = mod_w.shape
    tn = 1536
    rows = cond.shape[0]
    vm = 2 * _nbytes((d, tn), F32) + _nbytes((d, tn), BF16) + (4 << 20)
    return pl.pallas_call(
        _mod_kernel,
        grid=(depth, n6 // tn),
        in_specs=[pl.BlockSpec((rows, d), lambda l, j: (0, 0)),
                  pl.BlockSpec((1, d, tn), lambda l, j: (l, 0, j)),
                  pl.BlockSpec((1, 1, tn), lambda l, j: (l, 0, j))],
        out_specs=pl.BlockSpec((1, rows, tn), lambda l, j: (l, 0, j)),
        out_shape=jax.ShapeDtypeStruct((depth, rows, n6), F32),
        compiler_params=_cparams(("parallel", "parallel"), vm),
        name="adaln_vectors",
    )(cond, mod_w, mod_b.reshape(depth, 1, n6))


def _modmm_kernel(x_ref, g_ref, sh_ref, sc_ref, w_ref, o_ref, h_ref, *, row):
    @pl.when(pl.program_id(1) == 0)
    def _():
        h = _rms_rows(x_ref[...]) * g_ref[...]
        h = h * (1.0 + sc_ref[row:row + 1, :]) + sh_ref[row:row + 1, :]
        h_ref[...] = h.astype(BF16)

    o_ref[...] = _dot(h_ref[...], w_ref[...]).astype(o_ref.dtype)


def _modmm(x, gains, modv, l, row, col0, w, wi, tm, tn, out_dtype=F32):
    m, d = x.shape
    n = w.shape[2]
    tm = min(tm, m)
    vm = (2 * _nbytes((tm, d), F32) + _nbytes((tm, d), BF16) + 2 * _nbytes((d, tn), BF16)
          + 2 * _nbytes((tm, tn), F32) + 2 * _nbytes((tm, d), F32) + (4 << 20))
    return pl.pallas_call(
        functools.partial(_modmm_kernel, row=row),
        grid=(m // tm, n // tn),
        in_specs=[pl.BlockSpec((tm, d), lambda i, j: (i, 0)),
                  pl.BlockSpec((None, 1, d), lambda i, j: (l, 0, 0)),
                  pl.BlockSpec((None, SUBLANES, d), lambda i, j: (l, 0, col0)),
                  pl.BlockSpec((None, SUBLANES, d), lambda i, j: (l, 0, col0 + 1)),
                  pl.BlockSpec((None, d, tn), lambda i, j: (wi, 0, j))],
        out_specs=pl.BlockSpec((tm, tn), lambda i, j: (i, j)),
        out_shape=jax.ShapeDtypeStruct((m, n), out_dtype),
        scratch_shapes=[pltpu.VMEM((tm, d), BF16)],
        compiler_params=_cparams(("parallel", "arbitrary"), vm),
        name="modulated_projection",
    )(x, gains, modv, modv, w)


def _outproj_kernel(*refs, n_in, row):
    a_refs = refs[:n_in]
    w_refs = refs[n_in:2 * n_in]
    x_ref, gate_ref, g_ref, sh_ref, sc_ref, o_ref, h_ref = refs[2 * n_in:]
    acc = _dot(a_refs[0][...].astype(BF16), w_refs[0][...])
    for a_ref, w_ref in zip(a_refs[1:], w_refs[1:]):
        acc = acc + _dot(a_ref[...].astype(BF16), w_ref[...])
    x_new = x_ref[...] + gate_ref[row:row + 1, :] * acc
    o_ref[...] = x_new
    h = _rms_rows(x_new) * g_ref[...]
    h_ref[...] = (h * (1.0 + sc_ref[row:row + 1, :]) + sh_ref[row:row + 1, :]).astype(BF16)


def _outproj(acts, w, wi, x, modv, l, row, gains, tm):
    m, d = x.shape
    tm = min(tm, m)
    n_in = len(acts)
    in_specs, args = [], []
    for a in acts:
        in_specs.append(pl.BlockSpec((tm, a.shape[1]), lambda i: (i, 0)))
        args.append(a)
    row0 = 0
    for a in acts:
        k = a.shape[1]
        in_specs.append(pl.BlockSpec((None, k, d), functools.partial(lambda i, b: (wi, b, 0), b=row0 // k)))
        args.append(w)
        row0 += k
    modblk = lambda col: pl.BlockSpec((None, SUBLANES, d), lambda i: (l, 0, col))
    in_specs += [pl.BlockSpec((tm, d), lambda i: (i, 0)), modblk(2),
                 pl.BlockSpec((None, 1, d), lambda i: (l, 0, 0)), modblk(3), modblk(4)]
    args += [x, modv, gains, modv, modv]
    vm = (2 * sum(_nbytes((tm, a.shape[1]), a.dtype) for a in acts) + 2 * _nbytes(w.shape[1:], BF16)
          + 8 * _nbytes((tm, d), F32) + (4 << 20))
    rowblk = pl.BlockSpec((tm, d), lambda i: (i, 0))
    return pl.pallas_call(
        functools.partial(_outproj_kernel, n_in=n_in, row=row),
        grid=(m // tm,),
        in_specs=in_specs,
        out_specs=[rowblk, rowblk],
        out_shape=[jax.ShapeDtypeStruct((m, d), F32), jax.ShapeDtypeStruct((m, d), BF16)],
        compiler_params=_cparams(("parallel",), vm),
        name="gated_out_projection",
    )(*args)


def _ffn_kernel(*refs, row, final_norm):
    if final_norm:
        x_ref, h_ref, gate_ref, w1_ref, w2_ref, fg_ref, o_ref = refs
    else:
        x_ref, h_ref, gate_ref, w1_ref, w2_ref, o_ref = refs
    j = pl.program_id(1)

    @pl.when(j == 0)
    def _():
        o_ref[...] = jnp.zeros(o_ref.shape, F32)

    a = jnp.maximum(_dot(h_ref[...], w1_ref[...]), 0.0)
    o_ref[...] += _dot((a * a).astype(BF16), w2_ref[...])

    @pl.when(j == pl.num_programs(1) - 1)
    def _():
        y = x_ref[...] + gate_ref[row:row + 1, :] * o_ref[...]
        if final_norm:
            y = _rms_rows(y) * fg_ref[...]
        o_ref[...] = y


def _ffn(x, h, modv, l, row, w1, w2, tm, tf, final_gain=None):
    m, d = x.shape
    f = w1.shape[2]
    tm = min(tm, m)
    vm = (4 * _nbytes((tm, d), F32) + 2 * _nbytes((tm, d), BF16)
          + 4 * _nbytes((d, tf), BF16) + 3 * _nbytes((tm, tf), F32) + (6 << 20))
    in_specs = [pl.BlockSpec((tm, d), lambda i, j: (i, 0)),
                pl.BlockSpec((tm, d), lambda i, j: (i, 0)),
                pl.BlockSpec((None, SUBLANES, d), lambda i, j: (l, 0, 5)),
                pl.BlockSpec((None, d, tf), lambda i, j: (l, 0, j)),
                pl.BlockSpec((None, tf, d), lambda i, j: (l, j, 0))]
    args = [x, h, modv, w1, w2]
    if final_gain is not None:
        in_specs.append(pl.BlockSpec((1, d), lambda i, j: (0, 0)))
        args.append(final_gain.reshape(1, d))
    return pl.pallas_call(
        functools.partial(_ffn_kernel, row=row, final_norm=final_gain is not None),
        grid=(m // tm, f // tf),
        in_specs=in_specs,
        out_specs=pl.BlockSpec((tm, d), lambda i, j: (i, 0)),
        out_shape=jax.ShapeDtypeStruct((m, d), F32),
        compiler_params=_cparams(("parallel", "arbitrary"), vm),
        name="gated_ffn",
    )(*args)


def _rope(x, cos, sin):
    lane = lax.broadcasted_iota(jnp.int32, x.shape, 1)
    first_half = (lane % (HEAD_DIM // 2)) < (HEAD_DIM // 4)
    partner = jnp.where(first_half,
                        pltpu.roll(x, HEAD_DIM - HEAD_DIM // 4, 1),
                        pltpu.roll(x, HEAD_DIM // 4, 1))
    return x * cos + partner * sin


def _qkv_kernel(*refs, use_rope, q_scale):
    if use_rope:
        pq_ref, pkv_ref, qg_ref, kg_ref, cos_ref, sin_ref, q_ref, k_ref, v_ref = refs
        cos, sin = cos_ref[...], sin_ref[...]
    else:
        pq_ref, pkv_ref, qg_ref, kg_ref, q_ref, k_ref, v_ref = refs
    kv_w = N_KV_HEADS * HEAD_DIM
    for h in range(N_Q_HEADS):
        sl = slice(h * HEAD_DIM, (h + 1) * HEAD_DIM)
        xn = _rms_rows(pq_ref[:, sl]) * qg_ref[...]
        if use_rope:
            xn = _rope(xn, cos, sin)
        q_ref[:, sl] = (xn * q_scale).astype(BF16)
    for h in range(N_KV_HEADS):
        sl = slice(h * HEAD_DIM, (h + 1) * HEAD_DIM)
        xn = _rms_rows(pkv_ref[:, sl]) * kg_ref[...]
        if use_rope:
            xn = _rope(xn, cos, sin)
        k_ref[:, sl] = xn.astype(BF16)
    v_ref[...] = pkv_ref[:, kv_w:2 * kv_w].T.astype(BF16)


def _qkv_prep(p, q_col_block, kv_col_block, q_gain, k_gain, rope, tm):
    m = p.shape[0]
    tm = min(tm, m)
    attn_w = N_Q_HEADS * HEAD_DIM
    kv_w = N_KV_HEADS * HEAD_DIM
    in_specs = [pl.BlockSpec((tm, attn_w), lambda i: (i, q_col_block)),
                pl.BlockSpec((tm, 2 * kv_w), lambda i: (i, kv_col_block)),
                pl.BlockSpec((1, HEAD_DIM), lambda i: (0, 0)),
                pl.BlockSpec((1, HEAD_DIM), lambda i: (0, 0))]
    args = [p, p, q_gain.reshape(1, HEAD_DIM), k_gain.reshape(1, HEAD_DIM)]
    if rope is not None:
        in_specs += [pl.BlockSpec((tm, HEAD_DIM), lambda i: (i, 0))] * 2
        args += list(rope)
    vm = 4 * _nbytes((tm, attn_w + 2 * kv_w), F32) + (8 << 20)
    return pl.pallas_call(
        functools.partial(_qkv_kernel, use_rope=rope is not None, q_scale=HEAD_DIM ** -0.5),
        grid=(m // tm,),
        in_specs=in_specs,
        out_specs=[pl.BlockSpec((tm, attn_w), lambda i: (i, 0)),
                   pl.BlockSpec((tm, kv_w), lambda i: (i, 0)),
                   pl.BlockSpec((kv_w, tm), lambda i: (0, i))],
        out_shape=[jax.ShapeDtypeStruct((m, attn_w), BF16),
                   jax.ShapeDtypeStruct((m, kv_w), BF16),
                   jax.ShapeDtypeStruct((kv_w, m), BF16)],
        compiler_params=_cparams(("parallel",), vm),
        name="qkv_norm_rope",
    )(*args)


def _attn_kernel(*refs, has_extra, tq, n_cast):
    n_in = 5 if has_extra else 3
    cast_in = refs[n_in:n_in + n_cast]
    o_ref = refs[n_in + n_cast]
    cast_out = refs[n_in + n_cast + 1:n_in + 2 * n_cast + 1]
    qs_ref, m_ref, acc_ref = refs[n_in + 2 * n_cast + 1:]
    if has_extra:
        q_ref, k_ref, vt_ref, ke_ref, vte_ref = refs[:n_in]
    else:
        q_ref, k_ref, vt_ref = refs[:n_in]
    kv = pl.program_id(2)

    for src, dst in zip(cast_in, cast_out):
        dst[...] = src[...].astype(BF16)

    def update(k, vt):
        vt1 = jnp.concatenate([vt, jnp.ones((ONES_ROWS, vt.shape[1]), BF16)], axis=0)
        cols = [slice(h * tq, (h + 1) * tq) for h in range(Q_PER_KV)]
        s, p, alpha = {}, {}, {}
        for t in range(Q_PER_KV + 2):
            if t < Q_PER_KV:
                s[t] = lax.dot_general(k, qs_ref[cols[t], :], (((1,), (1,)), ((), ())),
                                       preferred_element_type=F32)
            h = t - 1
            if 0 <= h < Q_PER_KV:
                m_prev = m_ref[:, cols[h]]
                m_new = jnp.maximum(m_prev, jnp.max(s[h], axis=0, keepdims=True))
                alpha[h] = jnp.exp(m_prev - m_new)
                p[h] = jnp.exp(s.pop(h) - m_new).astype(BF16)
                m_ref[:, cols[h]] = m_new
            h = t - 2
            if 0 <= h < Q_PER_KV:
                acc_ref[:, cols[h]] = alpha.pop(h) * acc_ref[:, cols[h]] + _dot(vt1, p.pop(h))

    @pl.when(kv == 0)
    def _():
        for h in range(Q_PER_KV):
            qs_ref[h * tq:(h + 1) * tq, :] = q_ref[:, h * HEAD_DIM:(h + 1) * HEAD_DIM]
        m_ref[...] = jnp.full(m_ref.shape, -jnp.inf, F32)
        acc_ref[...] = jnp.zeros(acc_ref.shape, F32)
        if has_extra:
            update(ke_ref[...], vte_ref[...])

    update(k_ref[...], vt_ref[...])

    @pl.when(kv == pl.num_programs(2) - 1)
    def _():
        o_t = acc_ref[0:HEAD_DIM, :] / acc_ref[HEAD_DIM:HEAD_DIM + 1, :]
        for h in range(Q_PER_KV):
            o_ref[:, h * HEAD_DIM:(h + 1) * HEAD_DIM] = o_t[:, h * tq:(h + 1) * tq].T.astype(o_ref.dtype)


def _attention(q, k, vt, k_extra, vt_extra, tq, tk, cast=()):
    m = q.shape[0]
    s_len = k.shape[0]
    tq = min(tq, m)
    tk = min(tk, s_len)
    gw = Q_PER_KV * HEAD_DIM
    cols = Q_PER_KV * tq
    has_extra = k_extra is not None
    n_i, n_j = m // tq, s_len // tk
    steps = N_KV_HEADS * n_i * n_j
    in_specs = [pl.BlockSpec((tq, gw), lambda g, i, j: (i, g)),
                pl.BlockSpec((tk, HEAD_DIM), lambda g, i, j: (j, g)),
                pl.BlockSpec((HEAD_DIM, tk), lambda g, i, j: (g, j))]
    args = [q, k, vt]
    if has_extra:
        e = k_extra.shape[0]
        in_specs += [pl.BlockSpec((e, HEAD_DIM), lambda g, i, j: (0, g)),
                     pl.BlockSpec((HEAD_DIM, e), lambda g, i, j: (g, 0))]
        args += [k_extra, vt_extra]
    out_specs = [pl.BlockSpec((tq, gw), lambda g, i, j: (i, g))]
    out_shape = [jax.ShapeDtypeStruct((m, N_Q_HEADS * HEAD_DIM), BF16)]
    for w in cast:
        lanes = w.shape[-1]
        rows = w.size // (steps * lanes)
        assert rows * steps * lanes == w.size and rows % (2 * SUBLANES) == 0
        slab = pl.BlockSpec((None, rows, lanes), lambda g, i, j: ((g * n_i + i) * n_j + j, 0, 0))
        in_specs.append(slab)
        args.append(w.reshape(steps, rows, lanes))
        out_specs.append(slab)
        out_shape.append(jax.ShapeDtypeStruct((steps, rows, lanes), BF16))
    acc_rows = HEAD_DIM + ONES_ROWS
    vm = 4 * _nbytes((tk, cols), F32) + 6 * _nbytes((acc_rows, cols), F32) + (8 << 20)
    outs = pl.pallas_call(
        functools.partial(_attn_kernel, has_extra=has_extra, tq=tq, n_cast=len(cast)),
        grid=(N_KV_HEADS, n_i, n_j),
        in_specs=in_specs,
        out_specs=out_specs,
        out_shape=out_shape,
        scratch_shapes=[pltpu.VMEM((cols, HEAD_DIM), BF16),
                        pltpu.VMEM((1, cols), F32),
                        pltpu.VMEM((acc_rows, cols), F32)],
        compiler_params=_cparams(("parallel", "parallel", "arbitrary"), vm),
        name="gqa_flash_attention",
    )(*args)
    return (outs[0],) + tuple(o.reshape(w.shape) for o, w in zip(outs[1:], cast))


def _conv3(z, w_ref):
    n = z.shape[0]
    prev = pltpu.roll(z, 1, 0)
    nxt = pltpu.roll(z, n - 1, 0)
    row = lax.broadcasted_iota(jnp.int32, (SUBLANES, z.shape[1]), 0)
    prev = jnp.concatenate([jnp.where(row == 0, 0.0, prev[:SUBLANES]), prev[SUBLANES:]], axis=0)
    nxt = jnp.concatenate([nxt[:n - SUBLANES], jnp.where(row == SUBLANES - 1, 0.0, nxt[n - SUBLANES:])], axis=0)
    return prev * w_ref[0:1, :] + z * w_ref[1:2, :] + nxt * w_ref[2:3, :]


def _convmix_kernel(u_ref, b_ref, c_ref, w_ref, o_ref):
    o_ref[...] = (b_ref[...] * _conv3(c_ref[...] * u_ref[...], w_ref)).astype(o_ref.dtype)


def _short_conv_mixer(p, col0, conv_w):
    m = p.shape[0]
    c = conv_w.shape[1]
    ct = LANES
    nb = c // ct
    b0 = col0 // ct
    vm = 8 * _nbytes((m, ct), F32) + 8 * _nbytes((m, ct), F32) + (4 << 20)
    return pl.pallas_call(
        _convmix_kernel,
        grid=(nb,),
        in_specs=[pl.BlockSpec((m, ct), lambda j: (0, b0 + j)),
                  pl.BlockSpec((m, ct), lambda j: (0, b0 + nb + j)),
                  pl.BlockSpec((m, ct), lambda j: (0, b0 + 2 * nb + j)),
                  pl.BlockSpec((3, ct), lambda j: (0, j))],
        out_specs=pl.BlockSpec((m, ct), lambda j: (0, j)),
        out_shape=jax.ShapeDtypeStruct((m, c), BF16),
        compiler_params=_cparams(("parallel",), vm),
        name="short_conv_mixer",
    )(p, p, p, conv_w)


def _hyena_prep_kernel(p0_ref, p1_ref, p2_ref, w0_ref, w1_ref, w2_ref, b0_ref, b1_ref, b2_ref,
                       skip_ref, u_ref, x0_ref, usk_ref):
    x0 = _conv3(p0_ref[...].astype(F32), w0_ref) + b0_ref[...]
    x1 = _conv3(p1_ref[...].astype(F32), w1_ref) + b1_ref[...]
    v = _conv3(p2_ref[...].astype(F32), w2_ref) + b2_ref[...]
    u = x1 * v
    u_ref[...] = u.astype(u_ref.dtype)
    x0_ref[...] = x0.astype(x0_ref.dtype)
    usk_ref[...] = (x0 * (u * skip_ref[...])).astype(usk_ref.dtype)


def _hyena_prep(p, conv_w, conv_b, skip):
    m = p.shape[0]
    c = skip.shape[0]
    ct = LANES
    nb = c // ct
    vm = 12 * _nbytes((m, ct), F32) + 10 * _nbytes((m, ct), F32) + (4 << 20)
    blk = lambda off: pl.BlockSpec((m, ct), functools.partial(lambda j, o: (0, o + j), o=off))
    wblk = lambda off: pl.BlockSpec((3, ct), functools.partial(lambda j, o: (0, o + j), o=off))
    bblk = lambda off: pl.BlockSpec((1, ct), functools.partial(lambda j, o: (0, o + j), o=off))
    out = jax.ShapeDtypeStruct((m, c), BF16)
    return pl.pallas_call(
        _hyena_prep_kernel,
        grid=(nb,),
        in_specs=[blk(0), blk(nb), blk(2 * nb), wblk(0), wblk(nb), wblk(2 * nb),
                  bblk(0), bblk(nb), bblk(2 * nb), pl.BlockSpec((1, ct), lambda j: (0, j))],
        out_specs=[pl.BlockSpec((m, ct), lambda j: (0, j))] * 3,
        out_shape=[out, out, out],
        compiler_params=_cparams(("parallel",), vm),
        name="hyena_short_conv_gate",
    )(p, p, p, conv_w, conv_w, conv_w, conv_b.reshape(1, -1), conv_b.reshape(1, -1),
      conv_b.reshape(1, -1), skip.reshape(1, c))


def _filter_feats_t(n, order):
    t = np.asarray(order, np.float64)
    t_norm = t / max(n - 1, 1)
    bands = np.arange(1, FILTER_BANDS + 1, dtype=np.float64)
    ang = (2 * math.pi / n) * bands[:, None] * t[None, :]
    feats = np.concatenate([t_norm[None, :], np.cos(ang), np.sin(ang)], axis=0)
    out = np.zeros((LANES, len(t)), np.float32)
    out[:feats.shape[0]] = feats
    return out


def _split_bf16(a):
    hi = a.astype(BF16).astype(F32)
    return hi, a - hi


def _dot_split(a, b):
    a_hi, a_lo = _split_bf16(a)
    b_hi, b_lo = _split_bf16(b)
    lhs = jnp.concatenate([a_hi, a_hi, a_lo], axis=1).astype(BF16)
    rhs = jnp.concatenate([b_hi, b_lo, b_hi], axis=0).astype(BF16)
    return _dot(lhs, rhs)


def _filter_kernel(f_ref, w1_ref, b1_ref, fr_ref, w2_ref, b2_ref, w3_ref, dec_ref, o_ref, hs_ref, tn_ref,
                   *, n):
    tr = tn_ref.shape[0]
    hid = w3_ref.shape[0]

    @pl.when(pl.program_id(2) == 0)
    def _():
        fr = fr_ref[...]
        h = jnp.sin(fr * (_dot_split(w1_ref[...], f_ref[...]) + b1_ref[...]))
        h = jnp.concatenate([h, jnp.zeros((LANES - hid, tr), F32)], axis=0)
        h = jnp.sin(fr * (_dot_split(w2_ref[...], h) + b2_ref[...]))
        hi, lo = _split_bf16(h)
        pad = jnp.zeros((2 * LANES - 3 * hid, tr), F32)
        hs_ref[...] = jnp.concatenate([hi, hi, lo, pad], axis=0).T.astype(BF16)
        pos = pl.program_id(1) * tr + lax.broadcasted_iota(jnp.int32, (tr, LANES), 0)
        pos = jnp.where(pl.program_id(0) == 0, pos, n - 1 - pos)
        tn_ref[...] = pos.astype(F32) / float(max(n - 1, 1))

    w_hi, w_lo = _split_bf16(w3_ref[...])
    pad = jnp.zeros((2 * LANES - 3 * hid, w_hi.shape[1]), F32)
    rhs = jnp.concatenate([w_hi, w_lo, w_hi, pad], axis=0).astype(BF16)
    hw = _dot(hs_ref[...], rhs)
    t_norm = jnp.concatenate([tn_ref[...]] * (hw.shape[1] // LANES), axis=1)
    window = jnp.exp(-t_norm * jnp.abs(dec_ref[...])) + FILTER_MOD_SHIFT
    o_ref[...] = (hw * window).astype(o_ref.dtype)


def _hyena_filter(n, ch, f_w1, f_b1, f_freq, f_w2, f_b2, f_w3, f_decay, tr, tc):
    hid = f_w1.shape[1]
    assert 3 * hid <= 2 * LANES and hid <= LANES
    feats = np.stack([_filter_feats_t(n, np.arange(n)), _filter_feats_t(n, np.arange(n)[::-1])])
    w1t = jnp.zeros((hid, LANES), F32).at[:, :f_w1.shape[0]].set(f_w1.T)
    w2t = jnp.zeros((hid, LANES), F32).at[:, :hid].set(f_w2.T)
    tr = min(tr, n)
    nrb = n // tr
    ncb = ch // tc
    vm = 6 * _nbytes((tr, tc), F32) + 4 * _nbytes((LANES, tr), F32) + (8 << 20)
    col = lambda a: a.reshape(hid, 1)
    small = lambda shape: pl.BlockSpec(shape, lambda s, i, j: (0, 0))
    return pl.pallas_call(
        functools.partial(_filter_kernel, n=n),
        grid=(2, nrb, ncb),
        in_specs=[pl.BlockSpec((None, LANES, tr), lambda s, i, j: (s, 0, i)),
                  small((hid, LANES)), small((hid, 1)), small((hid, 1)),
                  small((hid, LANES)), small((hid, 1)),
                  pl.BlockSpec((hid, tc), lambda s, i, j: (0, s * ncb + j)),
                  pl.BlockSpec((1, tc), lambda s, i, j: (0, s * ncb + j))],
        out_specs=pl.BlockSpec((tr, tc), lambda s, i, j: (s * nrb + i, j)),
        out_shape=jax.ShapeDtypeStruct((2 * n, ch), BF16),
        scratch_shapes=[pltpu.VMEM((tr, 2 * LANES), BF16), pltpu.VMEM((tr, LANES), F32)],
        compiler_params=_cparams(("parallel", "parallel", "arbitrary"), vm),
        name="hyena_filter",
    )(jnp.asarray(feats), w1t, col(f_b1), col(f_freq), w2t, col(f_b2), f_w3, f_decay.reshape(1, -1))


def _dft_geometry(n):
    big_n = 2 * n
    n2 = big_n // DFT_N1
    nh = n2 // 2 + 1
    return big_n, n2, nh


def _k2_block(nh, limit=13):
    return max(k for k in range(1, limit + 1) if nh % k == 0)


@functools.lru_cache(maxsize=None)
def _stage1_matrix(n, rows_n2):
    big_n, _, nh = _dft_geometry(n)
    groups = DFT_N1 // SUBLANES
    out = np.zeros((groups, 2, nh, SUBLANES, rows_n2, SUBLANES), np.float32)
    k2 = np.arange(nh, dtype=np.float64)[:, None]
    n2v = np.arange(rows_n2, dtype=np.float64)[None, :]
    for g in range(groups):
        for j in range(SUBLANES):
            t = SUBLANES * g + j + DFT_N1 * n2v
            ang = 2 * math.pi * np.mod(k2 * t, big_n) / big_n
            out[g, 0, :, j, :, j] = np.cos(ang)
            out[g, 1, :, j, :, j] = -np.sin(ang)
    return out.reshape(groups, 2 * nh * SUBLANES, rows_n2 * SUBLANES).astype(BF16)


@functools.lru_cache(maxsize=None)
def _stage2_matrices():
    idx = np.arange(DFT_N1, dtype=np.float64)
    ang = 2 * math.pi * np.mod(np.outer(idx, idx), DFT_N1) / DFT_N1
    c, s = np.cos(ang), np.sin(ang)
    fwd = np.block([[c, s], [-s, c]])
    inv = np.block([[c, -s], [s, c]])
    return fwd.astype(BF16), inv.astype(BF16)


@functools.lru_cache(maxsize=None)
def _stage4_matrix(n):
    big_n, n2c, nh = _dft_geometry(n)
    groups = DFT_N1 // SUBLANES
    rows_n2 = n2c // 2
    out = np.zeros((groups, rows_n2, SUBLANES, 2, nh, SUBLANES), np.float32)
    k2 = np.arange(nh, dtype=np.float64)[None, :]
    wgt = np.where(k2 <= n2c // 2, 2.0, 0.0)
    wgt[0, 0] = 1.0
    wgt[0, n2c // 2] = 1.0
    n2v = np.arange(rows_n2, dtype=np.float64)[:, None]
    for g in range(groups):
        for j in range(SUBLANES):
            t = SUBLANES * g + j + DFT_N1 * n2v
            ang = 2 * math.pi * np.mod(k2 * t, big_n) / big_n
            out[g, :, j, 0, :, j] = wgt * np.cos(ang) / big_n
            out[g, :, j, 1, :, j] = -wgt * np.sin(ang) / big_n
    return out.reshape(groups, rows_n2 * SUBLANES, 2 * nh * SUBLANES).astype(BF16)


def _dft_stage1_kernel(x_ref, l_ref, o_ref, *, gsteps):
    nb, _, ct = x_ref.shape
    rows = o_ref.shape[0]
    xin = x_ref[...].astype(F32)
    outs = []
    for gi in range(gsteps):
        sl = slice(SUBLANES * gi, SUBLANES * (gi + 1))
        xv = xin[:, sl, :].reshape(nb * SUBLANES, ct).astype(BF16)
        outs.append(_dot(l_ref[gi], xv).reshape(rows, SUBLANES, ct))
    o_ref[...] = jnp.concatenate(outs, axis=1).astype(o_ref.dtype)


def _dft_stage1(x3, n, gsteps, ct):
    rows_n2, _, c = x3.shape
    _, _, nh = _dft_geometry(n)
    mat = _stage1_matrix(n, rows_n2)
    groups = mat.shape[0]
    ct = min(ct, c)
    gw = SUBLANES * gsteps
    vm = (2 * _nbytes((rows_n2, gw, ct), F32) + 2 * _nbytes((gsteps,) + mat.shape[1:], BF16)
          + 2 * _nbytes((2 * nh, gw, ct), F32) + 3 * _nbytes((2 * nh * SUBLANES, ct), F32) + (4 << 20))
    return pl.pallas_call(
        functools.partial(_dft_stage1_kernel, gsteps=gsteps),
        grid=(groups // gsteps, c // ct),
        in_specs=[pl.BlockSpec((rows_n2, gw, ct), lambda g, j: (0, g, j)),
                  pl.BlockSpec((gsteps,) + mat.shape[1:], lambda g, j: (g, 0, 0))],
        out_specs=pl.BlockSpec((2 * nh, gw, ct), lambda g, j: (0, g, j)),
        out_shape=jax.ShapeDtypeStruct((2 * nh, DFT_N1, c), BF16),
        compiler_params=_cparams(("parallel", "parallel"), vm),
        name="dft_stage_n2",
    )(x3, mat)


def _stack_parts(a_ref, kb):
    cols = [jnp.concatenate([a_ref[0, k], a_ref[1, k]], axis=0) for k in range(kb)]
    return jnp.concatenate(cols, axis=1).astype(BF16)


def _store_parts(o_ref, x, kb):
    ct = o_ref.shape[-1]
    for k in range(kb):
        o_ref[0, k] = x[:DFT_N1, k * ct:(k + 1) * ct].astype(o_ref.dtype)
        o_ref[1, k] = x[DFT_N1:, k * ct:(k + 1) * ct].astype(o_ref.dtype)


def _spectral_product_kernel(a_ref, fa_ref, l2_ref, l3_ref, o_ref, *, kb):
    ct = a_ref.shape[-1]
    x = _dot(l2_ref[...], _stack_parts(a_ref, kb))
    f = _dot(l2_ref[...], _stack_parts(fa_ref, kb))
    ys = []
    for k in range(kb):
        cols = slice(k * ct, (k + 1) * ct)
        xr, xi = x[:DFT_N1, cols], x[DFT_N1:, cols]
        kr, ki = f[:DFT_N1, cols], f[DFT_N1:, cols]
        ys.append(jnp.concatenate([xr * kr - xi * ki, xr * ki + xi * kr], axis=0))
    _store_parts(o_ref, _dot(l3_ref[...], jnp.concatenate(ys, axis=1).astype(BF16)), kb)


def _dft_stage2(a4, fa4, kb, ct):
    _, nh, _, c = a4.shape
    fwd, inv = _stage2_matrices()
    ct = min(ct, c)
    blk = pl.BlockSpec((2, kb, DFT_N1, ct), lambda g, j: (0, g, 0, j))
    mblk = pl.BlockSpec((2 * DFT_N1, 2 * DFT_N1), lambda g, j: (0, 0))
    vm = 6 * _nbytes((2, kb, DFT_N1, ct), BF16) + 8 * _nbytes((2 * DFT_N1, kb * ct), F32) + (4 << 20)
    return pl.pallas_call(
        functools.partial(_spectral_product_kernel, kb=kb),
        grid=(nh // kb, c // ct),
        in_specs=[blk, blk, mblk, mblk],
        out_specs=blk,
        out_shape=jax.ShapeDtypeStruct(a4.shape, BF16),
        compiler_params=_cparams(("parallel", "parallel"), vm),
        name="dft_stage_n1_product",
    )(a4, fa4, fwd, inv)


def _dft_stage4_kernel(v_ref, l_ref, x0_ref, usk_ref, o_ref, *, gsteps):
    rows, _, ct = v_ref.shape
    nb = o_ref.shape[0]
    vin = v_ref[...].astype(F32)
    ys = []
    for gi in range(gsteps):
        sl = slice(SUBLANES * gi, SUBLANES * (gi + 1))
        vv = vin[:, sl, :].reshape(rows * SUBLANES, ct).astype(BF16)
        ys.append(_dot(l_ref[gi], vv).reshape(nb, SUBLANES, ct))
    y = jnp.concatenate(ys, axis=1)
    o_ref[...] = (x0_ref[...].astype(F32) * y + usk_ref[...].astype(F32)).astype(o_ref.dtype)


def _dft_stage4(v3, x03, usk3, n, gsteps, ct):
    rows, _, c = v3.shape
    mat = _stage4_matrix(n)
    groups = mat.shape[0]
    nb = x03.shape[0]
    ct = min(ct, c)
    gw = SUBLANES * gsteps
    vm = (2 * _nbytes((rows, gw, ct), F32) + 2 * _nbytes((gsteps,) + mat.shape[1:], BF16)
          + 6 * _nbytes((nb, gw, ct), F32) + 3 * _nbytes((rows * SUBLANES, ct), F32) + (4 << 20))
    oblk = pl.BlockSpec((nb, gw, ct), lambda g, j: (0, g, j))
    return pl.pallas_call(
        functools.partial(_dft_stage4_kernel, gsteps=gsteps),
        grid=(groups // gsteps, c // ct),
        in_specs=[pl.BlockSpec((rows, gw, ct), lambda g, j: (0, g, j)),
                  pl.BlockSpec((gsteps,) + mat.shape[1:], lambda g, j: (g, 0, 0)),
                  oblk, oblk],
        out_specs=oblk,
        out_shape=jax.ShapeDtypeStruct((nb, DFT_N1, c), BF16),
        compiler_params=_cparams(("parallel", "parallel"), vm),
        name="dft_stage_k2_inverse",
    )(v3, mat, x03, usk3)


def _long_conv_gate(u, x0, usk, filt2, n):
    c = u.shape[1]
    _, _, nh = _dft_geometry(n)
    as3 = lambda a: a.reshape(a.shape[0] // DFT_N1, DFT_N1, c)
    fa = _dft_stage1(as3(filt2), n, gsteps=2, ct=512)
    a = _dft_stage1(as3(u), n, gsteps=2, ct=512)
    v = _dft_stage2(a.reshape(2, nh, DFT_N1, c), fa.reshape(2, nh, DFT_N1, c), kb=_k2_block(nh), ct=256)
    y = _dft_stage4(v.reshape(2 * nh, DFT_N1, c), as3(x0), as3(usk), n, gsteps=2, ct=512)
    return y.reshape(n, c)


@functools.lru_cache(maxsize=None)
def _dense_dft_matrices(n):
    big_n = 2 * n
    nf = n + 1
    nfp = -(-nf // SUBLANES) * SUBLANES
    k = np.arange(nf, dtype=np.float64)[:, None]
    t = np.arange(big_n, dtype=np.float64)[None, :]
    ang = 2 * math.pi * np.mod(k * t, big_n) / big_n
    fwd = np.zeros((2, nfp, big_n), np.float32)
    fwd[0, :nf] = np.cos(ang)
    fwd[1, :nf] = -np.sin(ang)
    wgt = np.full((nf, 1), 2.0)
    wgt[0] = 1.0
    wgt[n] = 1.0
    inv = np.zeros((2, nfp, n), np.float32)
    inv[0, :nf] = (wgt * np.cos(ang) / big_n)[:, :n]
    inv[1, :nf] = (-wgt * np.sin(ang) / big_n)[:, :n]
    fwd = fwd.reshape(2 * nfp, big_n)
    inv = inv.reshape(2 * nfp, n).T
    return (np.ascontiguousarray(fwd[:, :n]).astype(BF16), fwd.astype(BF16),
            np.ascontiguousarray(inv).astype(BF16))


def _short_conv_gate_kernel(u_ref, f_ref, x0_ref, usk_ref, lu_ref, lf_ref, li_ref, o_ref):
    half = lu_ref.shape[0] // 2
    xs = _dot(lu_ref[...], u_ref[...].astype(BF16))
    ks = _dot(lf_ref[...], f_ref[...].astype(BF16))
    xr, xi, kr, ki = xs[:half], xs[half:], ks[:half], ks[half:]
    y = jnp.concatenate([xr * kr - xi * ki, xr * ki + xi * kr], axis=0).astype(BF16)
    conv = _dot(li_ref[...], y)
    o_ref[...] = (x0_ref[...].astype(F32) * conv + usk_ref[...].astype(F32)).astype(o_ref.dtype)


def _long_conv_gate_short(u, x0, usk, filt2, n):
    c = u.shape[1]
    lu, lf, li = _dense_dft_matrices(n)
    ct = 256
    blk = pl.BlockSpec((n, ct), lambda j: (0, j))
    full = lambda a: pl.BlockSpec(a.shape, lambda j: (0, 0))
    vm = 32 << 20
    return pl.pallas_call(
        _short_conv_gate_kernel,
        grid=(c // ct,),
        in_specs=[blk, pl.BlockSpec((2 * n, ct), lambda j: (0, j)), blk, blk, full(lu), full(lf), full(li)],
        out_specs=blk,
        out_shape=jax.ShapeDtypeStruct((n, c), BF16),
        compiler_params=_cparams(("parallel",), vm),
        name="context_long_conv",
    )(u, filt2, x0, usk, lu, lf, li)


@functools.lru_cache(maxsize=None)
def _rope_tables(n):
    t = np.arange(n)
    row = (t // GRID_W).astype(np.float64)
    col = (t % GRID_W).astype(np.float64)
    axis_dim = HEAD_DIM // 2
    inv = 1.0 / (ROPE_THETA ** (np.arange(0, axis_dim, 2, dtype=np.float64) / axis_dim))
    ar, ac = row[:, None] * inv, col[:, None] * inv
    cos = np.concatenate([np.cos(ar), np.cos(ar), np.cos(ac), np.cos(ac)], axis=-1)
    sin = np.concatenate([-np.sin(ar), np.sin(ar), -np.sin(ac), np.sin(ac)], axis=-1)
    return cos.astype(np.float32), sin.astype(np.float32)


TM_PROJ = 512
TM_IN = 1024
TN_IN = 768
TM_FFN = 512
TF_FFN = 1024
TQ_ATTN = 512
TK_ATTN = 2048


def kernel(x, c, ctx, c_ctx, mod_w, mod_b, norm_mix_g, norm_ffn_g, ffn_w1, ffn_w2, ev_w_in, ev_q_norm, ev_k_norm, ev_conv_w, ev_w_out, od_w_in, od_conv_w, od_conv_b, od_f_w1, od_f_b1, od_f_freq, od_f_w2, od_f_b2, od_f_w3, od_f_decay, od_skip, od_w_out, final_g):
    batch, n, d = x.shape
    assert batch == 1
    n_ctx = ctx.shape[1]
    depth = mod_w.shape[0]
    last_ctx_read = ((depth - 1) // 2) * 2
    attn_w = N_Q_HEADS * HEAD_DIM
    kv_w = N_KV_HEADS * HEAD_DIM

    xs = x[0]
    xc = ctx[0]
    cond = jnp.zeros((SUBLANES, d), F32).at[0].set(c[0]).at[1].set(c_ctx)
    modv = _mod_vectors(cond, mod_w, mod_b)
    rope = _rope_tables(n)

    ev_in = [ev_w_in[:1].astype(BF16)]
    late_cast = (ffn_w1, ffn_w2, ev_w_out, od_w_in, od_w_out, ev_w_in[1:])
    g_mix = norm_mix_g.reshape(depth, 1, d)
    g_ffn = norm_ffn_g.reshape(depth, 1, d)
    kv_block = attn_w // (2 * kv_w)
    conv_col = attn_w + 2 * kv_w

    for l in range(depth):
        ctx_full = l < last_ctx_read
        if l % 2 == 0:
            e = l // 2
            p_l = _modmm(xs, g_mix, modv, l, 0, 0, ev_in[e], 0, TM_IN, TN_IN)
            q_l, k_l, v_l = _qkv_prep(p_l, 0, kv_block, ev_q_norm[e], ev_k_norm[e], rope, TM_PROJ)
            p_c = _modmm(xc, g_mix, modv, l, 1, 0, ev_in[e], 0, TM_IN, TN_IN)
            q_c, k_c, v_c = _qkv_prep(p_c, 0, kv_block, ev_q_norm[e], ev_k_norm[e], None, TM_PROJ)
            if e == 0:
                att_l, w1_all, w2_all, ev_out, od_in, od_out, ev_in_rest = _attention(
                    q_l, k_l, v_l, k_c, v_c, TQ_ATTN, TK_ATTN, cast=late_cast)
                ev_in += [ev_in_rest[i:i + 1] for i in range(ev_in_rest.shape[0])]
            else:
                att_l, = _attention(q_l, k_l, v_l, k_c, v_c, TQ_ATTN, TK_ATTN)
            conv_l = _short_conv_mixer(p_l, conv_col, ev_conv_w[e])
            if ctx_full:
                att_c, = _attention(q_c, k_c, v_c, None, None, TQ_ATTN, TK_ATTN)
                conv_c = _short_conv_mixer(p_c, conv_col, ev_conv_w[e])
                xc, hc = _outproj([att_c, conv_c], ev_out, e, xc, modv, l, 1, g_ffn, TM_PROJ)
            xs, hs = _outproj([att_l, conv_l], ev_out, e, xs, modv, l, 0, g_ffn, TM_PROJ)
        else:
            o = l // 2
            fargs = (od_f_w1[o], od_f_b1[o], od_f_freq[o], od_f_w2[o], od_f_b2[o], od_f_w3[o], od_f_decay[o])
            p_l = _modmm(xs, g_mix, modv, l, 0, 0, od_in, o, TM_IN, TN_IN, out_dtype=BF16)
            u, x0, usk = _hyena_prep(p_l, od_conv_w[o], od_conv_b[o], od_skip[o])
            filt2 = _hyena_filter(n, d, *fargs, tr=1024, tc=1024)
            y = _long_conv_gate(u, x0, usk, filt2, n)
            xs, hs = _outproj([y], od_out, o, xs, modv, l, 0, g_ffn, TM_PROJ)
            if ctx_full:
                p_c = _modmm(xc, g_mix, modv, l, 1, 0, od_in, o, TM_IN, TN_IN, out_dtype=BF16)
                u, x0, usk = _hyena_prep(p_c, od_conv_w[o], od_conv_b[o], od_skip[o])
                filt2 = _hyena_filter(n_ctx, d, *fargs, tr=1024, tc=1024)
                y = _long_conv_gate_short(u, x0, usk, filt2, n_ctx)
                xc, hc = _outproj([y], od_out, o, xc, modv, l, 1, g_ffn, TM_PROJ)
        last = l == depth - 1
        xs = _ffn(xs, hs, modv, l, 0, w1_all, w2_all, TM_FFN, TF_FFN, final_gain=final_g if last else None)
        if ctx_full:
            xc = _ffn(xc, hc, modv, l, 1, w1_all, w2_all, TM_FFN, TF_FFN)

    return xs[None]
```

```python
import functools
import math

import numpy as np
import jax
import jax.numpy as jnp
from jax import lax
from jax.experimental import pallas as pl
from jax.experimental.pallas import tpu as pltpu

F32 = jnp.float32
BF16 = jnp.bfloat16

HEAD_DIM = 128
N_Q_HEADS = 8
N_KV_HEADS = 2
Q_PER_KV = N_Q_HEADS // N_KV_HEADS
GRID_W = 64
ROPE_THETA = 10000.0
FILTER_BANDS = 16
FILTER_MOD_SHIFT = 0.05
EPS = 1e-6

LANES = 128
SUBLANES = 8
VMEM_BUDGET = 56 * 1024 * 1024
DFT_N1 = 128
ONES_ROWS = 16

TN_MOD = 1536
TM_PROJ = 512
TM_IN = 1024
TN_IN = 768
TM_FFN = 512
TF_FFN = 1024
TQ_ATTN = 512
TK_ATTN = 2048
TR_FILT = 1024
TC_FILT = 1024


def _cparams(sem, vmem_bytes):
    return pltpu.CompilerParams(dimension_semantics=sem,
                                vmem_limit_bytes=int(min(max(vmem_bytes, 16 << 20), VMEM_BUDGET)))


def _nbytes(shape, dtype):
    return int(np.prod(shape)) * jnp.dtype(dtype).itemsize


def _dot(a, b):
    return jnp.dot(a, b, preferred_element_type=F32)


def _rms_rows(x):
    return x * lax.rsqrt(jnp.mean(x * x, axis=-1, keepdims=True) + EPS)


def _mod_kernel(c_ref, w_ref, b_ref, o_ref):
    c = c_ref[...]
    s = c * (1.0 / (1.0 + jnp.exp(-c)))
    o_ref[0] = _dot(s.astype(BF16), w_ref[0].astype(BF16)) + b_ref[0]


def _mod_vectors(cond, mod_w, mod_b):
    depth, d, n6 = mod_w.shape
    tn = TN_MOD
    rows = cond.shape[0]
    vm = 2 * _nbytes((d, tn), F32) + _nbytes((d, tn), BF16) + (4 << 20)
    return pl.pallas_call(
        _mod_kernel,
        grid=(depth, n6 // tn),
        in_specs=[pl.BlockSpec((rows, d), lambda l, j: (0, 0)),
                  pl.BlockSpec((1, d, tn), lambda l, j: (l, 0, j)),
                  pl.BlockSpec((1, 1, tn), lambda l, j: (l, 0, j))],
        out_specs=pl.BlockSpec((1, rows, tn), lambda l, j: (l, 0, j)),
        out_shape=jax.ShapeDtypeStruct((depth, rows, n6), F32),
        compiler_params=_cparams(("parallel", "parallel"), vm),
        name="adaln_vectors",
    )(cond, mod_w, mod_b.reshape(depth, 1, n6))


def _modmm_kernel(x_ref, g_ref, sh_ref, sc_ref, w_ref, o_ref, h_ref, *, row):
    @pl.when(pl.program_id(1) == 0)
    def _():
        h = _rms_rows(x_ref[...]) * g_ref[...]
        h = h * (1.0 + sc_ref[row:row + 1, :]) + sh_ref[row:row + 1, :]
        h_ref[...] = h.astype(BF16)

    o_ref[...] = _dot(h_ref[...], w_ref[...]).astype(o_ref.dtype)


def _modmm(x, gains, modv, l, row, w, wi, tm, tn, out_dtype=F32):
    m, d = x.shape
    n = w.shape[2]
    tm = min(tm, m)
    vm = (2 * _nbytes((tm, d), F32) + _nbytes((tm, d), BF16) + 2 * _nbytes((d, tn), BF16)
          + 2 * _nbytes((tm, tn), F32) + 2 * _nbytes((tm, d), F32) + (4 << 20))
    return pl.pallas_call(
        functools.partial(_modmm_kernel, row=row),
        grid=(m // tm, n // tn),
        in_specs=[pl.BlockSpec((tm, d), lambda i, j: (i, 0)),
                  pl.BlockSpec((None, 1, d), lambda i, j: (l, 0, 0)),
                  pl.BlockSpec((None, SUBLANES, d), lambda i, j: (l, 0, 0)),
                  pl.BlockSpec((None, SUBLANES, d), lambda i, j: (l, 0, 1)),
                  pl.BlockSpec((None, d, tn), lambda i, j: (wi, 0, j))],
        out_specs=pl.BlockSpec((tm, tn), lambda i, j: (i, j)),
        out_shape=jax.ShapeDtypeStruct((m, n), out_dtype),
        scratch_shapes=[pltpu.VMEM((tm, d), BF16)],
        compiler_params=_cparams(("parallel", "arbitrary"), vm),
        name="modulated_projection",
    )(x, gains, modv, modv, w)


def _outproj_kernel(*refs, n_in, row):
    a_refs = refs[:n_in]
    w_refs = refs[n_in:2 * n_in]
    x_ref, gate_ref, g_ref, sh_ref, sc_ref, o_ref, h_ref = refs[2 * n_in:]
    acc = _dot(a_refs[0][...].astype(BF16), w_refs[0][...])
    for a_ref, w_ref in zip(a_refs[1:], w_refs[1:]):
        acc = acc + _dot(a_ref[...].astype(BF16), w_ref[...])
    x_new = x_ref[...] + gate_ref[row:row + 1, :] * acc
    o_ref[...] = x_new
    h = _rms_rows(x_new) * g_ref[...]
    h_ref[...] = (h * (1.0 + sc_ref[row:row + 1, :]) + sh_ref[row:row + 1, :]).astype(BF16)


def _outproj(acts, w, wi, x, modv, l, row, gains, tm):
    m, d = x.shape
    tm = min(tm, m)
    n_in = len(acts)
    in_specs, args = [], []
    for a in acts:
        in_specs.append(pl.BlockSpec((tm, a.shape[1]), lambda i: (i, 0)))
        args.append(a)
    row0 = 0
    for a in acts:
        k = a.shape[1]
        in_specs.append(pl.BlockSpec((None, k, d), functools.partial(lambda i, b: (wi, b, 0), b=row0 // k)))
        args.append(w)
        row0 += k
    modblk = lambda col: pl.BlockSpec((None, SUBLANES, d), lambda i: (l, 0, col))
    in_specs += [pl.BlockSpec((tm, d), lambda i: (i, 0)), modblk(2),
                 pl.BlockSpec((None, 1, d), lambda i: (l, 0, 0)), modblk(3), modblk(4)]
    args += [x, modv, gains, modv, modv]
    vm = (2 * sum(_nbytes((tm, a.shape[1]), a.dtype) for a in acts) + 2 * _nbytes(w.shape[1:], BF16)
          + 8 * _nbytes((tm, d), F32) + (4 << 20))
    rowblk = pl.BlockSpec((tm, d), lambda i: (i, 0))
    return pl.pallas_call(
        functools.partial(_outproj_kernel, n_in=n_in, row=row),
        grid=(m // tm,),
        in_specs=in_specs,
        out_specs=[rowblk, rowblk],
        out_shape=[jax.ShapeDtypeStruct((m, d), F32), jax.ShapeDtypeStruct((m, d), BF16)],
        compiler_params=_cparams(("parallel",), vm),
        name="gated_out_projection",
    )(*args)


def _ffn_kernel(*refs, row, final_norm):
    if final_norm:
        x_ref, h_ref, gate_ref, w1_ref, w2_ref, fg_ref, o_ref = refs
    else:
        x_ref, h_ref, gate_ref, w1_ref, w2_ref, o_ref = refs
    j = pl.program_id(1)

    @pl.when(j == 0)
    def _():
        o_ref[...] = jnp.zeros(o_ref.shape, F32)

    a = jnp.maximum(_dot(h_ref[...], w1_ref[...]), 0.0)
    o_ref[...] += _dot((a * a).astype(BF16), w2_ref[...])

    @pl.when(j == pl.num_programs(1) - 1)
    def _():
        y = x_ref[...] + gate_ref[row:row + 1, :] * o_ref[...]
        if final_norm:
            y = _rms_rows(y) * fg_ref[...]
        o_ref[...] = y


def _ffn(x, h, modv, l, row, w1, w2, tm, tf, final_gain=None):
    m, d = x.shape
    f = w1.shape[2]
    tm = min(tm, m)
    vm = (4 * _nbytes((tm, d), F32) + 2 * _nbytes((tm, d), BF16)
          + 4 * _nbytes((d, tf), BF16) + 3 * _nbytes((tm, tf), F32) + (6 << 20))
    in_specs = [pl.BlockSpec((tm, d), lambda i, j: (i, 0)),
                pl.BlockSpec((tm, d), lambda i, j: (i, 0)),
                pl.BlockSpec((None, SUBLANES, d), lambda i, j: (l, 0, 5)),
                pl.BlockSpec((None, d, tf), lambda i, j: (l, 0, j)),
                pl.BlockSpec((None, tf, d), lambda i, j: (l, j, 0))]
    args = [x, h, modv, w1, w2]
    if final_gain is not None:
        in_specs.append(pl.BlockSpec((1, d), lambda i, j: (0, 0)))
        args.append(final_gain.reshape(1, d))
    return pl.pallas_call(
        functools.partial(_ffn_kernel, row=row, final_norm=final_gain is not None),
        grid=(m // tm, f // tf),
        in_specs=in_specs,
        out_specs=pl.BlockSpec((tm, d), lambda i, j: (i, 0)),
        out_shape=jax.ShapeDtypeStruct((m, d), F32),
        compiler_params=_cparams(("parallel", "arbitrary"), vm),
        name="gated_ffn",
    )(*args)


def _rope(x, cos, sin):
    lane = lax.broadcasted_iota(jnp.int32, x.shape, 1)
    first_half = (lane % (HEAD_DIM // 2)) < (HEAD_DIM // 4)
    partner = jnp.where(first_half,
                        pltpu.roll(x, HEAD_DIM - HEAD_DIM // 4, 1),
                        pltpu.roll(x, HEAD_DIM // 4, 1))
    return x * cos + partner * sin


def _qkv_kernel(*refs, use_rope, q_scale):
    if use_rope:
        pq_ref, pkv_ref, qg_ref, kg_ref, cos_ref, sin_ref, q_ref, k_ref, v_ref = refs
        cos, sin = cos_ref[...], sin_ref[...]
    else:
        pq_ref, pkv_ref, qg_ref, kg_ref, q_ref, k_ref, v_ref = refs
    kv_w = N_KV_HEADS * HEAD_DIM
    for h in range(N_Q_HEADS):
        sl = slice(h * HEAD_DIM, (h + 1) * HEAD_DIM)
        xn = _rms_rows(pq_ref[:, sl]) * qg_ref[...]
        if use_rope:
            xn = _rope(xn, cos, sin)
        q_ref[:, sl] = (xn * q_scale).astype(BF16)
    for h in range(N_KV_HEADS):
        sl = slice(h * HEAD_DIM, (h + 1) * HEAD_DIM)
        xn = _rms_rows(pkv_ref[:, sl]) * kg_ref[...]
        if use_rope:
            xn = _rope(xn, cos, sin)
        k_ref[:, sl] = xn.astype(BF16)
    v_ref[...] = pkv_ref[:, kv_w:2 * kv_w].T.astype(BF16)


def _qkv_prep(p, q_col_block, kv_col_block, q_gain, k_gain, rope, tm):
    m = p.shape[0]
    tm = min(tm, m)
    attn_w = N_Q_HEADS * HEAD_DIM
    kv_w = N_KV_HEADS * HEAD_DIM
    in_specs = [pl.BlockSpec((tm, attn_w), lambda i: (i, q_col_block)),
                pl.BlockSpec((tm, 2 * kv_w), lambda i: (i, kv_col_block)),
                pl.BlockSpec((1, HEAD_DIM), lambda i: (0, 0)),
                pl.BlockSpec((1, HEAD_DIM), lambda i: (0, 0))]
    args = [p, p, q_gain.reshape(1, HEAD_DIM), k_gain.reshape(1, HEAD_DIM)]
    if rope is not None:
        in_specs += [pl.BlockSpec((tm, HEAD_DIM), lambda i: (i, 0))] * 2
        args += list(rope)
    vm = 4 * _nbytes((tm, attn_w + 2 * kv_w), F32) + (8 << 20)
    return pl.pallas_call(
        functools.partial(_qkv_kernel, use_rope=rope is not None, q_scale=HEAD_DIM ** -0.5),
        grid=(m // tm,),
        in_specs=in_specs,
        out_specs=[pl.BlockSpec((tm, attn_w), lambda i: (i, 0)),
                   pl.BlockSpec((tm, kv_w), lambda i: (i, 0)),
                   pl.BlockSpec((kv_w, tm), lambda i: (0, i))],
        out_shape=[jax.ShapeDtypeStruct((m, attn_w), BF16),
                   jax.ShapeDtypeStruct((m, kv_w), BF16),
                   jax.ShapeDtypeStruct((kv_w, m), BF16)],
        compiler_params=_cparams(("parallel",), vm),
        name="qkv_norm_rope",
    )(*args)


def _attn_kernel(*refs, has_extra, tq, n_cast):
    n_in = 5 if has_extra else 3
    cast_in = refs[n_in:n_in + n_cast]
    o_ref = refs[n_in + n_cast]
    cast_out = refs[n_in + n_cast + 1:n_in + 2 * n_cast + 1]
    qs_ref, m_ref, acc_ref = refs[n_in + 2 * n_cast + 1:]
    if has_extra:
        q_ref, k_ref, vt_ref, ke_ref, vte_ref = refs[:n_in]
    else:
        q_ref, k_ref, vt_ref = refs[:n_in]
    kv = pl.program_id(2)

    for src, dst in zip(cast_in, cast_out):
        dst[...] = src[...].astype(BF16)

    def update(k, vt):
        vt1 = jnp.concatenate([vt, jnp.ones((ONES_ROWS, vt.shape[1]), BF16)], axis=0)
        cols = [slice(h * tq, (h + 1) * tq) for h in range(Q_PER_KV)]
        s, p, alpha = {}, {}, {}
        for t in range(Q_PER_KV + 2):
            if t < Q_PER_KV:
                s[t] = lax.dot_general(k, qs_ref[cols[t], :], (((1,), (1,)), ((), ())),
                                       preferred_element_type=F32)
            h = t - 1
            if 0 <= h < Q_PER_KV:
                m_prev = m_ref[:, cols[h]]
                m_new = jnp.maximum(m_prev, jnp.max(s[h], axis=0, keepdims=True))
                alpha[h] = jnp.exp(m_prev - m_new)
                p[h] = jnp.exp(s.pop(h) - m_new).astype(BF16)
                m_ref[:, cols[h]] = m_new
            h = t - 2
            if 0 <= h < Q_PER_KV:
                acc_ref[:, cols[h]] = alpha.pop(h) * acc_ref[:, cols[h]] + _dot(vt1, p.pop(h))

    @pl.when(kv == 0)
    def _():
        for h in range(Q_PER_KV):
            qs_ref[h * tq:(h + 1) * tq, :] = q_ref[:, h * HEAD_DIM:(h + 1) * HEAD_DIM]
        m_ref[...] = jnp.full(m_ref.shape, -jnp.inf, F32)
        acc_ref[...] = jnp.zeros(acc_ref.shape, F32)
        if has_extra:
            update(ke_ref[...], vte_ref[...])

    update(k_ref[...], vt_ref[...])

    @pl.when(kv == pl.num_programs(2) - 1)
    def _():
        o_t = acc_ref[0:HEAD_DIM, :] / acc_ref[HEAD_DIM:HEAD_DIM + 1, :]
        for h in range(Q_PER_KV):
            o_ref[:, h * HEAD_DIM:(h + 1) * HEAD_DIM] = o_t[:, h * tq:(h + 1) * tq].T.astype(o_ref.dtype)


def _attention(q, k, vt, k_extra, vt_extra, tq, tk, cast=()):
    m = q.shape[0]
    s_len = k.shape[0]
    tq = min(tq, m)
    tk = min(tk, s_len)
    gw = Q_PER_KV * HEAD_DIM
    cols = Q_PER_KV * tq
    has_extra = k_extra is not None
    n_i, n_j = m // tq, s_len // tk
    steps = N_KV_HEADS * n_i * n_j
    in_specs = [pl.BlockSpec((tq, gw), lambda g, i, j: (i, g)),
                pl.BlockSpec((tk, HEAD_DIM), lambda g, i, j: (j, g)),
                pl.BlockSpec((HEAD_DIM, tk), lambda g, i, j: (g, j))]
    args = [q, k, vt]
    if has_extra:
        e = k_extra.shape[0]
        in_specs += [pl.BlockSpec((e, HEAD_DIM), lambda g, i, j: (0, g)),
                     pl.BlockSpec((HEAD_DIM, e), lambda g, i, j: (g, 0))]
        args += [k_extra, vt_extra]
    out_specs = [pl.BlockSpec((tq, gw), lambda g, i, j: (i, g))]
    out_shape = [jax.ShapeDtypeStruct((m, N_Q_HEADS * HEAD_DIM), BF16)]
    for w in cast:
        lanes = w.shape[-1]
        rows = w.size // (steps * lanes)
        assert rows * steps * lanes == w.size and rows % (2 * SUBLANES) == 0
        slab = pl.BlockSpec((None, rows, lanes), lambda g, i, j: ((g * n_i + i) * n_j + j, 0, 0))
        in_specs.append(slab)
        args.append(w.reshape(steps, rows, lanes))
        out_specs.append(slab)
        out_shape.append(jax.ShapeDtypeStruct((steps, rows, lanes), BF16))
    acc_rows = HEAD_DIM + ONES_ROWS
    vm = 4 * _nbytes((tk, cols), F32) + 6 * _nbytes((acc_rows, cols), F32) + (8 << 20)
    outs = pl.pallas_call(
        functools.partial(_attn_kernel, has_extra=has_extra, tq=tq, n_cast=len(cast)),
        grid=(N_KV_HEADS, n_i, n_j),
        in_specs=in_specs,
        out_specs=out_specs,
        out_shape=out_shape,
        scratch_shapes=[pltpu.VMEM((cols, HEAD_DIM), BF16),
                        pltpu.VMEM((1, cols), F32),
                        pltpu.VMEM((acc_rows, cols), F32)],
        compiler_params=_cparams(("parallel", "parallel", "arbitrary"), vm),
        name="gqa_flash_attention",
    )(*args)
    return (outs[0],) + tuple(o.reshape(w.shape) for o, w in zip(outs[1:], cast))


def _conv3(z, w_ref):
    n = z.shape[0]
    prev = pltpu.roll(z, 1, 0)
    nxt = pltpu.roll(z, n - 1, 0)
    row = lax.broadcasted_iota(jnp.int32, (SUBLANES, z.shape[1]), 0)
    prev = jnp.concatenate([jnp.where(row == 0, 0.0, prev[:SUBLANES]), prev[SUBLANES:]], axis=0)
    nxt = jnp.concatenate([nxt[:n - SUBLANES], jnp.where(row == SUBLANES - 1, 0.0, nxt[n - SUBLANES:])], axis=0)
    return prev * w_ref[0:1, :] + z * w_ref[1:2, :] + nxt * w_ref[2:3, :]


def _convmix_kernel(u_ref, b_ref, c_ref, w_ref, o_ref):
    o_ref[...] = (b_ref[...] * _conv3(c_ref[...] * u_ref[...], w_ref)).astype(o_ref.dtype)


def _short_conv_mixer(p, col0, conv_w):
    m = p.shape[0]
    c = conv_w.shape[1]
    ct = LANES
    nb = c // ct
    b0 = col0 // ct
    vm = 8 * _nbytes((m, ct), F32) + 8 * _nbytes((m, ct), F32) + (4 << 20)
    return pl.pallas_call(
        _convmix_kernel,
        grid=(nb,),
        in_specs=[pl.BlockSpec((m, ct), lambda j: (0, b0 + j)),
                  pl.BlockSpec((m, ct), lambda j: (0, b0 + nb + j)),
                  pl.BlockSpec((m, ct), lambda j: (0, b0 + 2 * nb + j)),
                  pl.BlockSpec((3, ct), lambda j: (0, j))],
        out_specs=pl.BlockSpec((m, ct), lambda j: (0, j)),
        out_shape=jax.ShapeDtypeStruct((m, c), BF16),
        compiler_params=_cparams(("parallel",), vm),
        name="short_conv_mixer",
    )(p, p, p, conv_w)


def _hyena_prep_kernel(p0_ref, p1_ref, p2_ref, w0_ref, w1_ref, w2_ref, b0_ref, b1_ref, b2_ref,
                       skip_ref, u_ref, x0_ref, usk_ref):
    x0 = _conv3(p0_ref[...].astype(F32), w0_ref) + b0_ref[...]
    x1 = _conv3(p1_ref[...].astype(F32), w1_ref) + b1_ref[...]
    v = _conv3(p2_ref[...].astype(F32), w2_ref) + b2_ref[...]
    u = x1 * v
    u_ref[...] = u.astype(u_ref.dtype)
    x0_ref[...] = x0.astype(x0_ref.dtype)
    usk_ref[...] = (x0 * (u * skip_ref[...])).astype(usk_ref.dtype)


def _hyena_prep(p, conv_w, conv_b, skip):
    m = p.shape[0]
    c = skip.shape[0]
    ct = LANES
    nb = c // ct
    vm = 12 * _nbytes((m, ct), F32) + 10 * _nbytes((m, ct), F32) + (4 << 20)
    blk = lambda off: pl.BlockSpec((m, ct), functools.partial(lambda j, o: (0, o + j), o=off))
    wblk = lambda off: pl.BlockSpec((3, ct), functools.partial(lambda j, o: (0, o + j), o=off))
    bblk = lambda off: pl.BlockSpec((1, ct), functools.partial(lambda j, o: (0, o + j), o=off))
    out = jax.ShapeDtypeStruct((m, c), BF16)
    return pl.pallas_call(
        _hyena_prep_kernel,
        grid=(nb,),
        in_specs=[blk(0), blk(nb), blk(2 * nb), wblk(0), wblk(nb), wblk(2 * nb),
                  bblk(0), bblk(nb), bblk(2 * nb), pl.BlockSpec((1, ct), lambda j: (0, j))],
        out_specs=[pl.BlockSpec((m, ct), lambda j: (0, j))] * 3,
        out_shape=[out, out, out],
        compiler_params=_cparams(("parallel",), vm),
        name="hyena_short_conv_gate",
    )(p, p, p, conv_w, conv_w, conv_w, conv_b.reshape(1, -1), conv_b.reshape(1, -1),
      conv_b.reshape(1, -1), skip.reshape(1, c))


def _filter_feats_t(n, order):
    t = np.asarray(order, np.float64)
    t_norm = t / max(n - 1, 1)
    bands = np.arange(1, FILTER_BANDS + 1, dtype=np.float64)
    ang = (2 * math.pi / n) * bands[:, None] * t[None, :]
    feats = np.concatenate([t_norm[None, :], np.cos(ang), np.sin(ang)], axis=0)
    out = np.zeros((LANES, len(t)), np.float32)
    out[:feats.shape[0]] = feats
    return out


def _split_bf16(a):
    hi = a.astype(BF16).astype(F32)
    return hi, a - hi


def _dot_split(a, b):
    a_hi, a_lo = _split_bf16(a)
    b_hi, b_lo = _split_bf16(b)
    lhs = jnp.concatenate([a_hi, a_hi, a_lo], axis=1).astype(BF16)
    rhs = jnp.concatenate([b_hi, b_lo, b_hi], axis=0).astype(BF16)
    return _dot(lhs, rhs)


def _filter_kernel(f_ref, w1_ref, b1_ref, fr_ref, w2_ref, b2_ref, w3_ref, dec_ref, o_ref, hs_ref, tn_ref,
                   *, n):
    tr = tn_ref.shape[0]
    hid = w3_ref.shape[0]

    @pl.when(pl.program_id(2) == 0)
    def _():
        fr = fr_ref[...]
        h = jnp.sin(fr * (_dot_split(w1_ref[...], f_ref[...]) + b1_ref[...]))
        h = jnp.concatenate([h, jnp.zeros((LANES - hid, tr), F32)], axis=0)
        h = jnp.sin(fr * (_dot_split(w2_ref[...], h) + b2_ref[...]))
        hi, lo = _split_bf16(h)
        pad = jnp.zeros((2 * LANES - 3 * hid, tr), F32)
        hs_ref[...] = jnp.concatenate([hi, hi, lo, pad], axis=0).T.astype(BF16)
        pos = pl.program_id(1) * tr + lax.broadcasted_iota(jnp.int32, (tr, LANES), 0)
        pos = jnp.where(pl.program_id(0) == 0, pos, n - 1 - pos)
        tn_ref[...] = pos.astype(F32) / float(max(n - 1, 1))

    w_hi, w_lo = _split_bf16(w3_ref[...])
    pad = jnp.zeros((2 * LANES - 3 * hid, w_hi.shape[1]), F32)
    rhs = jnp.concatenate([w_hi, w_lo, w_hi, pad], axis=0).astype(BF16)
    hw = _dot(hs_ref[...], rhs)
    t_norm = jnp.concatenate([tn_ref[...]] * (hw.shape[1] // LANES), axis=1)
    window = jnp.exp(-t_norm * jnp.abs(dec_ref[...])) + FILTER_MOD_SHIFT
    o_ref[...] = (hw * window).astype(o_ref.dtype)


def _hyena_filter(n, ch, f_w1, f_b1, f_freq, f_w2, f_b2, f_w3, f_decay, tr, tc):
    hid = f_w1.shape[1]
    assert 3 * hid <= 2 * LANES and hid <= LANES
    feats = np.stack([_filter_feats_t(n, np.arange(n)), _filter_feats_t(n, np.arange(n)[::-1])])
    w1t = jnp.zeros((hid, LANES), F32).at[:, :f_w1.shape[0]].set(f_w1.T)
    w2t = jnp.zeros((hid, LANES), F32).at[:, :hid].set(f_w2.T)
    tr = min(tr, n)
    nrb = n // tr
    ncb = ch // tc
    vm = 6 * _nbytes((tr, tc), F32) + 4 * _nbytes((LANES, tr), F32) + (8 << 20)
    col = lambda a: a.reshape(hid, 1)
    small = lambda shape: pl.BlockSpec(shape, lambda s, i, j: (0, 0))
    return pl.pallas_call(
        functools.partial(_filter_kernel, n=n),
        grid=(2, nrb, ncb),
        in_specs=[pl.BlockSpec((None, LANES, tr), lambda s, i, j: (s, 0, i)),
                  small((hid, LANES)), small((hid, 1)), small((hid, 1)),
                  small((hid, LANES)), small((hid, 1)),
                  pl.BlockSpec((hid, tc), lambda s, i, j: (0, s * ncb + j)),
                  pl.BlockSpec((1, tc), lambda s, i, j: (0, s * ncb + j))],
        out_specs=pl.BlockSpec((tr, tc), lambda s, i, j: (s * nrb + i, j)),
        out_shape=jax.ShapeDtypeStruct((2 * n, ch), BF16),
        scratch_shapes=[pltpu.VMEM((tr, 2 * LANES), BF16), pltpu.VMEM((tr, LANES), F32)],
        compiler_params=_cparams(("parallel", "parallel", "arbitrary"), vm),
        name="hyena_filter",
    )(jnp.asarray(feats), w1t, col(f_b1), col(f_freq), w2t, col(f_b2), f_w3, f_decay.reshape(1, -1))


def _dft_geometry(n):
    big_n = 2 * n
    n2 = big_n // DFT_N1
    nh = n2 // 2 + 1
    return big_n, n2, nh


def _k2_block(nh, limit=13):
    return max(k for k in range(1, limit + 1) if nh % k == 0)


@functools.lru_cache(maxsize=None)
def _stage1_matrix(n, rows_n2):
    big_n, _, nh = _dft_geometry(n)
    groups = DFT_N1 // SUBLANES
    out = np.zeros((groups, 2, nh, SUBLANES, rows_n2, SUBLANES), np.float32)
    k2 = np.arange(nh, dtype=np.float64)[:, None]
    n2v = np.arange(rows_n2, dtype=np.float64)[None, :]
    for g in range(groups):
        for j in range(SUBLANES):
            t = SUBLANES * g + j + DFT_N1 * n2v
            ang = 2 * math.pi * np.mod(k2 * t, big_n) / big_n
            out[g, 0, :, j, :, j] = np.cos(ang)
            out[g, 1, :, j, :, j] = -np.sin(ang)
    return out.reshape(groups, 2 * nh * SUBLANES, rows_n2 * SUBLANES).astype(BF16)


@functools.lru_cache(maxsize=None)
def _stage2_matrices():
    idx = np.arange(DFT_N1, dtype=np.float64)
    ang = 2 * math.pi * np.mod(np.outer(idx, idx), DFT_N1) / DFT_N1
    c, s = np.cos(ang), np.sin(ang)
    fwd = np.block([[c, s], [-s, c]])
    inv = np.block([[c, -s], [s, c]])
    return fwd.astype(BF16), inv.astype(BF16)


@functools.lru_cache(maxsize=None)
def _stage4_matrix(n):
    big_n, n2c, nh = _dft_geometry(n)
    groups = DFT_N1 // SUBLANES
    rows_n2 = n2c // 2
    out = np.zeros((groups, rows_n2, SUBLANES, 2, nh, SUBLANES), np.float32)
    k2 = np.arange(nh, dtype=np.float64)[None, :]
    wgt = np.where(k2 <= n2c // 2, 2.0, 0.0)
    wgt[0, 0] = 1.0
    wgt[0, n2c // 2] = 1.0
    n2v = np.arange(rows_n2, dtype=np.float64)[:, None]
    for g in range(groups):
        for j in range(SUBLANES):
            t = SUBLANES * g + j + DFT_N1 * n2v
            ang = 2 * math.pi * np.mod(k2 * t, big_n) / big_n
            out[g, :, j, 0, :, j] = wgt * np.cos(ang) / big_n
            out[g, :, j, 1, :, j] = -wgt * np.sin(ang) / big_n
    return out.reshape(groups, rows_n2 * SUBLANES, 2 * nh * SUBLANES).astype(BF16)


def _dft_stage1_kernel(x_ref, l_ref, o_ref, *, gsteps):
    nb, _, ct = x_ref.shape
    rows = o_ref.shape[0]
    xin = x_ref[...].astype(F32)
    outs = []
    for gi in range(gsteps):
        sl = slice(SUBLANES * gi, SUBLANES * (gi + 1))
        xv = xin[:, sl, :].reshape(nb * SUBLANES, ct).astype(BF16)
        outs.append(_dot(l_ref[gi], xv).reshape(rows, SUBLANES, ct))
    o_ref[...] = jnp.concatenate(outs, axis=1).astype(o_ref.dtype)


def _dft_stage1(x3, n, gsteps, ct):
    rows_n2, _, c = x3.shape
    _, _, nh = _dft_geometry(n)
    mat = _stage1_matrix(n, rows_n2)
    groups = mat.shape[0]
    ct = min(ct, c)
    gw = SUBLANES * gsteps
    vm = (2 * _nbytes((rows_n2, gw, ct), F32) + 2 * _nbytes((gsteps,) + mat.shape[1:], BF16)
          + 2 * _nbytes((2 * nh, gw, ct), F32) + 3 * _nbytes((2 * nh * SUBLANES, ct), F32) + (4 << 20))
    return pl.pallas_call(
        functools.partial(_dft_stage1_kernel, gsteps=gsteps),
        grid=(groups // gsteps, c // ct),
        in_specs=[pl.BlockSpec((rows_n2, gw, ct), lambda g, j: (0, g, j)),
                  pl.BlockSpec((gsteps,) + mat.shape[1:], lambda g, j: (g, 0, 0))],
        out_specs=pl.BlockSpec((2 * nh, gw, ct), lambda g, j: (0, g, j)),
        out_shape=jax.ShapeDtypeStruct((2 * nh, DFT_N1, c), BF16),
        compiler_params=_cparams(("parallel", "parallel"), vm),
        name="dft_stage_n2",
    )(x3, mat)


def _stack_parts(a_ref, kb):
    cols = [jnp.concatenate([a_ref[0, k], a_ref[1, k]], axis=0) for k in range(kb)]
    return jnp.concatenate(cols, axis=1).astype(BF16)


def _store_parts(o_ref, x, kb):
    ct = o_ref.shape[-1]
    for k in range(kb):
        o_ref[0, k] = x[:DFT_N1, k * ct:(k + 1) * ct].astype(o_ref.dtype)
        o_ref[1, k] = x[DFT_N1:, k * ct:(k + 1) * ct].astype(o_ref.dtype)


def _spectral_product_kernel(a_ref, fa_ref, l2_ref, l3_ref, o_ref, *, kb):
    ct = a_ref.shape[-1]
    x = _dot(l2_ref[...], _stack_parts(a_ref, kb))
    f = _dot(l2_ref[...], _stack_parts(fa_ref, kb))
    ys = []
    for k in range(kb):
        cols = slice(k * ct, (k + 1) * ct)
        xr, xi = x[:DFT_N1, cols], x[DFT_N1:, cols]
        kr, ki = f[:DFT_N1, cols], f[DFT_N1:, cols]
        ys.append(jnp.concatenate([xr * kr - xi * ki, xr * ki + xi * kr], axis=0))
    _store_parts(o_ref, _dot(l3_ref[...], jnp.concatenate(ys, axis=1).astype(BF16)), kb)


def _dft_stage2(a4, fa4, kb, ct):
    _, nh, _, c = a4.shape
    fwd, inv = _stage2_matrices()
    ct = min(ct, c)
    blk = pl.BlockSpec((2, kb, DFT_N1, ct), lambda g, j: (0, g, 0, j))
    mblk = pl.BlockSpec((2 * DFT_N1, 2 * DFT_N1), lambda g, j: (0, 0))
    vm = 6 * _nbytes((2, kb, DFT_N1, ct), BF16) + 8 * _nbytes((2 * DFT_N1, kb * ct), F32) + (4 << 20)
    return pl.pallas_call(
        functools.partial(_spectral_product_kernel, kb=kb),
        grid=(nh // kb, c // ct),
        in_specs=[blk, blk, mblk, mblk],
        out_specs=blk,
        out_shape=jax.ShapeDtypeStruct(a4.shape, BF16),
        compiler_params=_cparams(("parallel", "parallel"), vm),
        name="dft_stage_n1_product",
    )(a4, fa4, fwd, inv)


def _dft_stage4_kernel(v_ref, l_ref, x0_ref, usk_ref, o_ref, *, gsteps):
    rows, _, ct = v_ref.shape
    nb = o_ref.shape[0]
    vin = v_ref[...].astype(F32)
    ys = []
    for gi in range(gsteps):
        sl = slice(SUBLANES * gi, SUBLANES * (gi + 1))
        vv = vin[:, sl, :].reshape(rows * SUBLANES, ct).astype(BF16)
        ys.append(_dot(l_ref[gi], vv).reshape(nb, SUBLANES, ct))
    y = jnp.concatenate(ys, axis=1)
    o_ref[...] = (x0_ref[...].astype(F32) * y + usk_ref[...].astype(F32)).astype(o_ref.dtype)


def _dft_stage4(v3, x03, usk3, n, gsteps, ct):
    rows, _, c = v3.shape
    mat = _stage4_matrix(n)
    groups = mat.shape[0]
    nb = x03.shape[0]
    ct = min(ct, c)
    gw = SUBLANES * gsteps
    vm = (2 * _nbytes((rows, gw, ct), F32) + 2 * _nbytes((gsteps,) + mat.shape[1:], BF16)
          + 6 * _nbytes((nb, gw, ct), F32) + 3 * _nbytes((rows * SUBLANES, ct), F32) + (4 << 20))
    oblk = pl.BlockSpec((nb, gw, ct), lambda g, j: (0, g, j))
    return pl.pallas_call(
        functools.partial(_dft_stage4_kernel, gsteps=gsteps),
        grid=(groups // gsteps, c // ct),
        in_specs=[pl.BlockSpec((rows, gw, ct), lambda g, j: (0, g, j)),
                  pl.BlockSpec((gsteps,) + mat.shape[1:], lambda g, j: (g, 0, 0)),
                  oblk, oblk],
        out_specs=oblk,
        out_shape=jax.ShapeDtypeStruct((nb, DFT_N1, c), BF16),
        compiler_params=_cparams(("parallel", "parallel"), vm),
        name="dft_stage_k2_inverse",
    )(v3, mat, x03, usk3)


def _long_conv_gate(u, x0, usk, filt2, n):
    c = u.shape[1]
    _, _, nh = _dft_geometry(n)
    as3 = lambda a: a.reshape(a.shape[0] // DFT_N1, DFT_N1, c)
    fa = _dft_stage1(as3(filt2), n, gsteps=2, ct=512)
    a = _dft_stage1(as3(u), n, gsteps=2, ct=512)
    v = _dft_stage2(a.reshape(2, nh, DFT_N1, c), fa.reshape(2, nh, DFT_N1, c), kb=_k2_block(nh), ct=256)
    y = _dft_stage4(v.reshape(2 * nh, DFT_N1, c), as3(x0), as3(usk), n, gsteps=2, ct=512)
    return y.reshape(n, c)


@functools.lru_cache(maxsize=None)
def _dense_dft_matrices(n):
    big_n = 2 * n
    nf = n + 1
    nfp = -(-nf // SUBLANES) * SUBLANES
    k = np.arange(nf, dtype=np.float64)[:, None]
    t = np.arange(big_n, dtype=np.float64)[None, :]
    ang = 2 * math.pi * np.mod(k * t, big_n) / big_n
    fwd = np.zeros((2, nfp, big_n), np.float32)
    fwd[0, :nf] = np.cos(ang)
    fwd[1, :nf] = -np.sin(ang)
    wgt = np.full((nf, 1), 2.0)
    wgt[0] = 1.0
    wgt[n] = 1.0
    inv = np.zeros((2, nfp, n), np.float32)
    inv[0, :nf] = (wgt * np.cos(ang) / big_n)[:, :n]
    inv[1, :nf] = (-wgt * np.sin(ang) / big_n)[:, :n]
    fwd = fwd.reshape(2 * nfp, big_n)
    inv = inv.reshape(2 * nfp, n).T
    return (np.ascontiguousarray(fwd[:, :n]).astype(BF16), fwd.astype(BF16),
            np.ascontiguousarray(inv).astype(BF16))


def _short_conv_gate_kernel(u_ref, f_ref, x0_ref, usk_ref, lu_ref, lf_ref, li_ref, o_ref):
    half = lu_ref.shape[0] // 2
    xs = _dot(lu_ref[...], u_ref[...].astype(BF16))
    ks = _dot(lf_ref[...], f_ref[...].astype(BF16))
    xr, xi, kr, ki = xs[:half], xs[half:], ks[:half], ks[half:]
    y = jnp.concatenate([xr * kr - xi * ki, xr * ki + xi * kr], axis=0).astype(BF16)
    conv = _dot(li_ref[...], y)
    o_ref[...] = (x0_ref[...].astype(F32) * conv + usk_ref[...].astype(F32)).astype(o_ref.dtype)


def _long_conv_gate_short(u, x0, usk, filt2, n):
    c = u.shape[1]
    lu, lf, li = _dense_dft_matrices(n)
    ct = 256
    blk = pl.BlockSpec((n, ct), lambda j: (0, j))
    full = lambda a: pl.BlockSpec(a.shape, lambda j: (0, 0))
    vm = 32 << 20
    return pl.pallas_call(
        _short_conv_gate_kernel,
        grid=(c // ct,),
        in_specs=[blk, pl.BlockSpec((2 * n, ct), lambda j: (0, j)), blk, blk, full(lu), full(lf), full(li)],
        out_specs=blk,
        out_shape=jax.ShapeDtypeStruct((n, c), BF16),
        compiler_params=_cparams(("parallel",), vm),
        name="context_long_conv",
    )(u, filt2, x0, usk, lu, lf, li)


@functools.lru_cache(maxsize=None)
def _rope_tables(n):
    t = np.arange(n)
    row = (t // GRID_W).astype(np.float64)
    col = (t % GRID_W).astype(np.float64)
    axis_dim = HEAD_DIM // 2
    inv = 1.0 / (ROPE_THETA ** (np.arange(0, axis_dim, 2, dtype=np.float64) / axis_dim))
    ar, ac = row[:, None] * inv, col[:, None] * inv
    cos = np.concatenate([np.cos(ar), np.cos(ar), np.cos(ac), np.cos(ac)], axis=-1)
    sin = np.concatenate([-np.sin(ar), np.sin(ar), -np.sin(ac), np.sin(ac)], axis=-1)
    return cos.astype(np.float32), sin.astype(np.float32)


def kernel(x, c, ctx, c_ctx, mod_w, mod_b, norm_mix_g, norm_ffn_g, ffn_w1, ffn_w2, ev_w_in, ev_q_norm, ev_k_norm, ev_conv_w, ev_w_out, od_w_in, od_conv_w, od_conv_b, od_f_w1, od_f_b1, od_f_freq, od_f_w2, od_f_b2, od_f_w3, od_f_decay, od_skip, od_w_out, final_g):
    batch, n, d = x.shape
    assert batch == 1
    n_ctx = ctx.shape[1]
    depth = mod_w.shape[0]
    last_ctx_read = ((depth - 1) // 2) * 2
    attn_w = N_Q_HEADS * HEAD_DIM
    kv_w = N_KV_HEADS * HEAD_DIM

    xs = x[0]
    xc = ctx[0]
    cond = jnp.zeros((SUBLANES, d), F32).at[0].set(c[0]).at[1].set(c_ctx)
    modv = _mod_vectors(cond, mod_w, mod_b)
    rope = _rope_tables(n)

    ev_in0 = ev_w_in[:1].astype(BF16)
    late_cast = (ffn_w1, ffn_w2, ev_w_out, od_w_in, od_w_out, ev_w_in)
    g_mix = norm_mix_g.reshape(depth, 1, d)
    g_ffn = norm_ffn_g.reshape(depth, 1, d)
    kv_block = attn_w // (2 * kv_w)
    conv_col = attn_w + 2 * kv_w

    for l in range(depth):
        ctx_full = l < last_ctx_read
        if l % 2 == 0:
            e = l // 2
            w_in, wi = (ev_in0, 0) if e == 0 else (ev_in, e)
            p_l = _modmm(xs, g_mix, modv, l, 0, w_in, wi, TM_IN, TN_IN)
            q_l, k_l, v_l = _qkv_prep(p_l, 0, kv_block, ev_q_norm[e], ev_k_norm[e], rope, TM_PROJ)
            p_c = _modmm(xc, g_mix, modv, l, 1, w_in, wi, TM_IN, TN_IN)
            q_c, k_c, v_c = _qkv_prep(p_c, 0, kv_block, ev_q_norm[e], ev_k_norm[e], None, TM_PROJ)
            if e == 0:
                att_l, w1_all, w2_all, ev_out, od_in, od_out, ev_in = _attention(
                    q_l, k_l, v_l, k_c, v_c, TQ_ATTN, TK_ATTN, cast=late_cast)
            else:
                att_l, = _attention(q_l, k_l, v_l, k_c, v_c, TQ_ATTN, TK_ATTN)
            conv_l = _short_conv_mixer(p_l, conv_col, ev_conv_w[e])
            if ctx_full:
                att_c, = _attention(q_c, k_c, v_c, None, None, TQ_ATTN, TK_ATTN)
                conv_c = _short_conv_mixer(p_c, conv_col, ev_conv_w[e])
                xc, hc = _outproj([att_c, conv_c], ev_out, e, xc, modv, l, 1, g_ffn, TM_PROJ)
            xs, hs = _outproj([att_l, conv_l], ev_out, e, xs, modv, l, 0, g_ffn, TM_PROJ)
        else:
            o = l // 2
            fargs = (od_f_w1[o], od_f_b1[o], od_f_freq[o], od_f_w2[o], od_f_b2[o], od_f_w3[o], od_f_decay[o])
            p_l = _modmm(xs, g_mix, modv, l, 0, od_in, o, TM_IN, TN_IN, out_dtype=BF16)
            u, x0, usk = _hyena_prep(p_l, od_conv_w[o], od_conv_b[o], od_skip[o])
            filt2 = _hyena_filter(n, d, *fargs, tr=TR_FILT, tc=TC_FILT)
            y = _long_conv_gate(u, x0, usk, filt2, n)
            xs, hs = _outproj([y], od_out, o, xs, modv, l, 0, g_ffn, TM_PROJ)
            if ctx_full:
                p_c = _modmm(xc, g_mix, modv, l, 1, od_in, o, TM_IN, TN_IN, out_dtype=BF16)
                u, x0, usk = _hyena_prep(p_c, od_conv_w[o], od_conv_b[o], od_skip[o])
                filt2 = _hyena_filter(n_ctx, d, *fargs, tr=TR_FILT, tc=TC_FILT)
                y = _long_conv_gate_short(u, x0, usk, filt2, n_ctx)
                xc, hc = _outproj([y], od_out, o, xc, modv, l, 1, g_ffn, TM_PROJ)
        last = l == depth - 1
        xs = _ffn(xs, hs, modv, l, 0, w1_all, w2_all, TM_FFN, TF_FFN, final_gain=final_g if last else None)
        if ctx_full:
            xc = _ffn(xc, hc, modv, l, 1, w1_all, w2_all, TM_FFN, TF_FFN)

    return xs[None]
```

```python
import functools
import math

import numpy as np
import jax
import jax.numpy as jnp
from jax import lax
from jax.experimental import pallas as pl
from jax.experimental.pallas import tpu as pltpu

F32 = jnp.float32
BF16 = jnp.bfloat16

HEAD_DIM = 128
N_Q_HEADS = 8
N_KV_HEADS = 2
Q_PER_KV = N_Q_HEADS // N_KV_HEADS
GRID_W = 64
ROPE_THETA = 10000.0
FILTER_BANDS = 16
FILTER_MOD_SHIFT = 0.05
EPS = 1e-6

LANES = 128
SUBLANES = 8
VMEM_BUDGET = 56 * 1024 * 1024
DFT_N1 = 128
ONES_ROWS = 16

TN_MOD = 1536
TM_PROJ = 512
TM_IN = 1024
TN_IN = 768
TM_FFN = 512
TF_FFN = 1024
TQ_ATTN = 512
TK_ATTN = 2048
TR_FILT = 1024
TC_FILT = 1024


def _cparams(sem, vmem_bytes):
    return pltpu.CompilerParams(dimension_semantics=sem,
                                vmem_limit_bytes=int(min(max(vmem_bytes, 16 << 20), VMEM_BUDGET)))


def _nbytes(shape, dtype):
    return int(np.prod(shape)) * jnp.dtype(dtype).itemsize


def _dot(a, b):
    return jnp.dot(a, b, preferred_element_type=F32)


def _rms_rows(x):
    return x * lax.rsqrt(jnp.mean(x * x, axis=-1, keepdims=True) + EPS)


def _mod_kernel(c_ref, w_ref, b_ref, o_ref):
    c = c_ref[...]
    s = c * (1.0 / (1.0 + jnp.exp(-c)))
    o_ref[0] = _dot(s.astype(BF16), w_ref[0].astype(BF16)) + b_ref[0]


def _mod_vectors(cond, mod_w, mod_b):
    depth, d, n6 = mod_w.shape
    tn = TN_MOD
    rows = cond.shape[0]
    vm = 2 * _nbytes((d, tn), F32) + _nbytes((d, tn), BF16) + (4 << 20)
    return pl.pallas_call(
        _mod_kernel,
        grid=(depth, n6 // tn),
        in_specs=[pl.BlockSpec((rows, d), lambda l, j: (0, 0)),
                  pl.BlockSpec((1, d, tn), lambda l, j: (l, 0, j)),
                  pl.BlockSpec((1, 1, tn), lambda l, j: (l, 0, j))],
        out_specs=pl.BlockSpec((1, rows, tn), lambda l, j: (l, 0, j)),
        out_shape=jax.ShapeDtypeStruct((depth, rows, n6), F32),
        compiler_params=_cparams(("parallel", "parallel"), vm),
        name="adaln_vectors",
    )(cond, mod_w, mod_b.reshape(depth, 1, n6))


def _modmm_kernel(x_ref, g_ref, sh_ref, sc_ref, w_ref, o_ref, h_ref, *, row):
    @pl.when(pl.program_id(1) == 0)
    def _():
        h = _rms_rows(x_ref[...]) * g_ref[...]
        h = h * (1.0 + sc_ref[row:row + 1, :]) + sh_ref[row:row + 1, :]
        h_ref[...] = h.astype(BF16)

    o_ref[...] = _dot(h_ref[...], w_ref[...].astype(BF16)).astype(o_ref.dtype)


def _modmm(x, gains, modv, l, row, w, wi, tm, tn, out_dtype=F32):
    m, d = x.shape
    n = w.shape[2]
    tm = min(tm, m)
    vm = (2 * _nbytes((tm, d), F32) + _nbytes((tm, d), BF16) + 3 * _nbytes((d, tn), w.dtype)
          + 2 * _nbytes((tm, tn), F32) + 2 * _nbytes((tm, d), F32) + (4 << 20))
    return pl.pallas_call(
        functools.partial(_modmm_kernel, row=row),
        grid=(m // tm, n // tn),
        in_specs=[pl.BlockSpec((tm, d), lambda i, j: (i, 0)),
                  pl.BlockSpec((None, 1, d), lambda i, j: (l, 0, 0)),
                  pl.BlockSpec((None, SUBLANES, d), lambda i, j: (l, 0, 0)),
                  pl.BlockSpec((None, SUBLANES, d), lambda i, j: (l, 0, 1)),
                  pl.BlockSpec((None, d, tn), lambda i, j: (wi, 0, j))],
        out_specs=pl.BlockSpec((tm, tn), lambda i, j: (i, j)),
        out_shape=jax.ShapeDtypeStruct((m, n), out_dtype),
        scratch_shapes=[pltpu.VMEM((tm, d), BF16)],
        compiler_params=_cparams(("parallel", "arbitrary"), vm),
        name="modulated_projection",
    )(x, gains, modv, modv, w)


def _outproj_kernel(*refs, n_in, row):
    a_refs = refs[:n_in]
    w_refs = refs[n_in:2 * n_in]
    x_ref, gate_ref, g_ref, sh_ref, sc_ref, o_ref, h_ref = refs[2 * n_in:]
    acc = _dot(a_refs[0][...].astype(BF16), w_refs[0][...])
    for a_ref, w_ref in zip(a_refs[1:], w_refs[1:]):
        acc = acc + _dot(a_ref[...].astype(BF16), w_ref[...])
    x_new = x_ref[...] + gate_ref[row:row + 1, :] * acc
    o_ref[...] = x_new
    h = _rms_rows(x_new) * g_ref[...]
    h_ref[...] = (h * (1.0 + sc_ref[row:row + 1, :]) + sh_ref[row:row + 1, :]).astype(BF16)


def _outproj(acts, w, wi, x, modv, l, row, gains, tm):
    m, d = x.shape
    tm = min(tm, m)
    n_in = len(acts)
    in_specs, args = [], []
    for a in acts:
        in_specs.append(pl.BlockSpec((tm, a.shape[1]), lambda i: (i, 0)))
        args.append(a)
    row0 = 0
    for a in acts:
        k = a.shape[1]
        in_specs.append(pl.BlockSpec((None, k, d), functools.partial(lambda i, b: (wi, b, 0), b=row0 // k)))
        args.append(w)
        row0 += k
    modblk = lambda col: pl.BlockSpec((None, SUBLANES, d), lambda i: (l, 0, col))
    in_specs += [pl.BlockSpec((tm, d), lambda i: (i, 0)), modblk(2),
                 pl.BlockSpec((None, 1, d), lambda i: (l, 0, 0)), modblk(3), modblk(4)]
    args += [x, modv, gains, modv, modv]
    vm = (2 * sum(_nbytes((tm, a.shape[1]), a.dtype) for a in acts) + 2 * _nbytes(w.shape[1:], BF16)
          + 8 * _nbytes((tm, d), F32) + (4 << 20))
    rowblk = pl.BlockSpec((tm, d), lambda i: (i, 0))
    return pl.pallas_call(
        functools.partial(_outproj_kernel, n_in=n_in, row=row),
        grid=(m // tm,),
        in_specs=in_specs,
        out_specs=[rowblk, rowblk],
        out_shape=[jax.ShapeDtypeStruct((m, d), F32), jax.ShapeDtypeStruct((m, d), BF16)],
        compiler_params=_cparams(("parallel",), vm),
        name="gated_out_projection",
    )(*args)


def _ffn_kernel(*refs, row, final_norm):
    if final_norm:
        x_ref, h_ref, gate_ref, w1_ref, w2_ref, fg_ref, o_ref = refs
    else:
        x_ref, h_ref, gate_ref, w1_ref, w2_ref, o_ref = refs
    j = pl.program_id(1)

    @pl.when(j == 0)
    def _():
        o_ref[...] = jnp.zeros(o_ref.shape, F32)

    a = jnp.maximum(_dot(h_ref[...], w1_ref[...]), 0.0)
    o_ref[...] += _dot((a * a).astype(BF16), w2_ref[...])

    @pl.when(j == pl.num_programs(1) - 1)
    def _():
        y = x_ref[...] + gate_ref[row:row + 1, :] * o_ref[...]
        if final_norm:
            y = _rms_rows(y) * fg_ref[...]
        o_ref[...] = y


def _ffn(x, h, modv, l, row, w1, w2, tm, tf, final_gain=None):
    m, d = x.shape
    f = w1.shape[2]
    tm = min(tm, m)
    vm = (4 * _nbytes((tm, d), F32) + 2 * _nbytes((tm, d), BF16)
          + 4 * _nbytes((d, tf), BF16) + 3 * _nbytes((tm, tf), F32) + (6 << 20))
    in_specs = [pl.BlockSpec((tm, d), lambda i, j: (i, 0)),
                pl.BlockSpec((tm, d), lambda i, j: (i, 0)),
                pl.BlockSpec((None, SUBLANES, d), lambda i, j: (l, 0, 5)),
                pl.BlockSpec((None, d, tf), lambda i, j: (l, 0, j)),
                pl.BlockSpec((None, tf, d), lambda i, j: (l, j, 0))]
    args = [x, h, modv, w1, w2]
    if final_gain is not None:
        in_specs.append(pl.BlockSpec((1, d), lambda i, j: (0, 0)))
        args.append(final_gain.reshape(1, d))
    return pl.pallas_call(
        functools.partial(_ffn_kernel, row=row, final_norm=final_gain is not None),
        grid=(m // tm, f // tf),
        in_specs=in_specs,
        out_specs=pl.BlockSpec((tm, d), lambda i, j: (i, 0)),
        out_shape=jax.ShapeDtypeStruct((m, d), F32),
        compiler_params=_cparams(("parallel", "arbitrary"), vm),
        name="gated_ffn",
    )(*args)


def _rope(x, cos, sin):
    lane = lax.broadcasted_iota(jnp.int32, x.shape, 1)
    first_half = (lane % (HEAD_DIM // 2)) < (HEAD_DIM // 4)
    partner = jnp.where(first_half,
                        pltpu.roll(x, HEAD_DIM - HEAD_DIM // 4, 1),
                        pltpu.roll(x, HEAD_DIM // 4, 1))
    return x * cos + partner * sin


def _qkv_kernel(*refs, use_rope, q_scale):
    if use_rope:
        pq_ref, pkv_ref, qg_ref, kg_ref, cos_ref, sin_ref, q_ref, k_ref, v_ref = refs
        cos, sin = cos_ref[...], sin_ref[...]
    else:
        pq_ref, pkv_ref, qg_ref, kg_ref, q_ref, k_ref, v_ref = refs
    kv_w = N_KV_HEADS * HEAD_DIM
    for h in range(N_Q_HEADS):
        sl = slice(h * HEAD_DIM, (h + 1) * HEAD_DIM)
        xn = _rms_rows(pq_ref[:, sl]) * qg_ref[...]
        if use_rope:
            xn = _rope(xn, cos, sin)
        q_ref[:, sl] = (xn * q_scale).astype(BF16)
    for h in range(N_KV_HEADS):
        sl = slice(h * HEAD_DIM, (h + 1) * HEAD_DIM)
        xn = _rms_rows(pkv_ref[:, sl]) * kg_ref[...]
        if use_rope:
            xn = _rope(xn, cos, sin)
        k_ref[:, sl] = xn.astype(BF16)
    v_ref[...] = pkv_ref[:, kv_w:2 * kv_w].T.astype(BF16)


def _qkv_prep(p, q_col_block, kv_col_block, q_gain, k_gain, rope, tm):
    m = p.shape[0]
    tm = min(tm, m)
    attn_w = N_Q_HEADS * HEAD_DIM
    kv_w = N_KV_HEADS * HEAD_DIM
    in_specs = [pl.BlockSpec((tm, attn_w), lambda i: (i, q_col_block)),
                pl.BlockSpec((tm, 2 * kv_w), lambda i: (i, kv_col_block)),
                pl.BlockSpec((1, HEAD_DIM), lambda i: (0, 0)),
                pl.BlockSpec((1, HEAD_DIM), lambda i: (0, 0))]
    args = [p, p, q_gain.reshape(1, HEAD_DIM), k_gain.reshape(1, HEAD_DIM)]
    if rope is not None:
        in_specs += [pl.BlockSpec((tm, HEAD_DIM), lambda i: (i, 0))] * 2
        args += list(rope)
    vm = 4 * _nbytes((tm, attn_w + 2 * kv_w), F32) + (8 << 20)
    return pl.pallas_call(
        functools.partial(_qkv_kernel, use_rope=rope is not None, q_scale=HEAD_DIM ** -0.5),
        grid=(m // tm,),
        in_specs=in_specs,
        out_specs=[pl.BlockSpec((tm, attn_w), lambda i: (i, 0)),
                   pl.BlockSpec((tm, kv_w), lambda i: (i, 0)),
                   pl.BlockSpec((kv_w, tm), lambda i: (0, i))],
        out_shape=[jax.ShapeDtypeStruct((m, attn_w), BF16),
                   jax.ShapeDtypeStruct((m, kv_w), BF16),
                   jax.ShapeDtypeStruct((kv_w, m), BF16)],
        compiler_params=_cparams(("parallel",), vm),
        name="qkv_norm_rope",
    )(*args)


def _attn_kernel(*refs, has_extra, tq, n_cast):
    n_in = 5 if has_extra else 3
    cast_in = refs[n_in:n_in + n_cast]
    o_ref = refs[n_in + n_cast]
    cast_out = refs[n_in + n_cast + 1:n_in + 2 * n_cast + 1]
    qs_ref, m_ref, acc_ref = refs[n_in + 2 * n_cast + 1:]
    if has_extra:
        q_ref, k_ref, vt_ref, ke_ref, vte_ref = refs[:n_in]
    else:
        q_ref, k_ref, vt_ref = refs[:n_in]
    kv = pl.program_id(2)

    for src, dst in zip(cast_in, cast_out):
        dst[...] = src[...].astype(BF16)

    def update(k, vt):
        vt1 = jnp.concatenate([vt, jnp.ones((ONES_ROWS, vt.shape[1]), BF16)], axis=0)
        cols = [slice(h * tq, (h + 1) * tq) for h in range(Q_PER_KV)]
        s, p, alpha = {}, {}, {}
        for t in range(Q_PER_KV + 2):
            if t < Q_PER_KV:
                s[t] = lax.dot_general(k, qs_ref[cols[t], :], (((1,), (1,)), ((), ())),
                                       preferred_element_type=F32)
            h = t - 1
            if 0 <= h < Q_PER_KV:
                m_prev = m_ref[:, cols[h]]
                m_new = jnp.maximum(m_prev, jnp.max(s[h], axis=0, keepdims=True))
                alpha[h] = jnp.exp(m_prev - m_new)
                p[h] = jnp.exp(s.pop(h) - m_new).astype(BF16)
                m_ref[:, cols[h]] = m_new
            h = t - 2
            if 0 <= h < Q_PER_KV:
                acc_ref[:, cols[h]] = alpha.pop(h) * acc_ref[:, cols[h]] + _dot(vt1, p.pop(h))

    @pl.when(kv == 0)
    def _():
        for h in range(Q_PER_KV):
            qs_ref[h * tq:(h + 1) * tq, :] = q_ref[:, h * HEAD_DIM:(h + 1) * HEAD_DIM]
        m_ref[...] = jnp.full(m_ref.shape, -jnp.inf, F32)
        acc_ref[...] = jnp.zeros(acc_ref.shape, F32)
        if has_extra:
            update(ke_ref[...], vte_ref[...])

    update(k_ref[...], vt_ref[...])

    @pl.when(kv == pl.num_programs(2) - 1)
    def _():
        o_t = acc_ref[0:HEAD_DIM, :] / acc_ref[HEAD_DIM:HEAD_DIM + 1, :]
        for h in range(Q_PER_KV):
            o_ref[:, h * HEAD_DIM:(h + 1) * HEAD_DIM] = o_t[:, h * tq:(h + 1) * tq].T.astype(o_ref.dtype)


def _attention(q, k, vt, k_extra, vt_extra, tq, tk, cast=()):
    m = q.shape[0]
    s_len = k.shape[0]
    tq = min(tq, m)
    tk = min(tk, s_len)
    gw = Q_PER_KV * HEAD_DIM
    cols = Q_PER_KV * tq
    has_extra = k_extra is not None
    n_i, n_j = m // tq, s_len // tk
    steps = N_KV_HEADS * n_i * n_j
    in_specs = [pl.BlockSpec((tq, gw), lambda g, i, j: (i, g)),
                pl.BlockSpec((tk, HEAD_DIM), lambda g, i, j: (j, g)),
                pl.BlockSpec((HEAD_DIM, tk), lambda g, i, j: (g, j))]
    args = [q, k, vt]
    if has_extra:
        e = k_extra.shape[0]
        in_specs += [pl.BlockSpec((e, HEAD_DIM), lambda g, i, j: (0, g)),
                     pl.BlockSpec((HEAD_DIM, e), lambda g, i, j: (g, 0))]
        args += [k_extra, vt_extra]
    out_specs = [pl.BlockSpec((tq, gw), lambda g, i, j: (i, g))]
    out_shape = [jax.ShapeDtypeStruct((m, N_Q_HEADS * HEAD_DIM), BF16)]
    for w in cast:
        lanes = w.shape[-1]
        rows = w.size // (steps * lanes)
        assert rows * steps * lanes == w.size and rows % (2 * SUBLANES) == 0
        slab = pl.BlockSpec((None, rows, lanes), lambda g, i, j: ((g * n_i + i) * n_j + j, 0, 0))
        in_specs.append(slab)
        args.append(w.reshape(steps, rows, lanes))
        out_specs.append(slab)
        out_shape.append(jax.ShapeDtypeStruct((steps, rows, lanes), BF16))
    acc_rows = HEAD_DIM + ONES_ROWS
    vm = 4 * _nbytes((tk, cols), F32) + 6 * _nbytes((acc_rows, cols), F32) + (8 << 20)
    outs = pl.pallas_call(
        functools.partial(_attn_kernel, has_extra=has_extra, tq=tq, n_cast=len(cast)),
        grid=(N_KV_HEADS, n_i, n_j),
        in_specs=in_specs,
        out_specs=out_specs,
        out_shape=out_shape,
        scratch_shapes=[pltpu.VMEM((cols, HEAD_DIM), BF16),
                        pltpu.VMEM((1, cols), F32),
                        pltpu.VMEM((acc_rows, cols), F32)],
        compiler_params=_cparams(("parallel", "parallel", "arbitrary"), vm),
        name="gqa_flash_attention",
    )(*args)
    return (outs[0],) + tuple(o.reshape(w.shape) for o, w in zip(outs[1:], cast))


def _conv3(z, w_ref):
    n = z.shape[0]
    prev = pltpu.roll(z, 1, 0)
    nxt = pltpu.roll(z, n - 1, 0)
    row = lax.broadcasted_iota(jnp.int32, (SUBLANES, z.shape[1]), 0)
    prev = jnp.concatenate([jnp.where(row == 0, 0.0, prev[:SUBLANES]), prev[SUBLANES:]], axis=0)
    nxt = jnp.concatenate([nxt[:n - SUBLANES], jnp.where(row == SUBLANES - 1, 0.0, nxt[n - SUBLANES:])], axis=0)
    return prev * w_ref[0:1, :] + z * w_ref[1:2, :] + nxt * w_ref[2:3, :]


def _convmix_kernel(u_ref, b_ref, c_ref, w_ref, o_ref):
    o_ref[...] = (b_ref[...] * _conv3(c_ref[...] * u_ref[...], w_ref)).astype(o_ref.dtype)


def _short_conv_mixer(p, col0, conv_w):
    m = p.shape[0]
    c = conv_w.shape[1]
    ct = LANES
    nb = c // ct
    b0 = col0 // ct
    vm = 8 * _nbytes((m, ct), F32) + 8 * _nbytes((m, ct), F32) + (4 << 20)
    return pl.pallas_call(
        _convmix_kernel,
        grid=(nb,),
        in_specs=[pl.BlockSpec((m, ct), lambda j: (0, b0 + j)),
                  pl.BlockSpec((m, ct), lambda j: (0, b0 + nb + j)),
                  pl.BlockSpec((m, ct), lambda j: (0, b0 + 2 * nb + j)),
                  pl.BlockSpec((3, ct), lambda j: (0, j))],
        out_specs=pl.BlockSpec((m, ct), lambda j: (0, j)),
        out_shape=jax.ShapeDtypeStruct((m, c), BF16),
        compiler_params=_cparams(("parallel",), vm),
        name="short_conv_mixer",
    )(p, p, p, conv_w)


def _conv3_staged(z, w_ref, pad_ref):
    n = z.shape[0]
    zeros = jnp.zeros((SUBLANES, z.shape[1]), F32)
    pad_ref[0:SUBLANES, :] = zeros
    pad_ref[SUBLANES:n + SUBLANES, :] = z
    pad_ref[n + SUBLANES:n + 2 * SUBLANES, :] = zeros
    prev = pad_ref[SUBLANES - 1:n + SUBLANES - 1, :]
    nxt = pad_ref[SUBLANES + 1:n + SUBLANES + 1, :]
    return prev * w_ref[0:1, :] + z * w_ref[1:2, :] + nxt * w_ref[2:3, :]


def _hyena_prep_kernel(p0_ref, p1_ref, p2_ref, w0_ref, w1_ref, w2_ref, b0_ref, b1_ref, b2_ref,
                       skip_ref, u_ref, x0_ref, usk_ref, pad0_ref, pad1_ref, pad2_ref):
    x0 = _conv3_staged(p0_ref[...].astype(F32), w0_ref, pad0_ref) + b0_ref[...]
    x1 = _conv3_staged(p1_ref[...].astype(F32), w1_ref, pad1_ref) + b1_ref[...]
    v = _conv3_staged(p2_ref[...].astype(F32), w2_ref, pad2_ref) + b2_ref[...]
    u = x1 * v
    u_ref[...] = u.astype(u_ref.dtype)
    x0_ref[...] = x0.astype(x0_ref.dtype)
    usk_ref[...] = (x0 * (u * skip_ref[...])).astype(usk_ref.dtype)


def _hyena_prep(p, conv_w, conv_b, skip):
    m = p.shape[0]
    c = skip.shape[0]
    ct = LANES
    nb = c // ct
    vm = 12 * _nbytes((m, ct), F32) + 10 * _nbytes((m, ct), F32) + (4 << 20)
    blk = lambda off: pl.BlockSpec((m, ct), functools.partial(lambda j, o: (0, o + j), o=off))
    wblk = lambda off: pl.BlockSpec((3, ct), functools.partial(lambda j, o: (0, o + j), o=off))
    bblk = lambda off: pl.BlockSpec((1, ct), functools.partial(lambda j, o: (0, o + j), o=off))
    out = jax.ShapeDtypeStruct((m, c), BF16)
    return pl.pallas_call(
        _hyena_prep_kernel,
        grid=(nb,),
        in_specs=[blk(0), blk(nb), blk(2 * nb), wblk(0), wblk(nb), wblk(2 * nb),
                  bblk(0), bblk(nb), bblk(2 * nb), pl.BlockSpec((1, ct), lambda j: (0, j))],
        out_specs=[pl.BlockSpec((m, ct), lambda j: (0, j))] * 3,
        out_shape=[out, out, out],
        scratch_shapes=[pltpu.VMEM((m + 2 * SUBLANES, ct), F32)] * 3,
        compiler_params=_cparams(("parallel",), vm),
        name="hyena_short_conv_gate",
    )(p, p, p, conv_w, conv_w, conv_w, conv_b.reshape(1, -1), conv_b.reshape(1, -1),
      conv_b.reshape(1, -1), skip.reshape(1, c))


def _filter_feats_t(n, order):
    t = np.asarray(order, np.float64)
    t_norm = t / max(n - 1, 1)
    bands = np.arange(1, FILTER_BANDS + 1, dtype=np.float64)
    ang = (2 * math.pi / n) * bands[:, None] * t[None, :]
    feats = np.concatenate([t_norm[None, :], np.cos(ang), np.sin(ang)], axis=0)
    out = np.zeros((LANES, len(t)), np.float32)
    out[:feats.shape[0]] = feats
    return out


def _split_bf16(a):
    hi = a.astype(BF16).astype(F32)
    return hi, a - hi


def _dot_split(a, b):
    a_hi, a_lo = _split_bf16(a)
    b_hi, b_lo = _split_bf16(b)
    lhs = jnp.concatenate([a_hi, a_hi, a_lo], axis=1).astype(BF16)
    rhs = jnp.concatenate([b_hi, b_lo, b_hi], axis=0).astype(BF16)
    return _dot(lhs, rhs)


def _filter_kernel(f_ref, w1_ref, b1_ref, fr_ref, w2_ref, b2_ref, w3_ref, dec_ref, o_ref, hs_ref, tn_ref,
                   *, n):
    tr = tn_ref.shape[0]
    hid = w3_ref.shape[0]

    @pl.when(pl.program_id(2) == 0)
    def _():
        fr = fr_ref[...]
        h = jnp.sin(fr * (_dot_split(w1_ref[...], f_ref[...]) + b1_ref[...]))
        h = jnp.concatenate([h, jnp.zeros((LANES - hid, tr), F32)], axis=0)
        h = jnp.sin(fr * (_dot_split(w2_ref[...], h) + b2_ref[...]))
        hi, lo = _split_bf16(h)
        pad = jnp.zeros((2 * LANES - 3 * hid, tr), F32)
        hs_ref[...] = jnp.concatenate([hi, hi, lo, pad], axis=0).T.astype(BF16)
        pos = pl.program_id(1) * tr + lax.broadcasted_iota(jnp.int32, (tr, LANES), 0)
        pos = jnp.where(pl.program_id(0) == 0, pos, n - 1 - pos)
        tn_ref[...] = pos.astype(F32) / float(max(n - 1, 1))

    w_hi, w_lo = _split_bf16(w3_ref[...])
    pad = jnp.zeros((2 * LANES - 3 * hid, w_hi.shape[1]), F32)
    rhs = jnp.concatenate([w_hi, w_lo, w_hi, pad], axis=0).astype(BF16)
    hw = _dot(hs_ref[...], rhs)
    t_norm = jnp.concatenate([tn_ref[...]] * (hw.shape[1] // LANES), axis=1)
    window = jnp.exp(-t_norm * jnp.abs(dec_ref[...])) + FILTER_MOD_SHIFT
    o_ref[...] = (hw * window).astype(o_ref.dtype)


def _hyena_filter(n, ch, f_w1, f_b1, f_freq, f_w2, f_b2, f_w3, f_decay, tr, tc):
    hid = f_w1.shape[1]
    assert 3 * hid <= 2 * LANES and hid <= LANES
    feats = np.stack([_filter_feats_t(n, np.arange(n)), _filter_feats_t(n, np.arange(n)[::-1])])
    w1t = jnp.zeros((hid, LANES), F32).at[:, :f_w1.shape[0]].set(f_w1.T)
    w2t = jnp.zeros((hid, LANES), F32).at[:, :hid].set(f_w2.T)
    tr = min(tr, n)
    nrb = n // tr
    ncb = ch // tc
    vm = 6 * _nbytes((tr, tc), F32) + 4 * _nbytes((LANES, tr), F32) + (8 << 20)
    col = lambda a: a.reshape(hid, 1)
    small = lambda shape: pl.BlockSpec(shape, lambda s, i, j: (0, 0))
    return pl.pallas_call(
        functools.partial(_filter_kernel, n=n),
        grid=(2, nrb, ncb),
        in_specs=[pl.BlockSpec((None, LANES, tr), lambda s, i, j: (s, 0, i)),
                  small((hid, LANES)), small((hid, 1)), small((hid, 1)),
                  small((hid, LANES)), small((hid, 1)),
                  pl.BlockSpec((hid, tc), lambda s, i, j: (0, s * ncb + j)),
                  pl.BlockSpec((1, tc), lambda s, i, j: (0, s * ncb + j))],
        out_specs=pl.BlockSpec((tr, tc), lambda s, i, j: (s * nrb + i, j)),
        out_shape=jax.ShapeDtypeStruct((2 * n, ch), BF16),
        scratch_shapes=[pltpu.VMEM((tr, 2 * LANES), BF16), pltpu.VMEM((tr, LANES), F32)],
        compiler_params=_cparams(("parallel", "parallel", "arbitrary"), vm),
        name="hyena_filter",
    )(jnp.asarray(feats), w1t, col(f_b1), col(f_freq), w2t, col(f_b2), f_w3, f_decay.reshape(1, -1))


def _dft_geometry(n):
    big_n = 2 * n
    n2 = big_n // DFT_N1
    nh = n2 // 2 + 1
    return big_n, n2, nh


def _k2_block(nh, limit=13):
    return max(k for k in range(1, limit + 1) if nh % k == 0)


@functools.lru_cache(maxsize=None)
def _stage1_matrix(n, rows_n2):
    big_n, _, nh = _dft_geometry(n)
    groups = DFT_N1 // SUBLANES
    out = np.zeros((groups, 2, nh, SUBLANES, rows_n2, SUBLANES), np.float32)
    k2 = np.arange(nh, dtype=np.float64)[:, None]
    n2v = np.arange(rows_n2, dtype=np.float64)[None, :]
    for g in range(groups):
        for j in range(SUBLANES):
            t = SUBLANES * g + j + DFT_N1 * n2v
            ang = 2 * math.pi * np.mod(k2 * t, big_n) / big_n
            out[g, 0, :, j, :, j] = np.cos(ang)
            out[g, 1, :, j, :, j] = -np.sin(ang)
    return out.reshape(groups, 2 * nh * SUBLANES, rows_n2 * SUBLANES).astype(BF16)


@functools.lru_cache(maxsize=None)
def _stage2_matrices():
    idx = np.arange(DFT_N1, dtype=np.float64)
    ang = 2 * math.pi * np.mod(np.outer(idx, idx), DFT_N1) / DFT_N1
    c, s = np.cos(ang), np.sin(ang)
    fwd = np.block([[c, s], [-s, c]])
    inv = np.block([[c, -s], [s, c]])
    return fwd.astype(BF16), inv.astype(BF16)


@functools.lru_cache(maxsize=None)
def _stage4_matrix(n):
    big_n, n2c, nh = _dft_geometry(n)
    groups = DFT_N1 // SUBLANES
    rows_n2 = n2c // 2
    out = np.zeros((groups, rows_n2, SUBLANES, 2, nh, SUBLANES), np.float32)
    k2 = np.arange(nh, dtype=np.float64)[None, :]
    wgt = np.where(k2 <= n2c // 2, 2.0, 0.0)
    wgt[0, 0] = 1.0
    wgt[0, n2c // 2] = 1.0
    n2v = np.arange(rows_n2, dtype=np.float64)[:, None]
    for g in range(groups):
        for j in range(SUBLANES):
            t = SUBLANES * g + j + DFT_N1 * n2v
            ang = 2 * math.pi * np.mod(k2 * t, big_n) / big_n
            out[g, :, j, 0, :, j] = wgt * np.cos(ang) / big_n
            out[g, :, j, 1, :, j] = -wgt * np.sin(ang) / big_n
    return out.reshape(groups, rows_n2 * SUBLANES, 2 * nh * SUBLANES).astype(BF16)


def _dft_stage1_kernel(x_ref, l_ref, o_ref, *, gsteps):
    nb, _, ct = x_ref.shape
    rows = o_ref.shape[0]
    xin = x_ref[...].astype(F32)
    outs = []
    for gi in range(gsteps):
        sl = slice(SUBLANES * gi, SUBLANES * (gi + 1))
        xv = xin[:, sl, :].reshape(nb * SUBLANES, ct).astype(BF16)
        outs.append(_dot(l_ref[gi], xv).reshape(rows, SUBLANES, ct))
    o_ref[...] = jnp.concatenate(outs, axis=1).astype(o_ref.dtype)


def _dft_stage1(x3, n, gsteps, ct):
    rows_n2, _, c = x3.shape
    _, _, nh = _dft_geometry(n)
    mat = _stage1_matrix(n, rows_n2)
    groups = mat.shape[0]
    ct = min(ct, c)
    gw = SUBLANES * gsteps
    vm = (2 * _nbytes((rows_n2, gw, ct), F32) + 2 * _nbytes((gsteps,) + mat.shape[1:], BF16)
          + 2 * _nbytes((2 * nh, gw, ct), F32) + 3 * _nbytes((2 * nh * SUBLANES, ct), F32) + (4 << 20))
    return pl.pallas_call(
        functools.partial(_dft_stage1_kernel, gsteps=gsteps),
        grid=(groups // gsteps, c // ct),
        in_specs=[pl.BlockSpec((rows_n2, gw, ct), lambda g, j: (0, g, j)),
                  pl.BlockSpec((gsteps,) + mat.shape[1:], lambda g, j: (g, 0, 0))],
        out_specs=pl.BlockSpec((2 * nh, gw, ct), lambda g, j: (0, g, j)),
        out_shape=jax.ShapeDtypeStruct((2 * nh, DFT_N1, c), BF16),
        compiler_params=_cparams(("parallel", "parallel"), vm),
        name="dft_stage_n2",
    )(x3, mat)


def _stack_parts(a_ref, kb):
    cols = [jnp.concatenate([a_ref[0, k], a_ref[1, k]], axis=0) for k in range(kb)]
    return jnp.concatenate(cols, axis=1).astype(BF16)


def _store_parts(o_ref, x, kb):
    ct = o_ref.shape[-1]
    for k in range(kb):
        o_ref[0, k] = x[:DFT_N1, k * ct:(k + 1) * ct].astype(o_ref.dtype)
        o_ref[1, k] = x[DFT_N1:, k * ct:(k + 1) * ct].astype(o_ref.dtype)


def _spectral_product_kernel(a_ref, fa_ref, l2_ref, l3_ref, o_ref, *, kb):
    ct = a_ref.shape[-1]
    x = _dot(l2_ref[...], _stack_parts(a_ref, kb))
    f = _dot(l2_ref[...], _stack_parts(fa_ref, kb))
    ys = []
    for k in range(kb):
        cols = slice(k * ct, (k + 1) * ct)
        xr, xi = x[:DFT_N1, cols], x[DFT_N1:, cols]
        kr, ki = f[:DFT_N1, cols], f[DFT_N1:, cols]
        ys.append(jnp.concatenate([xr * kr - xi * ki, xr * ki + xi * kr], axis=0))
    _store_parts(o_ref, _dot(l3_ref[...], jnp.concatenate(ys, axis=1).astype(BF16)), kb)


def _dft_stage2(a4, fa4, kb, ct):
    _, nh, _, c = a4.shape
    fwd, inv = _stage2_matrices()
    ct = min(ct, c)
    blk = pl.BlockSpec((2, kb, DFT_N1, ct), lambda g, j: (0, g, 0, j))
    mblk = pl.BlockSpec((2 * DFT_N1, 2 * DFT_N1), lambda g, j: (0, 0))
    vm = 6 * _nbytes((2, kb, DFT_N1, ct), BF16) + 8 * _nbytes((2 * DFT_N1, kb * ct), F32) + (4 << 20)
    return pl.pallas_call(
        functools.partial(_spectral_product_kernel, kb=kb),
        grid=(nh // kb, c // ct),
        in_specs=[blk, blk, mblk, mblk],
        out_specs=blk,
        out_shape=jax.ShapeDtypeStruct(a4.shape, BF16),
        compiler_params=_cparams(("parallel", "parallel"), vm),
        name="dft_stage_n1_product",
    )(a4, fa4, fwd, inv)


def _dft_stage4_kernel(v_ref, l_ref, x0_ref, usk_ref, o_ref, *, gsteps):
    rows, _, ct = v_ref.shape
    nb = o_ref.shape[0]
    vin = v_ref[...].astype(F32)
    ys = []
    for gi in range(gsteps):
        sl = slice(SUBLANES * gi, SUBLANES * (gi + 1))
        vv = vin[:, sl, :].reshape(rows * SUBLANES, ct).astype(BF16)
        ys.append(_dot(l_ref[gi], vv).reshape(nb, SUBLANES, ct))
    y = jnp.concatenate(ys, axis=1)
    o_ref[...] = (x0_ref[...].astype(F32) * y + usk_ref[...].astype(F32)).astype(o_ref.dtype)


def _dft_stage4(v3, x03, usk3, n, gsteps, ct):
    rows, _, c = v3.shape
    mat = _stage4_matrix(n)
    groups = mat.shape[0]
    nb = x03.shape[0]
    ct = min(ct, c)
    gw = SUBLANES * gsteps
    vm = (2 * _nbytes((rows, gw, ct), F32) + 2 * _nbytes((gsteps,) + mat.shape[1:], BF16)
          + 6 * _nbytes((nb, gw, ct), F32) + 3 * _nbytes((rows * SUBLANES, ct), F32) + (4 << 20))
    oblk = pl.BlockSpec((nb, gw, ct), lambda g, j: (0, g, j))
    return pl.pallas_call(
        functools.partial(_dft_stage4_kernel, gsteps=gsteps),
        grid=(groups // gsteps, c // ct),
        in_specs=[pl.BlockSpec((rows, gw, ct), lambda g, j: (0, g, j)),
                  pl.BlockSpec((gsteps,) + mat.shape[1:], lambda g, j: (g, 0, 0)),
                  oblk, oblk],
        out_specs=oblk,
        out_shape=jax.ShapeDtypeStruct((nb, DFT_N1, c), BF16),
        compiler_params=_cparams(("parallel", "parallel"), vm),
        name="dft_stage_k2_inverse",
    )(v3, mat, x03, usk3)


def _long_conv_gate(u, x0, usk, filt2, n):
    c = u.shape[1]
    _, _, nh = _dft_geometry(n)
    as3 = lambda a: a.reshape(a.shape[0] // DFT_N1, DFT_N1, c)
    fa = _dft_stage1(as3(filt2), n, gsteps=2, ct=512)
    a = _dft_stage1(as3(u), n, gsteps=2, ct=512)
    v = _dft_stage2(a.reshape(2, nh, DFT_N1, c), fa.reshape(2, nh, DFT_N1, c), kb=_k2_block(nh), ct=256)
    y = _dft_stage4(v.reshape(2 * nh, DFT_N1, c), as3(x0), as3(usk), n, gsteps=2, ct=512)
    return y.reshape(n, c)


@functools.lru_cache(maxsize=None)
def _dense_dft_matrices(n):
    big_n = 2 * n
    nf = n + 1
    nfp = -(-nf // SUBLANES) * SUBLANES
    k = np.arange(nf, dtype=np.float64)[:, None]
    t = np.arange(big_n, dtype=np.float64)[None, :]
    ang = 2 * math.pi * np.mod(k * t, big_n) / big_n
    fwd = np.zeros((2, nfp, big_n), np.float32)
    fwd[0, :nf] = np.cos(ang)
    fwd[1, :nf] = -np.sin(ang)
    wgt = np.full((nf, 1), 2.0)
    wgt[0] = 1.0
    wgt[n] = 1.0
    inv = np.zeros((2, nfp, n), np.float32)
    inv[0, :nf] = (wgt * np.cos(ang) / big_n)[:, :n]
    inv[1, :nf] = (-wgt * np.sin(ang) / big_n)[:, :n]
    fwd = fwd.reshape(2 * nfp, big_n)
    inv = inv.reshape(2 * nfp, n).T
    return (np.ascontiguousarray(fwd[:, :n]).astype(BF16), fwd.astype(BF16),
            np.ascontiguousarray(inv).astype(BF16))


def _short_conv_gate_kernel(u_ref, f_ref, x0_ref, usk_ref, lu_ref, lf_ref, li_ref, o_ref):
    half = lu_ref.shape[0] // 2
    xs = _dot(lu_ref[...], u_ref[...].astype(BF16))
    ks = _dot(lf_ref[...], f_ref[...].astype(BF16))
    xr, xi, kr, ki = xs[:half], xs[half:], ks[:half], ks[half:]
    y = jnp.concatenate([xr * kr - xi * ki, xr * ki + xi * kr], axis=0).astype(BF16)
    conv = _dot(li_ref[...], y)
    o_ref[...] = (x0_ref[...].astype(F32) * conv + usk_ref[...].astype(F32)).astype(o_ref.dtype)


def _long_conv_gate_short(u, x0, usk, filt2, n):
    c = u.shape[1]
    lu, lf, li = _dense_dft_matrices(n)
    ct = 256
    blk = pl.BlockSpec((n, ct), lambda j: (0, j))
    full = lambda a: pl.BlockSpec(a.shape, lambda j: (0, 0))
    vm = 32 << 20
    return pl.pallas_call(
        _short_conv_gate_kernel,
        grid=(c // ct,),
        in_specs=[blk, pl.BlockSpec((2 * n, ct), lambda j: (0, j)), blk, blk, full(lu), full(lf), full(li)],
        out_specs=blk,
        out_shape=jax.ShapeDtypeStruct((n, c), BF16),
        compiler_params=_cparams(("parallel",), vm),
        name="context_long_conv",
    )(u, filt2, x0, usk, lu, lf, li)


@functools.lru_cache(maxsize=None)
def _rope_tables(n):
    t = np.arange(n)
    row = (t // GRID_W).astype(np.float64)
    col = (t % GRID_W).astype(np.float64)
    axis_dim = HEAD_DIM // 2
    inv = 1.0 / (ROPE_THETA ** (np.arange(0, axis_dim, 2, dtype=np.float64) / axis_dim))
    ar, ac = row[:, None] * inv, col[:, None] * inv
    cos = np.concatenate([np.cos(ar), np.cos(ar), np.cos(ac), np.cos(ac)], axis=-1)
    sin = np.concatenate([-np.sin(ar), np.sin(ar), -np.sin(ac), np.sin(ac)], axis=-1)
    return cos.astype(np.float32), sin.astype(np.float32)


def kernel(x, c, ctx, c_ctx, mod_w, mod_b, norm_mix_g, norm_ffn_g, ffn_w1, ffn_w2, ev_w_in, ev_q_norm, ev_k_norm, ev_conv_w, ev_w_out, od_w_in, od_conv_w, od_conv_b, od_f_w1, od_f_b1, od_f_freq, od_f_w2, od_f_b2, od_f_w3, od_f_decay, od_skip, od_w_out, final_g):
    batch, n, d = x.shape
    assert batch == 1
    n_ctx = ctx.shape[1]
    depth = mod_w.shape[0]
    last_ctx_read = ((depth - 1) // 2) * 2
    attn_w = N_Q_HEADS * HEAD_DIM
    kv_w = N_KV_HEADS * HEAD_DIM

    xs = x[0]
    xc = ctx[0]
    cond = jnp.zeros((SUBLANES, d), F32).at[0].set(c[0]).at[1].set(c_ctx)
    modv = _mod_vectors(cond, mod_w, mod_b)
    rope = _rope_tables(n)

    late_cast = (ffn_w1, ffn_w2, ev_w_out, od_w_in, od_w_out, ev_w_in)
    g_mix = norm_mix_g.reshape(depth, 1, d)
    g_ffn = norm_ffn_g.reshape(depth, 1, d)
    kv_block = attn_w // (2 * kv_w)
    conv_col = attn_w + 2 * kv_w

    for l in range(depth):
        ctx_full = l < last_ctx_read
        if l % 2 == 0:
            e = l // 2
            w_in = ev_w_in if e == 0 else ev_in
            p_l = _modmm(xs, g_mix, modv, l, 0, w_in, e, TM_IN, TN_IN)
            q_l, k_l, v_l = _qkv_prep(p_l, 0, kv_block, ev_q_norm[e], ev_k_norm[e], rope, TM_PROJ)
            p_c = _modmm(xc, g_mix, modv, l, 1, w_in, e, TM_IN, TN_IN)
            q_c, k_c, v_c = _qkv_prep(p_c, 0, kv_block, ev_q_norm[e], ev_k_norm[e], None, TM_PROJ)
            if e == 0:
                att_l, w1_all, w2_all, ev_out, od_in, od_out, ev_in = _attention(
                    q_l, k_l, v_l, k_c, v_c, TQ_ATTN, TK_ATTN, cast=late_cast)
            else:
                att_l, = _attention(q_l, k_l, v_l, k_c, v_c, TQ_ATTN, TK_ATTN)
            conv_l = _short_conv_mixer(p_l, conv_col, ev_conv_w[e])
            if ctx_full:
                att_c, = _attention(q_c, k_c, v_c, None, None, TQ_ATTN, TK_ATTN)
                conv_c = _short_conv_mixer(p_c, conv_col, ev_conv_w[e])
                xc, hc = _outproj([att_c, conv_c], ev_out, e, xc, modv, l, 1, g_ffn, TM_PROJ)
            xs, hs = _outproj([att_l, conv_l], ev_out, e, xs, modv, l, 0, g_ffn, TM_PROJ)
        else:
            o = l // 2
            fargs = (od_f_w1[o], od_f_b1[o], od_f_freq[o], od_f_w2[o], od_f_b2[o], od_f_w3[o], od_f_decay[o])
            p_l = _modmm(xs, g_mix, modv, l, 0, od_in, o, TM_IN, TN_IN, out_dtype=BF16)
            u, x0, usk = _hyena_prep(p_l, od_conv_w[o], od_conv_b[o], od_skip[o])
            filt2 = _hyena_filter(n, d, *fargs, tr=TR_FILT, tc=TC_FILT)
            y = _long_conv_gate(u, x0, usk, filt2, n)
            xs, hs = _outproj([y], od_out, o, xs, modv, l, 0, g_ffn, TM_PROJ)
            if ctx_full:
                p_c = _modmm(xc, g_mix, modv, l, 1, od_in, o, TM_IN, TN_IN, out_dtype=BF16)
                u, x0, usk = _hyena_prep(p_c, od_conv_w[o], od_conv_b[o], od_skip[o])
                filt2 = _hyena_filter(n_ctx, d, *fargs, tr=TR_FILT, tc=TC_FILT)
                y = _long_conv_gate_short(u, x0, usk, filt2, n_ctx)
                xc, hc = _outproj([y], od_out, o, xc, modv, l, 1, g_ffn, TM_PROJ)
        last = l == depth - 1
        xs = _ffn(xs, hs, modv, l, 0, w1_all, w2_all, TM_FFN, TF_FFN, final_gain=final_g if last else None)
        if ctx_full:
            xc = _ffn(xc, hc, modv, l, 1, w1_all, w2_all, TM_FFN, TF_FFN)

    return xs[None]
```

```python
import functools
import math

import numpy as np
import jax
import jax.numpy as jnp
from jax import lax
from jax.experimental import pallas as pl
from jax.experimental.pallas import tpu as pltpu

F32 = jnp.float32
BF16 = jnp.bfloat16

HEAD_DIM = 128
N_Q_HEADS = 8
N_KV_HEADS = 2
Q_PER_KV = N_Q_HEADS // N_KV_HEADS
GRID_W = 64
ROPE_THETA = 10000.0
FILTER_BANDS = 16
FILTER_MOD_SHIFT = 0.05
EPS = 1e-6

LANES = 128
SUBLANES = 8
VMEM_BUDGET = 56 * 1024 * 1024
DFT_N1 = 128
ONES_ROWS = 16

TN_MOD = 1536
TM_PROJ = 512
TM_IN = 1024
TN_IN = 768
TM_FFN = 512
TF_FFN = 1024
TQ_ATTN = 512
TK_ATTN = 2048
TR_FILT = 1024
TC_FILT = 1024


def _cparams(sem, vmem_bytes):
    return pltpu.CompilerParams(dimension_semantics=sem,
                                vmem_limit_bytes=int(min(max(vmem_bytes, 16 << 20), VMEM_BUDGET)))


def _nbytes(shape, dtype):
    return int(np.prod(shape)) * jnp.dtype(dtype).itemsize


def _dot(a, b):
    return jnp.dot(a, b, preferred_element_type=F32)


def _rms_rows(x):
    return x * lax.rsqrt(jnp.mean(x * x, axis=-1, keepdims=True) + EPS)


def _modulate(x, g_ref, sh_ref, sc_ref, row):
    gs = g_ref[...] * (1.0 + sc_ref[row:row + 1, :])
    return _rms_rows(x) * gs + sh_ref[row:row + 1, :]


def _mod_kernel(c_ref, w_ref, b_ref, o_ref):
    c = c_ref[...]
    s = c * (1.0 / (1.0 + jnp.exp(-c)))
    o_ref[0] = _dot(s.astype(BF16), w_ref[0].astype(BF16)) + b_ref[0]


def _mod_vectors(cond, mod_w, mod_b):
    depth, d, n6 = mod_w.shape
    tn = TN_MOD
    rows = cond.shape[0]
    vm = 2 * _nbytes((d, tn), F32) + _nbytes((d, tn), BF16) + (4 << 20)
    return pl.pallas_call(
        _mod_kernel,
        grid=(depth, n6 // tn),
        in_specs=[pl.BlockSpec((rows, d), lambda l, j: (0, 0)),
                  pl.BlockSpec((1, d, tn), lambda l, j: (l, 0, j)),
                  pl.BlockSpec((1, 1, tn), lambda l, j: (l, 0, j))],
        out_specs=pl.BlockSpec((1, rows, tn), lambda l, j: (l, 0, j)),
        out_shape=jax.ShapeDtypeStruct((depth, rows, n6), F32),
        compiler_params=_cparams(("parallel", "parallel"), vm),
        name="adaln_vectors",
    )(cond, mod_w, mod_b.reshape(depth, 1, n6))


def _modmm_kernel(x_ref, g_ref, sh_ref, sc_ref, w_ref, o_ref, h_ref, *, row):
    @pl.when(pl.program_id(1) == 0)
    def _():
        h_ref[...] = _modulate(x_ref[...], g_ref, sh_ref, sc_ref, row).astype(BF16)

    o_ref[...] = _dot(h_ref[...], w_ref[...].astype(BF16)).astype(o_ref.dtype)


def _modmm(x, gains, modv, l, row, w, wi, tm, tn, out_dtype=F32):
    m, d = x.shape
    n = w.shape[2]
    tm = min(tm, m)
    vm = (2 * _nbytes((tm, d), F32) + _nbytes((tm, d), BF16) + 3 * _nbytes((d, tn), w.dtype)
          + 2 * _nbytes((tm, tn), F32) + 2 * _nbytes((tm, d), F32) + (4 << 20))
    return pl.pallas_call(
        functools.partial(_modmm_kernel, row=row),
        grid=(m // tm, n // tn),
        in_specs=[pl.BlockSpec((tm, d), lambda i, j: (i, 0)),
                  pl.BlockSpec((None, 1, d), lambda i, j: (l, 0, 0)),
                  pl.BlockSpec((None, SUBLANES, d), lambda i, j: (l, 0, 0)),
                  pl.BlockSpec((None, SUBLANES, d), lambda i, j: (l, 0, 1)),
                  pl.BlockSpec((None, d, tn), lambda i, j: (wi, 0, j))],
        out_specs=pl.BlockSpec((tm, tn), lambda i, j: (i, j)),
        out_shape=jax.ShapeDtypeStruct((m, n), out_dtype),
        scratch_shapes=[pltpu.VMEM((tm, d), BF16)],
        compiler_params=_cparams(("parallel", "arbitrary"), vm),
        name="modulated_projection",
    )(x, gains, modv, modv, w)


def _outproj_kernel(*refs, n_in, row):
    a_refs = refs[:n_in]
    w_refs = refs[n_in:2 * n_in]
    x_ref, gate_ref, g_ref, sh_ref, sc_ref, o_ref, h_ref = refs[2 * n_in:]
    acc = _dot(a_refs[0][...].astype(BF16), w_refs[0][...])
    for a_ref, w_ref in zip(a_refs[1:], w_refs[1:]):
        acc = acc + _dot(a_ref[...].astype(BF16), w_ref[...])
    x_new = x_ref[...] + gate_ref[row:row + 1, :] * acc
    o_ref[...] = x_new
    h_ref[...] = _modulate(x_new, g_ref, sh_ref, sc_ref, row).astype(BF16)


def _outproj(acts, w, wi, x, modv, l, row, gains, tm):
    m, d = x.shape
    tm = min(tm, m)
    n_in = len(acts)
    in_specs, args = [], []
    for a in acts:
        in_specs.append(pl.BlockSpec((tm, a.shape[1]), lambda i: (i, 0)))
        args.append(a)
    row0 = 0
    for a in acts:
        k = a.shape[1]
        in_specs.append(pl.BlockSpec((None, k, d), functools.partial(lambda i, b: (wi, b, 0), b=row0 // k)))
        args.append(w)
        row0 += k
    modblk = lambda col: pl.BlockSpec((None, SUBLANES, d), lambda i: (l, 0, col))
    in_specs += [pl.BlockSpec((tm, d), lambda i: (i, 0)), modblk(2),
                 pl.BlockSpec((None, 1, d), lambda i: (l, 0, 0)), modblk(3), modblk(4)]
    args += [x, modv, gains, modv, modv]
    vm = (2 * sum(_nbytes((tm, a.shape[1]), a.dtype) for a in acts) + 2 * _nbytes(w.shape[1:], BF16)
          + 8 * _nbytes((tm, d), F32) + (4 << 20))
    rowblk = pl.BlockSpec((tm, d), lambda i: (i, 0))
    return pl.pallas_call(
        functools.partial(_outproj_kernel, n_in=n_in, row=row),
        grid=(m // tm,),
        in_specs=in_specs,
        out_specs=[rowblk, rowblk],
        out_shape=[jax.ShapeDtypeStruct((m, d), F32), jax.ShapeDtypeStruct((m, d), BF16)],
        compiler_params=_cparams(("parallel",), vm),
        name="gated_out_projection",
    )(*args)


def _ffn_kernel(*refs, row, final_norm):
    if final_norm:
        x_ref, h_ref, gate_ref, w1_ref, w2_ref, fg_ref, o_ref = refs
    else:
        x_ref, h_ref, gate_ref, w1_ref, w2_ref, o_ref = refs
    j = pl.program_id(1)

    @pl.when(j == 0)
    def _():
        o_ref[...] = jnp.zeros(o_ref.shape, F32)

    a = jnp.maximum(_dot(h_ref[...], w1_ref[...]), 0.0)
    o_ref[...] += _dot((a * a).astype(BF16), w2_ref[...])

    @pl.when(j == pl.num_programs(1) - 1)
    def _():
        y = x_ref[...] + gate_ref[row:row + 1, :] * o_ref[...]
        if final_norm:
            y = _rms_rows(y) * fg_ref[...]
        o_ref[...] = y


def _ffn(x, h, modv, l, row, w1, w2, tm, tf, final_gain=None):
    m, d = x.shape
    f = w1.shape[2]
    tm = min(tm, m)
    vm = (4 * _nbytes((tm, d), F32) + 2 * _nbytes((tm, d), BF16)
          + 4 * _nbytes((d, tf), BF16) + 3 * _nbytes((tm, tf), F32) + (6 << 20))
    in_specs = [pl.BlockSpec((tm, d), lambda i, j: (i, 0)),
                pl.BlockSpec((tm, d), lambda i, j: (i, 0)),
                pl.BlockSpec((None, SUBLANES, d), lambda i, j: (l, 0, 5)),
                pl.BlockSpec((None, d, tf), lambda i, j: (l, 0, j)),
                pl.BlockSpec((None, tf, d), lambda i, j: (l, j, 0))]
    args = [x, h, modv, w1, w2]
    if final_gain is not None:
        in_specs.append(pl.BlockSpec((1, d), lambda i, j: (0, 0)))
        args.append(final_gain.reshape(1, d))
    return pl.pallas_call(
        functools.partial(_ffn_kernel, row=row, final_norm=final_gain is not None),
        grid=(m // tm, f // tf),
        in_specs=in_specs,
        out_specs=pl.BlockSpec((tm, d), lambda i, j: (i, 0)),
        out_shape=jax.ShapeDtypeStruct((m, d), F32),
        compiler_params=_cparams(("parallel", "arbitrary"), vm),
        name="gated_ffn",
    )(*args)


def _rope(x, cos, sin):
    lane = lax.broadcasted_iota(jnp.int32, x.shape, 1)
    first_half = (lane % (HEAD_DIM // 2)) < (HEAD_DIM // 4)
    partner = jnp.where(first_half,
                        pltpu.roll(x, HEAD_DIM - HEAD_DIM // 4, 1),
                        pltpu.roll(x, HEAD_DIM // 4, 1))
    return x * cos + partner * sin


def _qkv_kernel(*refs, use_rope, q_scale):
    if use_rope:
        pq_ref, pkv_ref, qg_ref, kg_ref, cos_ref, sin_ref, q_ref, k_ref, v_ref = refs
        cos, sin = cos_ref[...], sin_ref[...]
    else:
        pq_ref, pkv_ref, qg_ref, kg_ref, q_ref, k_ref, v_ref = refs
    kv_w = N_KV_HEADS * HEAD_DIM
    for h in range(N_Q_HEADS):
        sl = slice(h * HEAD_DIM, (h + 1) * HEAD_DIM)
        xn = _rms_rows(pq_ref[:, sl]) * qg_ref[...]
        if use_rope:
            xn = _rope(xn, cos, sin)
        q_ref[:, sl] = (xn * q_scale).astype(BF16)
    for h in range(N_KV_HEADS):
        sl = slice(h * HEAD_DIM, (h + 1) * HEAD_DIM)
        xn = _rms_rows(pkv_ref[:, sl]) * kg_ref[...]
        if use_rope:
            xn = _rope(xn, cos, sin)
        k_ref[:, sl] = xn.astype(BF16)
    v_ref[...] = pkv_ref[:, kv_w:2 * kv_w].T.astype(BF16)


def _qkv_prep(p, q_col_block, kv_col_block, q_gain, k_gain, rope, tm):
    m = p.shape[0]
    tm = min(tm, m)
    attn_w = N_Q_HEADS * HEAD_DIM
    kv_w = N_KV_HEADS * HEAD_DIM
    in_specs = [pl.BlockSpec((tm, attn_w), lambda i: (i, q_col_block)),
                pl.BlockSpec((tm, 2 * kv_w), lambda i: (i, kv_col_block)),
                pl.BlockSpec((1, HEAD_DIM), lambda i: (0, 0)),
                pl.BlockSpec((1, HEAD_DIM), lambda i: (0, 0))]
    args = [p, p, q_gain.reshape(1, HEAD_DIM), k_gain.reshape(1, HEAD_DIM)]
    if rope is not None:
        in_specs += [pl.BlockSpec((tm, HEAD_DIM), lambda i: (i, 0))] * 2
        args += list(rope)
    vm = 4 * _nbytes((tm, attn_w + 2 * kv_w), F32) + (8 << 20)
    return pl.pallas_call(
        functools.partial(_qkv_kernel, use_rope=rope is not None, q_scale=HEAD_DIM ** -0.5),
        grid=(m // tm,),
        in_specs=in_specs,
        out_specs=[pl.BlockSpec((tm, attn_w), lambda i: (i, 0)),
                   pl.BlockSpec((tm, kv_w), lambda i: (i, 0)),
                   pl.BlockSpec((kv_w, tm), lambda i: (0, i))],
        out_shape=[jax.ShapeDtypeStruct((m, attn_w), BF16),
                   jax.ShapeDtypeStruct((m, kv_w), BF16),
                   jax.ShapeDtypeStruct((kv_w, m), BF16)],
        compiler_params=_cparams(("parallel",), vm),
        name="qkv_norm_rope",
    )(*args)


def _attn_kernel(*refs, has_extra, tq, n_cast):
    n_in = 5 if has_extra else 3
    cast_in = refs[n_in:n_in + n_cast]
    o_ref = refs[n_in + n_cast]
    cast_out = refs[n_in + n_cast + 1:n_in + 2 * n_cast + 1]
    qs_ref, m_ref, acc_ref = refs[n_in + 2 * n_cast + 1:]
    if has_extra:
        q_ref, k_ref, vt_ref, ke_ref, vte_ref = refs[:n_in]
    else:
        q_ref, k_ref, vt_ref = refs[:n_in]
    kv = pl.program_id(2)

    for src, dst in zip(cast_in, cast_out):
        dst[...] = src[...].astype(BF16)

    def update(k, vt):
        vt1 = jnp.concatenate([vt, jnp.ones((ONES_ROWS, vt.shape[1]), BF16)], axis=0)
        cols = [slice(h * tq, (h + 1) * tq) for h in range(Q_PER_KV)]
        s, p, alpha = {}, {}, {}
        for t in range(Q_PER_KV + 2):
            if t < Q_PER_KV:
                s[t] = lax.dot_general(k, qs_ref[cols[t], :], (((1,), (1,)), ((), ())),
                                       preferred_element_type=F32)
            h = t - 1
            if 0 <= h < Q_PER_KV:
                m_prev = m_ref[:, cols[h]]
                m_new = jnp.maximum(m_prev, jnp.max(s[h], axis=0, keepdims=True))
                alpha[h] = jnp.exp(m_prev - m_new)
                p[h] = jnp.exp(s.pop(h) - m_new).astype(BF16)
                m_ref[:, cols[h]] = m_new
            h = t - 2
            if 0 <= h < Q_PER_KV:
                acc_ref[:, cols[h]] = alpha.pop(h) * acc_ref[:, cols[h]] + _dot(vt1, p.pop(h))

    @pl.when(kv == 0)
    def _():
        for h in range(Q_PER_KV):
            qs_ref[h * tq:(h + 1) * tq, :] = q_ref[:, h * HEAD_DIM:(h + 1) * HEAD_DIM]
        m_ref[...] = jnp.full(m_ref.shape, -jnp.inf, F32)
        acc_ref[...] = jnp.zeros(acc_ref.shape, F32)
        if has_extra:
            update(ke_ref[...], vte_ref[...])

    update(k_ref[...], vt_ref[...])

    @pl.when(kv == pl.num_programs(2) - 1)
    def _():
        o_t = acc_ref[0:HEAD_DIM, :] / acc_ref[HEAD_DIM:HEAD_DIM + 1, :]
        for h in range(Q_PER_KV):
            o_ref[:, h * HEAD_DIM:(h + 1) * HEAD_DIM] = o_t[:, h * tq:(h + 1) * tq].T.astype(o_ref.dtype)


def _attention(q, k, vt, k_extra, vt_extra, tq, tk, cast=()):
    m = q.shape[0]
    s_len = k.shape[0]
    tq = min(tq, m)
    tk = min(tk, s_len)
    gw = Q_PER_KV * HEAD_DIM
    cols = Q_PER_KV * tq
    has_extra = k_extra is not None
    n_i, n_j = m // tq, s_len // tk
    steps = N_KV_HEADS * n_i * n_j
    in_specs = [pl.BlockSpec((tq, gw), lambda g, i, j: (i, g)),
                pl.BlockSpec((tk, HEAD_DIM), lambda g, i, j: (j, g)),
                pl.BlockSpec((HEAD_DIM, tk), lambda g, i, j: (g, j))]
    args = [q, k, vt]
    if has_extra:
        e = k_extra.shape[0]
        in_specs += [pl.BlockSpec((e, HEAD_DIM), lambda g, i, j: (0, g)),
                     pl.BlockSpec((HEAD_DIM, e), lambda g, i, j: (g, 0))]
        args += [k_extra, vt_extra]
    out_specs = [pl.BlockSpec((tq, gw), lambda g, i, j: (i, g))]
    out_shape = [jax.ShapeDtypeStruct((m, N_Q_HEADS * HEAD_DIM), BF16)]
    for w in cast:
        lanes = w.shape[-1]
        rows = w.size // (steps * lanes)
        assert rows * steps * lanes == w.size and rows % (2 * SUBLANES) == 0
        slab = pl.BlockSpec((None, rows, lanes), lambda g, i, j: ((g * n_i + i) * n_j + j, 0, 0))
        in_specs.append(slab)
        args.append(w.reshape(steps, rows, lanes))
        out_specs.append(slab)
        out_shape.append(jax.ShapeDtypeStruct((steps, rows, lanes), BF16))
    acc_rows = HEAD_DIM + ONES_ROWS
    vm = 4 * _nbytes((tk, cols), F32) + 6 * _nbytes((acc_rows, cols), F32) + (8 << 20)
    outs = pl.pallas_call(
        functools.partial(_attn_kernel, has_extra=has_extra, tq=tq, n_cast=len(cast)),
        grid=(N_KV_HEADS, n_i, n_j),
        in_specs=in_specs,
        out_specs=out_specs,
        out_shape=out_shape,
        scratch_shapes=[pltpu.VMEM((cols, HEAD_DIM), BF16),
                        pltpu.VMEM((1, cols), F32),
                        pltpu.VMEM((acc_rows, cols), F32)],
        compiler_params=_cparams(("parallel", "parallel", "arbitrary"), vm),
        name="gqa_flash_attention",
    )(*args)
    return (outs[0],) + tuple(o.reshape(w.shape) for o, w in zip(outs[1:], cast))


def _conv3(z, w_ref):
    n = z.shape[0]
    prev = pltpu.roll(z, 1, 0)
    nxt = pltpu.roll(z, n - 1, 0)
    row = lax.broadcasted_iota(jnp.int32, (SUBLANES, z.shape[1]), 0)
    prev = jnp.concatenate([jnp.where(row == 0, 0.0, prev[:SUBLANES]), prev[SUBLANES:]], axis=0)
    nxt = jnp.concatenate([nxt[:n - SUBLANES], jnp.where(row == SUBLANES - 1, 0.0, nxt[n - SUBLANES:])], axis=0)
    return prev * w_ref[0:1, :] + z * w_ref[1:2, :] + nxt * w_ref[2:3, :]


def _convmix_kernel(u_ref, b_ref, c_ref, w_ref, o_ref):
    o_ref[...] = (b_ref[...] * _conv3(c_ref[...] * u_ref[...], w_ref)).astype(o_ref.dtype)


def _short_conv_mixer(p, col0, conv_w):
    m = p.shape[0]
    c = conv_w.shape[1]
    ct = LANES
    nb = c // ct
    b0 = col0 // ct
    vm = 8 * _nbytes((m, ct), F32) + 8 * _nbytes((m, ct), F32) + (4 << 20)
    return pl.pallas_call(
        _convmix_kernel,
        grid=(nb,),
        in_specs=[pl.BlockSpec((m, ct), lambda j: (0, b0 + j)),
                  pl.BlockSpec((m, ct), lambda j: (0, b0 + nb + j)),
                  pl.BlockSpec((m, ct), lambda j: (0, b0 + 2 * nb + j)),
                  pl.BlockSpec((3, ct), lambda j: (0, j))],
        out_specs=pl.BlockSpec((m, ct), lambda j: (0, j)),
        out_shape=jax.ShapeDtypeStruct((m, c), BF16),
        compiler_params=_cparams(("parallel",), vm),
        name="short_conv_mixer",
    )(p, p, p, conv_w)


def _conv3_staged(z, w_ref, pad_ref):
    n = z.shape[0]
    zeros = jnp.zeros((SUBLANES, z.shape[1]), F32)
    pad_ref[0:SUBLANES, :] = zeros
    pad_ref[SUBLANES:n + SUBLANES, :] = z
    pad_ref[n + SUBLANES:n + 2 * SUBLANES, :] = zeros
    prev = pad_ref[SUBLANES - 1:n + SUBLANES - 1, :]
    nxt = pad_ref[SUBLANES + 1:n + SUBLANES + 1, :]
    return prev * w_ref[0:1, :] + z * w_ref[1:2, :] + nxt * w_ref[2:3, :]


def _hyena_prep_kernel(p0_ref, p1_ref, p2_ref, w0_ref, w1_ref, w2_ref, b0_ref, b1_ref, b2_ref,
                       skip_ref, u_ref, x0_ref, usk_ref, pad0_ref, pad1_ref, pad2_ref):
    x0 = _conv3_staged(p0_ref[...].astype(F32), w0_ref, pad0_ref) + b0_ref[...]
    x1 = _conv3_staged(p1_ref[...].astype(F32), w1_ref, pad1_ref) + b1_ref[...]
    v = _conv3_staged(p2_ref[...].astype(F32), w2_ref, pad2_ref) + b2_ref[...]
    u = x1 * v
    u_ref[...] = u.astype(u_ref.dtype)
    x0_ref[...] = x0.astype(x0_ref.dtype)
    usk_ref[...] = (x0 * (u * skip_ref[...])).astype(usk_ref.dtype)


def _hyena_prep(p, conv_w, conv_b, skip):
    m = p.shape[0]
    c = skip.shape[0]
    ct = LANES
    nb = c // ct
    vm = 12 * _nbytes((m, ct), F32) + 10 * _nbytes((m, ct), F32) + (4 << 20)
    blk = lambda off: pl.BlockSpec((m, ct), functools.partial(lambda j, o: (0, o + j), o=off))
    wblk = lambda off: pl.BlockSpec((3, ct), functools.partial(lambda j, o: (0, o + j), o=off))
    bblk = lambda off: pl.BlockSpec((1, ct), functools.partial(lambda j, o: (0, o + j), o=off))
    out = jax.ShapeDtypeStruct((m, c), BF16)
    return pl.pallas_call(
        _hyena_prep_kernel,
        grid=(nb,),
        in_specs=[blk(0), blk(nb), blk(2 * nb), wblk(0), wblk(nb), wblk(2 * nb),
                  bblk(0), bblk(nb), bblk(2 * nb), pl.BlockSpec((1, ct), lambda j: (0, j))],
        out_specs=[pl.BlockSpec((m, ct), lambda j: (0, j))] * 3,
        out_shape=[out, out, out],
        scratch_shapes=[pltpu.VMEM((m + 2 * SUBLANES, ct), F32)] * 3,
        compiler_params=_cparams(("parallel",), vm),
        name="hyena_short_conv_gate",
    )(p, p, p, conv_w, conv_w, conv_w, conv_b.reshape(1, -1), conv_b.reshape(1, -1),
      conv_b.reshape(1, -1), skip.reshape(1, c))


def _filter_feats_t(n, order):
    t = np.asarray(order, np.float64)
    t_norm = t / max(n - 1, 1)
    bands = np.arange(1, FILTER_BANDS + 1, dtype=np.float64)
    ang = (2 * math.pi / n) * bands[:, None] * t[None, :]
    feats = np.concatenate([t_norm[None, :], np.cos(ang), np.sin(ang)], axis=0)
    out = np.zeros((LANES, len(t)), np.float32)
    out[:feats.shape[0]] = feats
    return out


def _split_bf16(a):
    hi = a.astype(BF16).astype(F32)
    return hi, a - hi


def _dot_split(a, b):
    a_hi, a_lo = _split_bf16(a)
    b_hi, b_lo = _split_bf16(b)
    lhs = jnp.concatenate([a_hi, a_hi, a_lo], axis=1).astype(BF16)
    rhs = jnp.concatenate([b_hi, b_lo, b_hi], axis=0).astype(BF16)
    return _dot(lhs, rhs)


def _filter_kernel(f_ref, w1_ref, b1_ref, fr_ref, w2_ref, b2_ref, w3_ref, dec_ref, o_ref, hs_ref, tn_ref,
                   *, n):
    tr = tn_ref.shape[0]
    hid = w3_ref.shape[0]

    @pl.when(pl.program_id(2) == 0)
    def _():
        fr = fr_ref[...]
        h = jnp.sin(fr * (_dot_split(w1_ref[...], f_ref[...]) + b1_ref[...]))
        h = jnp.concatenate([h, jnp.zeros((LANES - hid, tr), F32)], axis=0)
        h = jnp.sin(fr * (_dot_split(w2_ref[...], h) + b2_ref[...]))
        hi, lo = _split_bf16(h)
        pad = jnp.zeros((2 * LANES - 3 * hid, tr), F32)
        hs_ref[...] = jnp.concatenate([hi, hi, lo, pad], axis=0).T.astype(BF16)
        pos = pl.program_id(1) * tr + lax.broadcasted_iota(jnp.int32, (tr, LANES), 0)
        pos = jnp.where(pl.program_id(0) == 0, pos, n - 1 - pos)
        tn_ref[...] = pos.astype(F32) / float(max(n - 1, 1))

    w_hi, w_lo = _split_bf16(w3_ref[...])
    pad = jnp.zeros((2 * LANES - 3 * hid, w_hi.shape[1]), F32)
    rhs = jnp.concatenate([w_hi, w_lo, w_hi, pad], axis=0).astype(BF16)
    hw = _dot(hs_ref[...], rhs)
    t_norm = jnp.concatenate([tn_ref[...]] * (hw.shape[1] // LANES), axis=1)
    window = jnp.exp(-t_norm * jnp.abs(dec_ref[...])) + FILTER_MOD_SHIFT
    o_ref[...] = (hw * window).astype(o_ref.dtype)


def _hyena_filter(n, ch, f_w1, f_b1, f_freq, f_w2, f_b2, f_w3, f_decay, tr, tc):
    hid = f_w1.shape[1]
    assert 3 * hid <= 2 * LANES and hid <= LANES
    feats = np.stack([_filter_feats_t(n, np.arange(n)), _filter_feats_t(n, np.arange(n)[::-1])])
    w1t = jnp.zeros((hid, LANES), F32).at[:, :f_w1.shape[0]].set(f_w1.T)
    w2t = jnp.zeros((hid, LANES), F32).at[:, :hid].set(f_w2.T)
    tr = min(tr, n)
    nrb = n // tr
    ncb = ch // tc
    vm = 6 * _nbytes((tr, tc), F32) + 4 * _nbytes((LANES, tr), F32) + (8 << 20)
    col = lambda a: a.reshape(hid, 1)
    small = lambda shape: pl.BlockSpec(shape, lambda s, i, j: (0, 0))
    return pl.pallas_call(
        functools.partial(_filter_kernel, n=n),
        grid=(2, nrb, ncb),
        in_specs=[pl.BlockSpec((None, LANES, tr), lambda s, i, j: (s, 0, i)),
                  small((hid, LANES)), small((hid, 1)), small((hid, 1)),
                  small((hid, LANES)), small((hid, 1)),
                  pl.BlockSpec((hid, tc), lambda s, i, j: (0, s * ncb + j)),
                  pl.BlockSpec((1, tc), lambda s, i, j: (0, s * ncb + j))],
        out_specs=pl.BlockSpec((tr, tc), lambda s, i, j: (s * nrb + i, j)),
        out_shape=jax.ShapeDtypeStruct((2 * n, ch), BF16),
        scratch_shapes=[pltpu.VMEM((tr, 2 * LANES), BF16), pltpu.VMEM((tr, LANES), F32)],
        compiler_params=_cparams(("parallel", "parallel", "arbitrary"), vm),
        name="hyena_filter",
    )(jnp.asarray(feats), w1t, col(f_b1), col(f_freq), w2t, col(f_b2), f_w3, f_decay.reshape(1, -1))


def _dft_geometry(n):
    big_n = 2 * n
    n2 = big_n // DFT_N1
    nh = n2 // 2 + 1
    return big_n, n2, nh


def _k2_block(nh, limit=13):
    return max(k for k in range(1, limit + 1) if nh % k == 0)


@functools.lru_cache(maxsize=None)
def _stage1_matrix(n, rows_n2):
    big_n, _, nh = _dft_geometry(n)
    groups = DFT_N1 // SUBLANES
    out = np.zeros((groups, 2, nh, SUBLANES, rows_n2, SUBLANES), np.float32)
    k2 = np.arange(nh, dtype=np.float64)[:, None]
    n2v = np.arange(rows_n2, dtype=np.float64)[None, :]
    for g in range(groups):
        for j in range(SUBLANES):
            t = SUBLANES * g + j + DFT_N1 * n2v
            ang = 2 * math.pi * np.mod(k2 * t, big_n) / big_n
            out[g, 0, :, j, :, j] = np.cos(ang)
            out[g, 1, :, j, :, j] = -np.sin(ang)
    return out.reshape(groups, 2 * nh * SUBLANES, rows_n2 * SUBLANES).astype(BF16)


@functools.lru_cache(maxsize=None)
def _stage2_matrices():
    idx = np.arange(DFT_N1, dtype=np.float64)
    ang = 2 * math.pi * np.mod(np.outer(idx, idx), DFT_N1) / DFT_N1
    c, s = np.cos(ang), np.sin(ang)
    fwd = np.block([[c, s], [-s, c]])
    inv = np.block([[c, -s], [s, c]])
    return fwd.astype(BF16), inv.astype(BF16)


@functools.lru_cache(maxsize=None)
def _stage4_matrix(n):
    big_n, n2c, nh = _dft_geometry(n)
    groups = DFT_N1 // SUBLANES
    rows_n2 = n2c // 2
    out = np.zeros((groups, rows_n2, SUBLANES, 2, nh, SUBLANES), np.float32)
    k2 = np.arange(nh, dtype=np.float64)[None, :]
    wgt = np.where(k2 <= n2c // 2, 2.0, 0.0)
    wgt[0, 0] = 1.0
    wgt[0, n2c // 2] = 1.0
    n2v = np.arange(rows_n2, dtype=np.float64)[:, None]
    for g in range(groups):
        for j in range(SUBLANES):
            t = SUBLANES * g + j + DFT_N1 * n2v
            ang = 2 * math.pi * np.mod(k2 * t, big_n) / big_n
            out[g, :, j, 0, :, j] = wgt * np.cos(ang) / big_n
            out[g, :, j, 1, :, j] = -wgt * np.sin(ang) / big_n
    return out.reshape(groups, rows_n2 * SUBLANES, 2 * nh * SUBLANES).astype(BF16)


def _dft_stage1_kernel(x_ref, l_ref, o_ref, *, gsteps):
    nb, _, ct = x_ref.shape
    rows = o_ref.shape[0]
    xin = x_ref[...].astype(F32)
    outs = []
    for gi in range(gsteps):
        sl = slice(SUBLANES * gi, SUBLANES * (gi + 1))
        xv = xin[:, sl, :].reshape(nb * SUBLANES, ct).astype(BF16)
        outs.append(_dot(l_ref[gi], xv).reshape(rows, SUBLANES, ct))
    o_ref[...] = jnp.concatenate(outs, axis=1).astype(o_ref.dtype)


def _dft_stage1(x3, n, gsteps, ct):
    rows_n2, _, c = x3.shape
    _, _, nh = _dft_geometry(n)
    mat = _stage1_matrix(n, rows_n2)
    groups = mat.shape[0]
    ct = min(ct, c)
    gw = SUBLANES * gsteps
    vm = (2 * _nbytes((rows_n2, gw, ct), F32) + 2 * _nbytes((gsteps,) + mat.shape[1:], BF16)
          + 2 * _nbytes((2 * nh, gw, ct), F32) + 3 * _nbytes((2 * nh * SUBLANES, ct), F32) + (4 << 20))
    return pl.pallas_call(
        functools.partial(_dft_stage1_kernel, gsteps=gsteps),
        grid=(groups // gsteps, c // ct),
        in_specs=[pl.BlockSpec((rows_n2, gw, ct), lambda g, j: (0, g, j)),
                  pl.BlockSpec((gsteps,) + mat.shape[1:], lambda g, j: (g, 0, 0))],
        out_specs=pl.BlockSpec((2 * nh, gw, ct), lambda g, j: (0, g, j)),
        out_shape=jax.ShapeDtypeStruct((2 * nh, DFT_N1, c), BF16),
        compiler_params=_cparams(("parallel", "parallel"), vm),
        name="dft_stage_n2",
    )(x3, mat)


def _stack_parts(a_ref, kb):
    cols = [jnp.concatenate([a_ref[0, k], a_ref[1, k]], axis=0) for k in range(kb)]
    return jnp.concatenate(cols, axis=1).astype(BF16)


def _store_parts(o_ref, x, kb):
    ct = o_ref.shape[-1]
    for k in range(kb):
        o_ref[0, k] = x[:DFT_N1, k * ct:(k + 1) * ct].astype(o_ref.dtype)
        o_ref[1, k] = x[DFT_N1:, k * ct:(k + 1) * ct].astype(o_ref.dtype)


def _spectral_product_kernel(a_ref, fa_ref, l2_ref, l3_ref, o_ref, *, kb):
    ct = a_ref.shape[-1]
    x = _dot(l2_ref[...], _stack_parts(a_ref, kb))
    f = _dot(l2_ref[...], _stack_parts(fa_ref, kb))
    ys = []
    for k in range(kb):
        cols = slice(k * ct, (k + 1) * ct)
        xr, xi = x[:DFT_N1, cols], x[DFT_N1:, cols]
        kr, ki = f[:DFT_N1, cols], f[DFT_N1:, cols]
        ys.append(jnp.concatenate([xr * kr - xi * ki, xr * ki + xi * kr], axis=0))
    _store_parts(o_ref, _dot(l3_ref[...], jnp.concatenate(ys, axis=1).astype(BF16)), kb)


def _dft_stage2(a4, fa4, kb, ct):
    _, nh, _, c = a4.shape
    fwd, inv = _stage2_matrices()
    ct = min(ct, c)
    blk = pl.BlockSpec((2, kb, DFT_N1, ct), lambda g, j: (0, g, 0, j))
    mblk = pl.BlockSpec((2 * DFT_N1, 2 * DFT_N1), lambda g, j: (0, 0))
    vm = 6 * _nbytes((2, kb, DFT_N1, ct), BF16) + 8 * _nbytes((2 * DFT_N1, kb * ct), F32) + (4 << 20)
    return pl.pallas_call(
        functools.partial(_spectral_product_kernel, kb=kb),
        grid=(nh // kb, c // ct),
        in_specs=[blk, blk, mblk, mblk],
        out_specs=blk,
        out_shape=jax.ShapeDtypeStruct(a4.shape, BF16),
        compiler_params=_cparams(("parallel", "parallel"), vm),
        name="dft_stage_n1_product",
    )(a4, fa4, fwd, inv)


def _dft_stage4_kernel(v_ref, l_ref, x0_ref, usk_ref, o_ref, *, gsteps):
    rows, _, ct = v_ref.shape
    nb = o_ref.shape[0]
    vin = v_ref[...].astype(F32)
    ys = []
    for gi in range(gsteps):
        sl = slice(SUBLANES * gi, SUBLANES * (gi + 1))
        vv = vin[:, sl, :].reshape(rows * SUBLANES, ct).astype(BF16)
        ys.append(_dot(l_ref[gi], vv).reshape(nb, SUBLANES, ct))
    y = jnp.concatenate(ys, axis=1)
    o_ref[...] = (x0_ref[...].astype(F32) * y + usk_ref[...].astype(F32)).astype(o_ref.dtype)


def _dft_stage4(v3, x03, usk3, n, gsteps, ct):
    rows, _, c = v3.shape
    mat = _stage4_matrix(n)
    groups = mat.shape[0]
    nb = x03.shape[0]
    ct = min(ct, c)
    gw = SUBLANES * gsteps
    vm = (2 * _nbytes((rows, gw, ct), F32) + 2 * _nbytes((gsteps,) + mat.shape[1:], BF16)
          + 6 * _nbytes((nb, gw, ct), F32) + 3 * _nbytes((rows * SUBLANES, ct), F32) + (4 << 20))
    oblk = pl.BlockSpec((nb, gw, ct), lambda g, j: (0, g, j))
    return pl.pallas_call(
        functools.partial(_dft_stage4_kernel, gsteps=gsteps),
        grid=(groups // gsteps, c // ct),
        in_specs=[pl.BlockSpec((rows, gw, ct), lambda g, j: (0, g, j)),
                  pl.BlockSpec((gsteps,) + mat.shape[1:], lambda g, j: (g, 0, 0)),
                  oblk, oblk],
        out_specs=oblk,
        out_shape=jax.ShapeDtypeStruct((nb, DFT_N1, c), BF16),
        compiler_params=_cparams(("parallel", "parallel"), vm),
        name="dft_stage_k2_inverse",
    )(v3, mat, x03, usk3)


def _long_conv_gate(u, x0, usk, filt2, n):
    c = u.shape[1]
    _, _, nh = _dft_geometry(n)
    as3 = lambda a: a.reshape(a.shape[0] // DFT_N1, DFT_N1, c)
    fa = _dft_stage1(as3(filt2), n, gsteps=2, ct=512)
    a = _dft_stage1(as3(u), n, gsteps=2, ct=512)
    v = _dft_stage2(a.reshape(2, nh, DFT_N1, c), fa.reshape(2, nh, DFT_N1, c), kb=_k2_block(nh), ct=256)
    y = _dft_stage4(v.reshape(2 * nh, DFT_N1, c), as3(x0), as3(usk), n, gsteps=2, ct=512)
    return y.reshape(n, c)


@functools.lru_cache(maxsize=None)
def _dense_dft_matrices(n):
    big_n = 2 * n
    nf = n + 1
    nfp = -(-nf // SUBLANES) * SUBLANES
    k = np.arange(nf, dtype=np.float64)[:, None]
    t = np.arange(big_n, dtype=np.float64)[None, :]
    ang = 2 * math.pi * np.mod(k * t, big_n) / big_n
    fwd = np.zeros((2, nfp, big_n), np.float32)
    fwd[0, :nf] = np.cos(ang)
    fwd[1, :nf] = -np.sin(ang)
    wgt = np.full((nf, 1), 2.0)
    wgt[0] = 1.0
    wgt[n] = 1.0
    inv = np.zeros((2, nfp, n), np.float32)
    inv[0, :nf] = (wgt * np.cos(ang) / big_n)[:, :n]
    inv[1, :nf] = (-wgt * np.sin(ang) / big_n)[:, :n]
    fwd = fwd.reshape(2 * nfp, big_n)
    inv = inv.reshape(2 * nfp, n).T
    return (np.ascontiguousarray(fwd[:, :n]).astype(BF16), fwd.astype(BF16),
            np.ascontiguousarray(inv).astype(BF16))


def _short_conv_gate_kernel(u_ref, f_ref, x0_ref, usk_ref, lu_ref, lf_ref, li_ref, o_ref):
    half = lu_ref.shape[0] // 2
    xs = _dot(lu_ref[...], u_ref[...].astype(BF16))
    ks = _dot(lf_ref[...], f_ref[...].astype(BF16))
    xr, xi, kr, ki = xs[:half], xs[half:], ks[:half], ks[half:]
    y = jnp.concatenate([xr * kr - xi * ki, xr * ki + xi * kr], axis=0).astype(BF16)
    conv = _dot(li_ref[...], y)
    o_ref[...] = (x0_ref[...].astype(F32) * conv + usk_ref[...].astype(F32)).astype(o_ref.dtype)


def _long_conv_gate_short(u, x0, usk, filt2, n):
    c = u.shape[1]
    lu, lf, li = _dense_dft_matrices(n)
    ct = 256
    blk = pl.BlockSpec((n, ct), lambda j: (0, j))
    full = lambda a: pl.BlockSpec(a.shape, lambda j: (0, 0))
    vm = 32 << 20
    return pl.pallas_call(
        _short_conv_gate_kernel,
        grid=(c // ct,),
        in_specs=[blk, pl.BlockSpec((2 * n, ct), lambda j: (0, j)), blk, blk, full(lu), full(lf), full(li)],
        out_specs=blk,
        out_shape=jax.ShapeDtypeStruct((n, c), BF16),
        compiler_params=_cparams(("parallel",), vm),
        name="context_long_conv",
    )(u, filt2, x0, usk, lu, lf, li)


@functools.lru_cache(maxsize=None)
def _rope_tables(n):
    t = np.arange(n)
    row = (t // GRID_W).astype(np.float64)
    col = (t % GRID_W).astype(np.float64)
    axis_dim = HEAD_DIM // 2
    inv = 1.0 / (ROPE_THETA ** (np.arange(0, axis_dim, 2, dtype=np.float64) / axis_dim))
    ar, ac = row[:, None] * inv, col[:, None] * inv
    cos = np.concatenate([np.cos(ar), np.cos(ar), np.cos(ac), np.cos(ac)], axis=-1)
    sin = np.concatenate([-np.sin(ar), np.sin(ar), -np.sin(ac), np.sin(ac)], axis=-1)
    return cos.astype(np.float32), sin.astype(np.float32)


def kernel(x, c, ctx, c_ctx, mod_w, mod_b, norm_mix_g, norm_ffn_g, ffn_w1, ffn_w2, ev_w_in, ev_q_norm, ev_k_norm, ev_conv_w, ev_w_out, od_w_in, od_conv_w, od_conv_b, od_f_w1, od_f_b1, od_f_freq, od_f_w2, od_f_b2, od_f_w3, od_f_decay, od_skip, od_w_out, final_g):
    batch, n, d = x.shape
    assert batch == 1
    n_ctx = ctx.shape[1]
    depth = mod_w.shape[0]
    last_ctx_read = ((depth - 1) // 2) * 2
    attn_w = N_Q_HEADS * HEAD_DIM
    kv_w = N_KV_HEADS * HEAD_DIM

    xs = x[0]
    xc = ctx[0]
    cond = jnp.zeros((SUBLANES, d), F32).at[0].set(c[0]).at[1].set(c_ctx)
    modv = _mod_vectors(cond, mod_w, mod_b)
    rope = _rope_tables(n)

    late_cast = (ffn_w1, ffn_w2, ev_w_out, od_w_in, od_w_out, ev_w_in)
    g_mix = norm_mix_g.reshape(depth, 1, d)
    g_ffn = norm_ffn_g.reshape(depth, 1, d)
    kv_block = attn_w // (2 * kv_w)
    conv_col = attn_w + 2 * kv_w

    for l in range(depth):
        ctx_full = l < last_ctx_read
        if l % 2 == 0:
            e = l // 2
            w_in = ev_w_in if e == 0 else ev_in
            p_l = _modmm(xs, g_mix, modv, l, 0, w_in, e, TM_IN, TN_IN)
            q_l, k_l, v_l = _qkv_prep(p_l, 0, kv_block, ev_q_norm[e], ev_k_norm[e], rope, TM_PROJ)
            p_c = _modmm(xc, g_mix, modv, l, 1, w_in, e, TM_IN, TN_IN)
            q_c, k_c, v_c = _qkv_prep(p_c, 0, kv_block, ev_q_norm[e], ev_k_norm[e], None, TM_PROJ)
            if e == 0:
                att_l, w1_all, w2_all, ev_out, od_in, od_out, ev_in = _attention(
                    q_l, k_l, v_l, k_c, v_c, TQ_ATTN, TK_ATTN, cast=late_cast)
            else:
                att_l, = _attention(q_l, k_l, v_l, k_c, v_c, TQ_ATTN, TK_ATTN)
            conv_l = _short_conv_mixer(p_l, conv_col, ev_conv_w[e])
            if ctx_full:
                att_c, = _attention(q_c, k_c, v_c, None, None, TQ_ATTN, TK_ATTN)
                conv_c = _short_conv_mixer(p_c, conv_col, ev_conv_w[e])
                xc, hc = _outproj([att_c, conv_c], ev_out, e, xc, modv, l, 1, g_ffn, TM_PROJ)
            xs, hs = _outproj([att_l, conv_l], ev_out, e, xs, modv, l, 0, g_ffn, TM_PROJ)
        else:
            o = l // 2
            fargs = (od_f_w1[o], od_f_b1[o], od_f_freq[o], od_f_w2[o], od_f_b2[o], od_f_w3[o], od_f_decay[o])
            p_l = _modmm(xs, g_mix, modv, l, 0, od_in, o, TM_IN, TN_IN, out_dtype=BF16)
            u, x0, usk = _hyena_prep(p_l, od_conv_w[o], od_conv_b[o], od_skip[o])
            filt2 = _hyena_filter(n, d, *fargs, tr=TR_FILT, tc=TC_FILT)
            y = _long_conv_gate(u, x0, usk, filt2, n)
            xs, hs = _outproj([y], od_out, o, xs, modv, l, 0, g_ffn, TM_PROJ)
            if ctx_full:
                p_c = _modmm(xc, g_mix, modv, l, 1, od_in, o, TM_IN, TN_IN, out_dtype=BF16)
                u, x0, usk = _hyena_prep(p_c, od_conv_w[o], od_conv_b[o], od_skip[o])
                filt2 = _hyena_filter(n_ctx, d, *fargs, tr=TR_FILT, tc=TC_FILT)
                y = _long_conv_gate_short(u, x0, usk, filt2, n_ctx)
                xc, hc = _outproj([y], od_out, o, xc, modv, l, 1, g_ffn, TM_PROJ)
        last = l == depth - 1
        xs = _ffn(xs, hs, modv, l, 0, w1_all, w2_all, TM_FFN, TF_FFN, final_gain=final_g if last else None)
        if ctx_full:
            xc = _ffn(xc, hc, modv, l, 1, w1_all, w2_all, TM_FFN, TF_FFN)

    return xs[None]
```

```python
import functools
import math

import numpy as np
import jax
import jax.numpy as jnp
from jax import lax
from jax.experimental import pallas as pl
from jax.experimental.pallas import tpu as pltpu

F32 = jnp.float32
BF16 = jnp.bfloat16

HEAD_DIM = 128
N_Q_HEADS = 8
N_KV_HEADS = 2
Q_PER_KV = N_Q_HEADS // N_KV_HEADS
GRID_W = 64
ROPE_THETA = 10000.0
FILTER_BANDS = 16
FILTER_MOD_SHIFT = 0.05
EPS = 1e-6

LANES = 128
SUBLANES = 8
VMEM_BUDGET = 56 * 1024 * 1024
DFT_N1 = 128
ONES_ROWS = 16

TN_MOD = 1536
TM_PROJ = 512
TM_IN = 1024
TN_IN = 768
TM_FFN = 512
TF_FFN = 1024
TQ_ATTN = 512
TK_ATTN = 2048
TR_FILT = 1024
TC_FILT = 1024


def _cparams(sem, vmem_bytes):
    return pltpu.CompilerParams(dimension_semantics=sem,
                                vmem_limit_bytes=int(min(max(vmem_bytes, 16 << 20), VMEM_BUDGET)))


def _nbytes(shape, dtype):
    return int(np.prod(shape)) * jnp.dtype(dtype).itemsize


def _dot(a, b):
    return jnp.dot(a, b, preferred_element_type=F32)


def _rms_rows(x):
    return x * lax.rsqrt(jnp.mean(x * x, axis=-1, keepdims=True) + EPS)


def _modulate(x, g_ref, sh_ref, sc_ref, row):
    gs = g_ref[...] * (1.0 + sc_ref[row:row + 1, :])
    return _rms_rows(x) * gs + sh_ref[row:row + 1, :]


def _mod_kernel(c_ref, w_ref, b_ref, o_ref):
    c = c_ref[...]
    s = c * (1.0 / (1.0 + jnp.exp(-c)))
    o_ref[0] = _dot(s.astype(BF16), w_ref[0].astype(BF16)) + b_ref[0]


def _mod_vectors(cond, mod_w, mod_b):
    depth, d, n6 = mod_w.shape
    tn = TN_MOD
    rows = cond.shape[0]
    vm = 2 * _nbytes((d, tn), F32) + _nbytes((d, tn), BF16) + (4 << 20)
    return pl.pallas_call(
        _mod_kernel,
        grid=(depth, n6 // tn),
        in_specs=[pl.BlockSpec((rows, d), lambda l, j: (0, 0)),
                  pl.BlockSpec((1, d, tn), lambda l, j: (l, 0, j)),
                  pl.BlockSpec((1, 1, tn), lambda l, j: (l, 0, j))],
        out_specs=pl.BlockSpec((1, rows, tn), lambda l, j: (l, 0, j)),
        out_shape=jax.ShapeDtypeStruct((depth, rows, n6), F32),
        compiler_params=_cparams(("parallel", "parallel"), vm),
        name="adaln_vectors",
    )(cond, mod_w, mod_b.reshape(depth, 1, n6))


def _modmm_kernel(x_ref, g_ref, sh_ref, sc_ref, w_ref, o_ref, h_ref, *, row):
    @pl.when(pl.program_id(1) == 0)
    def _():
        h_ref[...] = _modulate(x_ref[...], g_ref, sh_ref, sc_ref, row).astype(BF16)

    o_ref[...] = _dot(h_ref[...], w_ref[...].astype(BF16)).astype(o_ref.dtype)


def _modmm(x, gains, modv, l, row, w, wi, tm, tn, out_dtype=F32):
    m, d = x.shape
    n = w.shape[2]
    tm = min(tm, m)
    vm = (2 * _nbytes((tm, d), F32) + _nbytes((tm, d), BF16) + 3 * _nbytes((d, tn), w.dtype)
          + 2 * _nbytes((tm, tn), F32) + 2 * _nbytes((tm, d), F32) + (4 << 20))
    return pl.pallas_call(
        functools.partial(_modmm_kernel, row=row),
        grid=(m // tm, n // tn),
        in_specs=[pl.BlockSpec((tm, d), lambda i, j: (i, 0)),
                  pl.BlockSpec((None, 1, d), lambda i, j: (l, 0, 0)),
                  pl.BlockSpec((None, SUBLANES, d), lambda i, j: (l, 0, 0)),
                  pl.BlockSpec((None, SUBLANES, d), lambda i, j: (l, 0, 1)),
                  pl.BlockSpec((None, d, tn), lambda i, j: (wi, 0, j))],
        out_specs=pl.BlockSpec((tm, tn), lambda i, j: (i, j)),
        out_shape=jax.ShapeDtypeStruct((m, n), out_dtype),
        scratch_shapes=[pltpu.VMEM((tm, d), BF16)],
        compiler_params=_cparams(("parallel", "arbitrary"), vm),
        name="modulated_projection",
    )(x, gains, modv, modv, w)


def _outproj_kernel(*refs, n_in, row):
    a_refs = refs[:n_in]
    w_refs = refs[n_in:2 * n_in]
    x_ref, gate_ref, g_ref, sh_ref, sc_ref, o_ref, h_ref = refs[2 * n_in:]
    acc = _dot(a_refs[0][...].astype(BF16), w_refs[0][...])
    for a_ref, w_ref in zip(a_refs[1:], w_refs[1:]):
        acc = acc + _dot(a_ref[...].astype(BF16), w_ref[...])
    x_new = x_ref[...] + gate_ref[row:row + 1, :] * acc
    o_ref[...] = x_new
    h_ref[...] = _modulate(x_new, g_ref, sh_ref, sc_ref, row).astype(BF16)


def _outproj(acts, w, wi, x, modv, l, row, gains, tm):
    m, d = x.shape
    tm = min(tm, m)
    n_in = len(acts)
    in_specs, args = [], []
    for a in acts:
        in_specs.append(pl.BlockSpec((tm, a.shape[1]), lambda i: (i, 0)))
        args.append(a)
    row0 = 0
    for a in acts:
        k = a.shape[1]
        in_specs.append(pl.BlockSpec((None, k, d), functools.partial(lambda i, b: (wi, b, 0), b=row0 // k)))
        args.append(w)
        row0 += k
    modblk = lambda col: pl.BlockSpec((None, SUBLANES, d), lambda i: (l, 0, col))
    in_specs += [pl.BlockSpec((tm, d), lambda i: (i, 0)), modblk(2),
                 pl.BlockSpec((None, 1, d), lambda i: (l, 0, 0)), modblk(3), modblk(4)]
    args += [x, modv, gains, modv, modv]
    vm = (2 * sum(_nbytes((tm, a.shape[1]), a.dtype) for a in acts) + 2 * _nbytes(w.shape[1:], BF16)
          + 8 * _nbytes((tm, d), F32) + (4 << 20))
    rowblk = pl.BlockSpec((tm, d), lambda i: (i, 0))
    return pl.pallas_call(
        functools.partial(_outproj_kernel, n_in=n_in, row=row),
        grid=(m // tm,),
        in_specs=in_specs,
        out_specs=[rowblk, rowblk],
        out_shape=[jax.ShapeDtypeStruct((m, d), F32), jax.ShapeDtypeStruct((m, d), BF16)],
        compiler_params=_cparams(("parallel",), vm),
        name="gated_out_projection",
    )(*args)


def _ffn_kernel(*refs, row, final_norm):
    if final_norm:
        x_ref, h_ref, gate_ref, w1_ref, w2_ref, fg_ref, o_ref = refs
    else:
        x_ref, h_ref, gate_ref, w1_ref, w2_ref, o_ref = refs
    j = pl.program_id(1)

    @pl.when(j == 0)
    def _():
        o_ref[...] = jnp.zeros(o_ref.shape, F32)

    a = jnp.maximum(_dot(h_ref[...], w1_ref[...]), 0.0)
    o_ref[...] += _dot((a * a).astype(BF16), w2_ref[...])

    @pl.when(j == pl.num_programs(1) - 1)
    def _():
        y = x_ref[...] + gate_ref[row:row + 1, :] * o_ref[...]
        if final_norm:
            y = _rms_rows(y) * fg_ref[...]
        o_ref[...] = y


def _ffn(x, h, modv, l, row, w1, w2, tm, tf, final_gain=None):
    m, d = x.shape
    f = w1.shape[2]
    tm = min(tm, m)
    vm = (4 * _nbytes((tm, d), F32) + 2 * _nbytes((tm, d), BF16)
          + 4 * _nbytes((d, tf), BF16) + 3 * _nbytes((tm, tf), F32) + (6 << 20))
    in_specs = [pl.BlockSpec((tm, d), lambda i, j: (i, 0)),
                pl.BlockSpec((tm, d), lambda i, j: (i, 0)),
                pl.BlockSpec((None, SUBLANES, d), lambda i, j: (l, 0, 5)),
                pl.BlockSpec((None, d, tf), lambda i, j: (l, 0, j)),
                pl.BlockSpec((None, tf, d), lambda i, j: (l, j, 0))]
    args = [x, h, modv, w1, w2]
    if final_gain is not None:
        in_specs.append(pl.BlockSpec((1, d), lambda i, j: (0, 0)))
        args.append(final_gain.reshape(1, d))
    return pl.pallas_call(
        functools.partial(_ffn_kernel, row=row, final_norm=final_gain is not None),
        grid=(m // tm, f // tf),
        in_specs=in_specs,
        out_specs=pl.BlockSpec((tm, d), lambda i, j: (i, 0)),
        out_shape=jax.ShapeDtypeStruct((m, d), F32),
        compiler_params=_cparams(("parallel", "arbitrary"), vm),
        name="gated_ffn",
    )(*args)


def _rope(x, cos, sin):
    lane = lax.broadcasted_iota(jnp.int32, x.shape, 1)
    first_half = (lane % (HEAD_DIM // 2)) < (HEAD_DIM // 4)
    partner = jnp.where(first_half,
                        pltpu.roll(x, HEAD_DIM - HEAD_DIM // 4, 1),
                        pltpu.roll(x, HEAD_DIM // 4, 1))
    return x * cos + partner * sin


def _kv_kernel(*refs, use_rope):
    if use_rope:
        pkv_ref, kg_ref, cos_ref, sin_ref, k_ref, v_ref = refs
        cos, sin = cos_ref[...], sin_ref[...]
    else:
        pkv_ref, kg_ref, k_ref, v_ref = refs
    kv_w = N_KV_HEADS * HEAD_DIM
    for h in range(N_KV_HEADS):
        sl = slice(h * HEAD_DIM, (h + 1) * HEAD_DIM)
        xn = _rms_rows(pkv_ref[:, sl]) * kg_ref[...]
        if use_rope:
            xn = _rope(xn, cos, sin)
        k_ref[:, sl] = xn.astype(BF16)
    v_ref[...] = pkv_ref[:, kv_w:2 * kv_w].T.astype(BF16)


def _kv_prep(p, kv_col_block, k_gain, rope, tm):
    m = p.shape[0]
    tm = min(tm, m)
    kv_w = N_KV_HEADS * HEAD_DIM
    in_specs = [pl.BlockSpec((tm, 2 * kv_w), lambda i: (i, kv_col_block)),
                pl.BlockSpec((1, HEAD_DIM), lambda i: (0, 0))]
    args = [p, k_gain.reshape(1, HEAD_DIM)]
    if rope is not None:
        in_specs += [pl.BlockSpec((tm, HEAD_DIM), lambda i: (i, 0))] * 2
        args += list(rope)
    vm = 6 * _nbytes((tm, 2 * kv_w), F32) + (8 << 20)
    return pl.pallas_call(
        functools.partial(_kv_kernel, use_rope=rope is not None),
        grid=(m // tm,),
        in_specs=in_specs,
        out_specs=[pl.BlockSpec((tm, kv_w), lambda i: (i, 0)),
                   pl.BlockSpec((kv_w, tm), lambda i: (0, i))],
        out_shape=[jax.ShapeDtypeStruct((m, kv_w), BF16),
                   jax.ShapeDtypeStruct((kv_w, m), BF16)],
        compiler_params=_cparams(("parallel",), vm),
        name="kv_norm_rope",
    )(*args)


def _attn_kernel(*refs, has_extra, use_rope, tq, n_cast):
    q_ref, k_ref, vt_ref = refs[:3]
    n_in = 3
    if has_extra:
        ke_ref, vte_ref = refs[3:5]
        n_in = 5
    qg_ref = refs[n_in]
    n_in += 1
    if use_rope:
        cos_ref, sin_ref = refs[n_in:n_in + 2]
        n_in += 2
    cast_in = refs[n_in:n_in + n_cast]
    o_ref = refs[n_in + n_cast]
    cast_out = refs[n_in + n_cast + 1:n_in + 2 * n_cast + 1]
    qs_ref, m_ref, acc_ref = refs[n_in + 2 * n_cast + 1:]
    kv = pl.program_id(2)

    for src, dst in zip(cast_in, cast_out):
        dst[...] = src[...].astype(BF16)

    def update(k, vt):
        vt1 = jnp.concatenate([vt, jnp.ones((ONES_ROWS, vt.shape[1]), BF16)], axis=0)
        cols = [slice(h * tq, (h + 1) * tq) for h in range(Q_PER_KV)]
        s, p, alpha = {}, {}, {}
        for t in range(Q_PER_KV + 2):
            if t < Q_PER_KV:
                s[t] = lax.dot_general(k, qs_ref[cols[t], :], (((1,), (1,)), ((), ())),
                                       preferred_element_type=F32)
            h = t - 1
            if 0 <= h < Q_PER_KV:
                m_prev = m_ref[:, cols[h]]
                m_new = jnp.maximum(m_prev, jnp.max(s[h], axis=0, keepdims=True))
                alpha[h] = jnp.exp(m_prev - m_new)
                p[h] = jnp.exp(s.pop(h) - m_new).astype(BF16)
                m_ref[:, cols[h]] = m_new
            h = t - 2
            if 0 <= h < Q_PER_KV:
                acc_ref[:, cols[h]] = alpha.pop(h) * acc_ref[:, cols[h]] + _dot(vt1, p.pop(h))

    @pl.when(kv == 0)
    def _():
        for h in range(Q_PER_KV):
            xn = _rms_rows(q_ref[:, h * HEAD_DIM:(h + 1) * HEAD_DIM]) * qg_ref[...]
            if use_rope:
                xn = _rope(xn, cos_ref[...], sin_ref[...])
            qs_ref[h * tq:(h + 1) * tq, :] = (xn * HEAD_DIM ** -0.5).astype(BF16)
        m_ref[...] = jnp.full(m_ref.shape, -jnp.inf, F32)
        acc_ref[...] = jnp.zeros(acc_ref.shape, F32)
        if has_extra:
            update(ke_ref[...], vte_ref[...])

    update(k_ref[...], vt_ref[...])

    @pl.when(kv == pl.num_programs(2) - 1)
    def _():
        o_t = acc_ref[0:HEAD_DIM, :] / acc_ref[HEAD_DIM:HEAD_DIM + 1, :]
        for h in range(Q_PER_KV):
            o_ref[:, h * HEAD_DIM:(h + 1) * HEAD_DIM] = o_t[:, h * tq:(h + 1) * tq].T.astype(o_ref.dtype)


def _attention(q, q_gain, rope, k, vt, k_extra, vt_extra, tq, tk, cast=()):
    m = q.shape[0]
    s_len = k.shape[0]
    tq = min(tq, m)
    tk = min(tk, s_len)
    gw = Q_PER_KV * HEAD_DIM
    cols = Q_PER_KV * tq
    has_extra = k_extra is not None
    n_i, n_j = m // tq, s_len // tk
    steps = N_KV_HEADS * n_i * n_j
    in_specs = [pl.BlockSpec((tq, gw), lambda g, i, j: (i, g)),
                pl.BlockSpec((tk, HEAD_DIM), lambda g, i, j: (j, g)),
                pl.BlockSpec((HEAD_DIM, tk), lambda g, i, j: (g, j))]
    args = [q, k, vt]
    if has_extra:
        e = k_extra.shape[0]
        in_specs += [pl.BlockSpec((e, HEAD_DIM), lambda g, i, j: (0, g)),
                     pl.BlockSpec((HEAD_DIM, e), lambda g, i, j: (g, 0))]
        args += [k_extra, vt_extra]
    in_specs.append(pl.BlockSpec((1, HEAD_DIM), lambda g, i, j: (0, 0)))
    args.append(q_gain.reshape(1, HEAD_DIM))
    if rope is not None:
        in_specs += [pl.BlockSpec((tq, HEAD_DIM), lambda g, i, j: (i, 0))] * 2
        args += list(rope)
    out_specs = [pl.BlockSpec((tq, gw), lambda g, i, j: (i, g))]
    out_shape = [jax.ShapeDtypeStruct((m, N_Q_HEADS * HEAD_DIM), BF16)]
    for w in cast:
        lanes = w.shape[-1]
        rows = w.size // (steps * lanes)
        assert rows * steps * lanes == w.size and rows % (2 * SUBLANES) == 0
        slab = pl.BlockSpec((None, rows, lanes), lambda g, i, j: ((g * n_i + i) * n_j + j, 0, 0))
        in_specs.append(slab)
        args.append(w.reshape(steps, rows, lanes))
        out_specs.append(slab)
        out_shape.append(jax.ShapeDtypeStruct((steps, rows, lanes), BF16))
    acc_rows = HEAD_DIM + ONES_ROWS
    vm = 4 * _nbytes((tk, cols), F32) + 6 * _nbytes((acc_rows, cols), F32) + (8 << 20)
    outs = pl.pallas_call(
        functools.partial(_attn_kernel, has_extra=has_extra, use_rope=rope is not None, tq=tq,
                          n_cast=len(cast)),
        grid=(N_KV_HEADS, n_i, n_j),
        in_specs=in_specs,
        out_specs=out_specs,
        out_shape=out_shape,
        scratch_shapes=[pltpu.VMEM((cols, HEAD_DIM), BF16),
                        pltpu.VMEM((1, cols), F32),
                        pltpu.VMEM((acc_rows, cols), F32)],
        compiler_params=_cparams(("parallel", "parallel", "arbitrary"), vm),
        name="gqa_flash_attention",
    )(*args)
    return (outs[0],) + tuple(o.reshape(w.shape) for o, w in zip(outs[1:], cast))


def _conv3(z, w_ref):
    n = z.shape[0]
    prev = pltpu.roll(z, 1, 0)
    nxt = pltpu.roll(z, n - 1, 0)
    row = lax.broadcasted_iota(jnp.int32, (SUBLANES, z.shape[1]), 0)
    prev = jnp.concatenate([jnp.where(row == 0, 0.0, prev[:SUBLANES]), prev[SUBLANES:]], axis=0)
    nxt = jnp.concatenate([nxt[:n - SUBLANES], jnp.where(row == SUBLANES - 1, 0.0, nxt[n - SUBLANES:])], axis=0)
    return prev * w_ref[0:1, :] + z * w_ref[1:2, :] + nxt * w_ref[2:3, :]


def _convmix_kernel(u_ref, b_ref, c_ref, w_ref, o_ref):
    o_ref[...] = (b_ref[...] * _conv3(c_ref[...] * u_ref[...], w_ref)).astype(o_ref.dtype)


def _short_conv_mixer(p, col0, conv_w):
    m = p.shape[0]
    c = conv_w.shape[1]
    ct = LANES
    nb = c // ct
    b0 = col0 // ct
    vm = 8 * _nbytes((m, ct), F32) + 8 * _nbytes((m, ct), F32) + (4 << 20)
    return pl.pallas_call(
        _convmix_kernel,
        grid=(nb,),
        in_specs=[pl.BlockSpec((m, ct), lambda j: (0, b0 + j)),
                  pl.BlockSpec((m, ct), lambda j: (0, b0 + nb + j)),
                  pl.BlockSpec((m, ct), lambda j: (0, b0 + 2 * nb + j)),
                  pl.BlockSpec((3, ct), lambda j: (0, j))],
        out_specs=pl.BlockSpec((m, ct), lambda j: (0, j)),
        out_shape=jax.ShapeDtypeStruct((m, c), BF16),
        compiler_params=_cparams(("parallel",), vm),
        name="short_conv_mixer",
    )(p, p, p, conv_w)


def _conv3_staged(z, w_ref, pad_ref):
    n = z.shape[0]
    zeros = jnp.zeros((SUBLANES, z.shape[1]), F32)
    pad_ref[0:SUBLANES, :] = zeros
    pad_ref[SUBLANES:n + SUBLANES, :] = z
    pad_ref[n + SUBLANES:n + 2 * SUBLANES, :] = zeros
    prev = pad_ref[SUBLANES - 1:n + SUBLANES - 1, :]
    nxt = pad_ref[SUBLANES + 1:n + SUBLANES + 1, :]
    return prev * w_ref[0:1, :] + z * w_ref[1:2, :] + nxt * w_ref[2:3, :]


def _hyena_prep_kernel(p0_ref, p1_ref, p2_ref, w0_ref, w1_ref, w2_ref, b0_ref, b1_ref, b2_ref,
                       skip_ref, u_ref, x0_ref, usk_ref, pad0_ref, pad1_ref, pad2_ref):
    x0 = _conv3_staged(p0_ref[...].astype(F32), w0_ref, pad0_ref) + b0_ref[...]
    x1 = _conv3_staged(p1_ref[...].astype(F32), w1_ref, pad1_ref) + b1_ref[...]
    v = _conv3_staged(p2_ref[...].astype(F32), w2_ref, pad2_ref) + b2_ref[...]
    u = x1 * v
    u_ref[...] = u.astype(u_ref.dtype)
    x0_ref[...] = x0.astype(x0_ref.dtype)
    usk_ref[...] = (x0 * (u * skip_ref[...])).astype(usk_ref.dtype)


def _hyena_prep(p, conv_w, conv_b, skip):
    m = p.shape[0]
    c = skip.shape[0]
    ct = LANES
    nb = c // ct
    vm = 12 * _nbytes((m, ct), F32) + 10 * _nbytes((m, ct), F32) + (4 << 20)
    blk = lambda off: pl.BlockSpec((m, ct), functools.partial(lambda j, o: (0, o + j), o=off))
    wblk = lambda off: pl.BlockSpec((3, ct), functools.partial(lambda j, o: (0, o + j), o=off))
    bblk = lambda off: pl.BlockSpec((1, ct), functools.partial(lambda j, o: (0, o + j), o=off))
    out = jax.ShapeDtypeStruct((m, c), BF16)
    return pl.pallas_call(
        _hyena_prep_kernel,
        grid=(nb,),
        in_specs=[blk(0), blk(nb), blk(2 * nb), wblk(0), wblk(nb), wblk(2 * nb),
                  bblk(0), bblk(nb), bblk(2 * nb), pl.BlockSpec((1, ct), lambda j: (0, j))],
        out_specs=[pl.BlockSpec((m, ct), lambda j: (0, j))] * 3,
        out_shape=[out, out, out],
        scratch_shapes=[pltpu.VMEM((m + 2 * SUBLANES, ct), F32)] * 3,
        compiler_params=_cparams(("parallel",), vm),
        name="hyena_short_conv_gate",
    )(p, p, p, conv_w, conv_w, conv_w, conv_b.reshape(1, -1), conv_b.reshape(1, -1),
      conv_b.reshape(1, -1), skip.reshape(1, c))


def _filter_feats_t(n, order):
    t = np.asarray(order, np.float64)
    t_norm = t / max(n - 1, 1)
    bands = np.arange(1, FILTER_BANDS + 1, dtype=np.float64)
    ang = (2 * math.pi / n) * bands[:, None] * t[None, :]
    feats = np.concatenate([t_norm[None, :], np.cos(ang), np.sin(ang)], axis=0)
    out = np.zeros((LANES, len(t)), np.float32)
    out[:feats.shape[0]] = feats
    return out


def _split_bf16(a):
    hi = a.astype(BF16).astype(F32)
    return hi, a - hi


def _dot_split(a, b):
    a_hi, a_lo = _split_bf16(a)
    b_hi, b_lo = _split_bf16(b)
    lhs = jnp.concatenate([a_hi, a_hi, a_lo], axis=1).astype(BF16)
    rhs = jnp.concatenate([b_hi, b_lo, b_hi], axis=0).astype(BF16)
    return _dot(lhs, rhs)


def _filter_kernel(f_ref, w1_ref, b1_ref, fr_ref, w2_ref, b2_ref, w3_ref, dec_ref, o_ref, hs_ref, tn_ref,
                   *, n):
    tr = tn_ref.shape[0]
    hid = w3_ref.shape[0]

    @pl.when(pl.program_id(2) == 0)
    def _():
        fr = fr_ref[...]
        h = jnp.sin(fr * (_dot_split(w1_ref[...], f_ref[...]) + b1_ref[...]))
        h = jnp.concatenate([h, jnp.zeros((LANES - hid, tr), F32)], axis=0)
        h = jnp.sin(fr * (_dot_split(w2_ref[...], h) + b2_ref[...]))
        hi, lo = _split_bf16(h)
        pad = jnp.zeros((2 * LANES - 3 * hid, tr), F32)
        hs_ref[...] = jnp.concatenate([hi, hi, lo, pad], axis=0).T.astype(BF16)
        pos = pl.program_id(1) * tr + lax.broadcasted_iota(jnp.int32, (tr, LANES), 0)
        pos = jnp.where(pl.program_id(0) == 0, pos, n - 1 - pos)
        tn_ref[...] = pos.astype(F32) / float(max(n - 1, 1))

    w_hi, w_lo = _split_bf16(w3_ref[...])
    pad = jnp.zeros((2 * LANES - 3 * hid, w_hi.shape[1]), F32)
    rhs = jnp.concatenate([w_hi, w_lo, w_hi, pad], axis=0).astype(BF16)
    hw = _dot(hs_ref[...], rhs)
    t_norm = jnp.concatenate([tn_ref[...]] * (hw.shape[1] // LANES), axis=1)
    window = jnp.exp(-t_norm * jnp.abs(dec_ref[...])) + FILTER_MOD_SHIFT
    o_ref[...] = (hw * window).astype(o_ref.dtype)


def _hyena_filter(n, ch, f_w1, f_b1, f_freq, f_w2, f_b2, f_w3, f_decay, tr, tc):
    hid = f_w1.shape[1]
    assert 3 * hid <= 2 * LANES and hid <= LANES
    feats = np.stack([_filter_feats_t(n, np.arange(n)), _filter_feats_t(n, np.arange(n)[::-1])])
    w1t = jnp.zeros((hid, LANES), F32).at[:, :f_w1.shape[0]].set(f_w1.T)
    w2t = jnp.zeros((hid, LANES), F32).at[:, :hid].set(f_w2.T)
    tr = min(tr, n)
    nrb = n // tr
    ncb = ch // tc
    vm = 6 * _nbytes((tr, tc), F32) + 4 * _nbytes((LANES, tr), F32) + (8 << 20)
    col = lambda a: a.reshape(hid, 1)
    small = lambda shape: pl.BlockSpec(shape, lambda s, i, j: (0, 0))
    return pl.pallas_call(
        functools.partial(_filter_kernel, n=n),
        grid=(2, nrb, ncb),
        in_specs=[pl.BlockSpec((None, LANES, tr), lambda s, i, j: (s, 0, i)),
                  small((hid, LANES)), small((hid, 1)), small((hid, 1)),
                  small((hid, LANES)), small((hid, 1)),
                  pl.BlockSpec((hid, tc), lambda s, i, j: (0, s * ncb + j)),
                  pl.BlockSpec((1, tc), lambda s, i, j: (0, s * ncb + j))],
        out_specs=pl.BlockSpec((tr, tc), lambda s, i, j: (s * nrb + i, j)),
        out_shape=jax.ShapeDtypeStruct((2 * n, ch), BF16),
        scratch_shapes=[pltpu.VMEM((tr, 2 * LANES), BF16), pltpu.VMEM((tr, LANES), F32)],
        compiler_params=_cparams(("parallel", "parallel", "arbitrary"), vm),
        name="hyena_filter",
    )(jnp.asarray(feats), w1t, col(f_b1), col(f_freq), w2t, col(f_b2), f_w3, f_decay.reshape(1, -1))


def _dft_geometry(n):
    big_n = 2 * n
    n2 = big_n // DFT_N1
    nh = n2 // 2 + 1
    return big_n, n2, nh


def _k2_block(nh, limit=13):
    return max(k for k in range(1, limit + 1) if nh % k == 0)


@functools.lru_cache(maxsize=None)
def _stage1_matrix(n, rows_n2):
    big_n, _, nh = _dft_geometry(n)
    groups = DFT_N1 // SUBLANES
    out = np.zeros((groups, 2, nh, SUBLANES, rows_n2, SUBLANES), np.float32)
    k2 = np.arange(nh, dtype=np.float64)[:, None]
    n2v = np.arange(rows_n2, dtype=np.float64)[None, :]
    for g in range(groups):
        for j in range(SUBLANES):
            t = SUBLANES * g + j + DFT_N1 * n2v
            ang = 2 * math.pi * np.mod(k2 * t, big_n) / big_n
            out[g, 0, :, j, :, j] = np.cos(ang)
            out[g, 1, :, j, :, j] = -np.sin(ang)
    return out.reshape(groups, 2 * nh * SUBLANES, rows_n2 * SUBLANES).astype(BF16)


@functools.lru_cache(maxsize=None)
def _stage2_matrices():
    idx = np.arange(DFT_N1, dtype=np.float64)
    ang = 2 * math.pi * np.mod(np.outer(idx, idx), DFT_N1) / DFT_N1
    c, s = np.cos(ang), np.sin(ang)
    fwd = np.block([[c, s], [-s, c]])
    inv = np.block([[c, -s], [s, c]])
    return fwd.astype(BF16), inv.astype(BF16)


@functools.lru_cache(maxsize=None)
def _stage4_matrix(n):
    big_n, n2c, nh = _dft_geometry(n)
    groups = DFT_N1 // SUBLANES
    rows_n2 = n2c // 2
    out = np.zeros((groups, rows_n2, SUBLANES, 2, nh, SUBLANES), np.float32)
    k2 = np.arange(nh, dtype=np.float64)[None, :]
    wgt = np.where(k2 <= n2c // 2, 2.0, 0.0)
    wgt[0, 0] = 1.0
    wgt[0, n2c // 2] = 1.0
    n2v = np.arange(rows_n2, dtype=np.float64)[:, None]
    for g in range(groups):
        for j in range(SUBLANES):
            t = SUBLANES * g + j + DFT_N1 * n2v
            ang = 2 * math.pi * np.mod(k2 * t, big_n) / big_n
            out[g, :, j, 0, :, j] = wgt * np.cos(ang) / big_n
            out[g, :, j, 1, :, j] = -wgt * np.sin(ang) / big_n
    return out.reshape(groups, rows_n2 * SUBLANES, 2 * nh * SUBLANES).astype(BF16)


def _dft_stage1_kernel(x_ref, l_ref, o_ref, *, gsteps):
    nb, _, ct = x_ref.shape
    rows = o_ref.shape[0]
    xin = x_ref[...].astype(F32)
    outs = []
    for gi in range(gsteps):
        sl = slice(SUBLANES * gi, SUBLANES * (gi + 1))
        xv = xin[:, sl, :].reshape(nb * SUBLANES, ct).astype(BF16)
        outs.append(_dot(l_ref[gi], xv).reshape(rows, SUBLANES, ct))
    o_ref[...] = jnp.concatenate(outs, axis=1).astype(o_ref.dtype)


def _dft_stage1(x3, n, gsteps, ct):
    rows_n2, _, c = x3.shape
    _, _, nh = _dft_geometry(n)
    mat = _stage1_matrix(n, rows_n2)
    groups = mat.shape[0]
    ct = min(ct, c)
    gw = SUBLANES * gsteps
    vm = (2 * _nbytes((rows_n2, gw, ct), F32) + 2 * _nbytes((gsteps,) + mat.shape[1:], BF16)
          + 2 * _nbytes((2 * nh, gw, ct), F32) + 3 * _nbytes((2 * nh * SUBLANES, ct), F32) + (4 << 20))
    return pl.pallas_call(
        functools.partial(_dft_stage1_kernel, gsteps=gsteps),
        grid=(groups // gsteps, c // ct),
        in_specs=[pl.BlockSpec((rows_n2, gw, ct), lambda g, j: (0, g, j)),
                  pl.BlockSpec((gsteps,) + mat.shape[1:], lambda g, j: (g, 0, 0))],
        out_specs=pl.BlockSpec((2 * nh, gw, ct), lambda g, j: (0, g, j)),
        out_shape=jax.ShapeDtypeStruct((2 * nh, DFT_N1, c), BF16),
        compiler_params=_cparams(("parallel", "parallel"), vm),
        name="dft_stage_n2",
    )(x3, mat)


def _stack_parts(a_ref, kb):
    cols = [jnp.concatenate([a_ref[0, k], a_ref[1, k]], axis=0) for k in range(kb)]
    return jnp.concatenate(cols, axis=1).astype(BF16)


def _store_parts(o_ref, x, kb):
    ct = o_ref.shape[-1]
    for k in range(kb):
        o_ref[0, k] = x[:DFT_N1, k * ct:(k + 1) * ct].astype(o_ref.dtype)
        o_ref[1, k] = x[DFT_N1:, k * ct:(k + 1) * ct].astype(o_ref.dtype)


def _spectral_product_kernel(a_ref, fa_ref, l2_ref, l3_ref, o_ref, *, kb):
    ct = a_ref.shape[-1]
    x = _dot(l2_ref[...], _stack_parts(a_ref, kb))
    f = _dot(l2_ref[...], _stack_parts(fa_ref, kb))
    ys = []
    for k in range(kb):
        cols = slice(k * ct, (k + 1) * ct)
        xr, xi = x[:DFT_N1, cols], x[DFT_N1:, cols]
        kr, ki = f[:DFT_N1, cols], f[DFT_N1:, cols]
        ys.append(jnp.concatenate([xr * kr - xi * ki, xr * ki + xi * kr], axis=0))
    _store_parts(o_ref, _dot(l3_ref[...], jnp.concatenate(ys, axis=1).astype(BF16)), kb)


def _dft_stage2(a4, fa4, kb, ct):
    _, nh, _, c = a4.shape
    fwd, inv = _stage2_matrices()
    ct = min(ct, c)
    blk = pl.BlockSpec((2, kb, DFT_N1, ct), lambda g, j: (0, g, 0, j))
    mblk = pl.BlockSpec((2 * DFT_N1, 2 * DFT_N1), lambda g, j: (0, 0))
    vm = 6 * _nbytes((2, kb, DFT_N1, ct), BF16) + 8 * _nbytes((2 * DFT_N1, kb * ct), F32) + (4 << 20)
    return pl.pallas_call(
        functools.partial(_spectral_product_kernel, kb=kb),
        grid=(nh // kb, c // ct),
        in_specs=[blk, blk, mblk, mblk],
        out_specs=blk,
        out_shape=jax.ShapeDtypeStruct(a4.shape, BF16),
        compiler_params=_cparams(("parallel", "parallel"), vm),
        name="dft_stage_n1_product",
    )(a4, fa4, fwd, inv)


def _dft_stage4_kernel(v_ref, l_ref, x0_ref, usk_ref, o_ref, *, gsteps):
    rows, _, ct = v_ref.shape
    nb = o_ref.shape[0]
    vin = v_ref[...].astype(F32)
    ys = []
    for gi in range(gsteps):
        sl = slice(SUBLANES * gi, SUBLANES * (gi + 1))
        vv = vin[:, sl, :].reshape(rows * SUBLANES, ct).astype(BF16)
        ys.append(_dot(l_ref[gi], vv).reshape(nb, SUBLANES, ct))
    y = jnp.concatenate(ys, axis=1)
    o_ref[...] = (x0_ref[...].astype(F32) * y + usk_ref[...].astype(F32)).astype(o_ref.dtype)


def _dft_stage4(v3, x03, usk3, n, gsteps, ct):
    rows, _, c = v3.shape
    mat = _stage4_matrix(n)
    groups = mat.shape[0]
    nb = x03.shape[0]
    ct = min(ct, c)
    gw = SUBLANES * gsteps
    vm = (2 * _nbytes((rows, gw, ct), F32) + 2 * _nbytes((gsteps,) + mat.shape[1:], BF16)
          + 6 * _nbytes((nb, gw, ct), F32) + 3 * _nbytes((rows * SUBLANES, ct), F32) + (4 << 20))
    oblk = pl.BlockSpec((nb, gw, ct), lambda g, j: (0, g, j))
    return pl.pallas_call(
        functools.partial(_dft_stage4_kernel, gsteps=gsteps),
        grid=(groups // gsteps, c // ct),
        in_specs=[pl.BlockSpec((rows, gw, ct), lambda g, j: (0, g, j)),
                  pl.BlockSpec((gsteps,) + mat.shape[1:], lambda g, j: (g, 0, 0)),
                  oblk, oblk],
        out_specs=oblk,
        out_shape=jax.ShapeDtypeStruct((nb, DFT_N1, c), BF16),
        compiler_params=_cparams(("parallel", "parallel"), vm),
        name="dft_stage_k2_inverse",
    )(v3, mat, x03, usk3)


def _long_conv_gate(u, x0, usk, filt2, n):
    c = u.shape[1]
    _, _, nh = _dft_geometry(n)
    as3 = lambda a: a.reshape(a.shape[0] // DFT_N1, DFT_N1, c)
    fa = _dft_stage1(as3(filt2), n, gsteps=2, ct=512)
    a = _dft_stage1(as3(u), n, gsteps=2, ct=512)
    v = _dft_stage2(a.reshape(2, nh, DFT_N1, c), fa.reshape(2, nh, DFT_N1, c), kb=_k2_block(nh), ct=256)
    y = _dft_stage4(v.reshape(2 * nh, DFT_N1, c), as3(x0), as3(usk), n, gsteps=2, ct=512)
    return y.reshape(n, c)


@functools.lru_cache(maxsize=None)
def _dense_dft_matrices(n):
    big_n = 2 * n
    nf = n + 1
    nfp = -(-nf // SUBLANES) * SUBLANES
    k = np.arange(nf, dtype=np.float64)[:, None]
    t = np.arange(big_n, dtype=np.float64)[None, :]
    ang = 2 * math.pi * np.mod(k * t, big_n) / big_n
    fwd = np.zeros((2, nfp, big_n), np.float32)
    fwd[0, :nf] = np.cos(ang)
    fwd[1, :nf] = -np.sin(ang)
    wgt = np.full((nf, 1), 2.0)
    wgt[0] = 1.0
    wgt[n] = 1.0
    inv = np.zeros((2, nfp, n), np.float32)
    inv[0, :nf] = (wgt * np.cos(ang) / big_n)[:, :n]
    inv[1, :nf] = (-wgt * np.sin(ang) / big_n)[:, :n]
    fwd = fwd.reshape(2 * nfp, big_n)
    inv = inv.reshape(2 * nfp, n).T
    return (np.ascontiguousarray(fwd[:, :n]).astype(BF16), fwd.astype(BF16),
            np.ascontiguousarray(inv).astype(BF16))


def _short_conv_gate_kernel(u_ref, f_ref, x0_ref, usk_ref, lu_ref, lf_ref, li_ref, o_ref):
    half = lu_ref.shape[0] // 2
    xs = _dot(lu_ref[...], u_ref[...].astype(BF16))
    ks = _dot(lf_ref[...], f_ref[...].astype(BF16))
    xr, xi, kr, ki = xs[:half], xs[half:], ks[:half], ks[half:]
    y = jnp.concatenate([xr * kr - xi * ki, xr * ki + xi * kr], axis=0).astype(BF16)
    conv = _dot(li_ref[...], y)
    o_ref[...] = (x0_ref[...].astype(F32) * conv + usk_ref[...].astype(F32)).astype(o_ref.dtype)


def _long_conv_gate_short(u, x0, usk, filt2, n):
    c = u.shape[1]
    lu, lf, li = _dense_dft_matrices(n)
    ct = 256
    blk = pl.BlockSpec((n, ct), lambda j: (0, j))
    full = lambda a: pl.BlockSpec(a.shape, lambda j: (0, 0))
    vm = 32 << 20
    return pl.pallas_call(
        _short_conv_gate_kernel,
        grid=(c // ct,),
        in_specs=[blk, pl.BlockSpec((2 * n, ct), lambda j: (0, j)), blk, blk, full(lu), full(lf), full(li)],
        out_specs=blk,
        out_shape=jax.ShapeDtypeStruct((n, c), BF16),
        compiler_params=_cparams(("parallel",), vm),
        name="context_long_conv",
    )(u, filt2, x0, usk, lu, lf, li)


@functools.lru_cache(maxsize=None)
def _rope_tables(n):
    t = np.arange(n)
    row = (t // GRID_W).astype(np.float64)
    col = (t % GRID_W).astype(np.float64)
    axis_dim = HEAD_DIM // 2
    inv = 1.0 / (ROPE_THETA ** (np.arange(0, axis_dim, 2, dtype=np.float64) / axis_dim))
    ar, ac = row[:, None] * inv, col[:, None] * inv
    cos = np.concatenate([np.cos(ar), np.cos(ar), np.cos(ac), np.cos(ac)], axis=-1)
    sin = np.concatenate([-np.sin(ar), np.sin(ar), -np.sin(ac), np.sin(ac)], axis=-1)
    return cos.astype(np.float32), sin.astype(np.float32)


def kernel(x, c, ctx, c_ctx, mod_w, mod_b, norm_mix_g, norm_ffn_g, ffn_w1, ffn_w2, ev_w_in, ev_q_norm, ev_k_norm, ev_conv_w, ev_w_out, od_w_in, od_conv_w, od_conv_b, od_f_w1, od_f_b1, od_f_freq, od_f_w2, od_f_b2, od_f_w3, od_f_decay, od_skip, od_w_out, final_g):
    batch, n, d = x.shape
    assert batch == 1
    n_ctx = ctx.shape[1]
    depth = mod_w.shape[0]
    last_ctx_read = ((depth - 1) // 2) * 2
    attn_w = N_Q_HEADS * HEAD_DIM
    kv_w = N_KV_HEADS * HEAD_DIM

    xs = x[0]
    xc = ctx[0]
    cond = jnp.zeros((SUBLANES, d), F32).at[0].set(c[0]).at[1].set(c_ctx)
    modv = _mod_vectors(cond, mod_w, mod_b)
    rope = _rope_tables(n)

    late_cast = (ffn_w1, ffn_w2, ev_w_out, od_w_in, od_w_out, ev_w_in)
    g_mix = norm_mix_g.reshape(depth, 1, d)
    g_ffn = norm_ffn_g.reshape(depth, 1, d)
    kv_block = attn_w // (2 * kv_w)
    conv_col = attn_w + 2 * kv_w

    for l in range(depth):
        ctx_full = l < last_ctx_read
        if l % 2 == 0:
            e = l // 2
            w_in = ev_w_in if e == 0 else ev_in
            p_l = _modmm(xs, g_mix, modv, l, 0, w_in, e, TM_IN, TN_IN)
            k_l, v_l = _kv_prep(p_l, kv_block, ev_k_norm[e], rope, TM_PROJ)
            p_c = _modmm(xc, g_mix, modv, l, 1, w_in, e, TM_IN, TN_IN)
            k_c, v_c = _kv_prep(p_c, kv_block, ev_k_norm[e], None, TM_PROJ)
            if e == 0:
                att_l, w1_all, w2_all, ev_out, od_in, od_out, ev_in = _attention(
                    p_l, ev_q_norm[e], rope, k_l, v_l, k_c, v_c, TQ_ATTN, TK_ATTN, cast=late_cast)
            else:
                att_l, = _attention(p_l, ev_q_norm[e], rope, k_l, v_l, k_c, v_c, TQ_ATTN, TK_ATTN)
            conv_l = _short_conv_mixer(p_l, conv_col, ev_conv_w[e])
            if ctx_full:
                att_c, = _attention(p_c, ev_q_norm[e], None, k_c, v_c, None, None, TQ_ATTN, TK_ATTN)
                conv_c = _short_conv_mixer(p_c, conv_col, ev_conv_w[e])
                xc, hc = _outproj([att_c, conv_c], ev_out, e, xc, modv, l, 1, g_ffn, TM_PROJ)
            xs, hs = _outproj([att_l, conv_l], ev_out, e, xs, modv, l, 0, g_ffn, TM_PROJ)
        else:
            o = l // 2
            fargs = (od_f_w1[o], od_f_b1[o], od_f_freq[o], od_f_w2[o], od_f_b2[o], od_f_w3[o], od_f_decay[o])
            p_l = _modmm(xs, g_mix, modv, l, 0, od_in, o, TM_IN, TN_IN, out_dtype=BF16)
            u, x0, usk = _hyena_prep(p_l, od_conv_w[o], od_conv_b[o], od_skip[o])
            filt2 = _hyena_filter(n, d, *fargs, tr=TR_FILT, tc=TC_FILT)
            y = _long_conv_gate(u, x0, usk, filt2, n)
            xs, hs = _outproj([y], od_out, o, xs, modv, l, 0, g_ffn, TM_PROJ)
            if ctx_full:
                p_c = _modmm(xc, g_mix, modv, l, 1, od_in, o, TM_IN, TN_IN, out_dtype=BF16)
                u, x0, usk = _hyena_prep(p_c, od_conv_w[o], od_conv_b[o], od_skip[o])
                filt2 = _hyena_filter(n_ctx, d, *fargs, tr=TR_FILT, tc=TC_FILT)
                y = _long_conv_gate_short(u, x0, usk, filt2, n_ctx)
                xc, hc = _outproj([y], od_out, o, xc, modv, l, 1, g_ffn, TM_PROJ)
        last = l == depth - 1
        xs = _ffn(xs, hs, modv, l, 0, w1_all, w2_all, TM_FFN, TF_FFN, final_gain=final_g if last else None)
        if ctx_full:
            xc = _ffn(xc, hc, modv, l, 1, w1_all, w2_all, TM_FFN, TF_FFN)

    return xs[None]
```
